```python
import math
import jax, jax.numpy as jnp
from jax import lax
import numpy as np

D_MODEL = 1024
BATCH = 8
SEQ = 4096
DEPTH = 1
DEC_BATCH = 32
DEC_SEQ = 4
PAST_LEN = 16384
PAGE_SIZE = 128

HEAD_DIM = 64
ATT_WIDTH = D_MODEL // 2
N_HEADS_A = ATT_WIDTH // HEAD_DIM
N_IDX_HEADS = 8
D_IDX = 64
IDX_ROPE = 32
ROPE_BASE = 10000.0
TOPK_MAX = 256
Q_BLOCK = 128
NUM_BUCKETS = 32
MAX_DISTANCE = 128
SSM_WIDTH = D_MODEL // 2
SSM_HEAD_DIM = 64
SSM_HEADS = SSM_WIDTH // SSM_HEAD_DIM
SSM_GROUPS = 2
D_STATE = 128
CONV_W = 4
CONV_CH = SSM_WIDTH + 2 * SSM_GROUPS * D_STATE
CHUNK = 128
MIX_WIDTH = ATT_WIDTH + SSM_WIDTH
EPS = 1e-6
SPLIT_WIDTHS = (ATT_WIDTH, ATT_WIDTH, ATT_WIDTH, ATT_WIDTH, N_IDX_HEADS * D_IDX, D_IDX, N_IDX_HEADS, SSM_WIDTH, CONV_CH, SSM_HEADS)
IN_COLS = sum(SPLIT_WIDTHS)

kernel_name = "hymba_dsa_ssd_sandwich_step"


def rmsnorm(x, g):
    xf = x.astype(jnp.float32)
    y = xf * lax.rsqrt(jnp.mean(xf * xf, axis=-1, keepdims=True) + EPS)
    return (y * g.astype(jnp.float32)).astype(x.dtype)


def split_proj(p):
    offs = np.cumsum(np.array(SPLIT_WIDTHS))[:-1].tolist()
    return jnp.split(p, offs, axis=-1)


def rope_angles(pos):
    inv = ROPE_BASE ** (-jnp.arange(0, IDX_ROPE, 2, dtype=jnp.float32) / IDX_ROPE)
    ang = pos.astype(jnp.float32)[:, None] * inv[None, :]
    return jnp.cos(ang), jnp.sin(ang)


def apply_rope(x, cos, sin):
    half = IDX_ROPE // 2
    x1, x2, rest = x[..., :half], x[..., half:IDX_ROPE], x[..., IDX_ROPE:]
    c, s = cos.astype(x.dtype), sin.astype(x.dtype)
    return jnp.concatenate([x1 * c - x2 * s, x2 * c + x1 * s, rest], axis=-1)


def rel_bucket(dist):
    max_exact = NUM_BUCKETS // 2
    n = jnp.maximum(dist, 0)
    nf = jnp.maximum(n, max_exact).astype(jnp.float32)
    large = max_exact + (jnp.log(nf / max_exact) / math.log(MAX_DISTANCE / max_exact) * (NUM_BUCKETS - max_exact)).astype(jnp.int32)
    large = jnp.minimum(large, NUM_BUCKETS - 1)
    return jnp.where(n < max_exact, n, large)


def index_scores(qi, wi, ki_all, q_pos, k_pos):
    dots = jnp.einsum('bthd,bsd->bths', qi.astype(jnp.float32), ki_all.astype(jnp.float32)) * (D_IDX ** -0.5)
    scores = jnp.einsum('bths,bth->bts', jax.nn.relu(dots), wi.astype(jnp.float32))
    causal = k_pos[None, :] <= q_pos[:, None]
    return jnp.where(causal[None], scores, -jnp.inf)


def attend_selected(q, ksel, vsel, q_pos, sel_pos, rel_bias):
    dist = q_pos[None, :, None] - sel_pos
    valid = dist >= 0
    bias = jnp.swapaxes(rel_bias.astype(jnp.float32)[rel_bucket(dist)], -1, -2)
    logits = jnp.einsum('bthd,btkhd->bthk', q.astype(jnp.float32), ksel.astype(jnp.float32)) * (HEAD_DIM ** -0.5) + bias
    logits = jnp.where(valid[:, :, None, :], logits, -jnp.inf)
    probs = jax.nn.softmax(logits, axis=-1)
    return jnp.einsum('bthk,btkhd->bthd', probs, vsel.astype(jnp.float32)).astype(q.dtype)


def prompt_attention(q, k, v, qi, wi, ki, rel_bias):
    B, S = q.shape[0], q.shape[1]
    n_top = min(TOPK_MAX, S // 4)
    nb = S // Q_BLOCK
    k_pos = jnp.arange(S, dtype=jnp.int32)
    bi = jnp.arange(B)[:, None, None]

    def blocks(a):
        return jnp.swapaxes(a.reshape((B, nb, Q_BLOCK) + a.shape[2:]), 0, 1)

    def one_block(args):
        qb, qib, wib, start = args
        q_pos = start + jnp.arange(Q_BLOCK, dtype=jnp.int32)
        sc = index_scores(qib, wib, ki, q_pos, k_pos)
        _, sel = lax.top_k(sc, n_top)
        sel = sel.astype(jnp.int32)
        return attend_selected(qb, k[bi, sel], v[bi, sel], q_pos, sel, rel_bias)

    starts = jnp.arange(nb, dtype=jnp.int32) * Q_BLOCK
    out = lax.map(one_block, (blocks(q), blocks(qi), blocks(wi), starts))
    return jnp.swapaxes(out, 0, 1).reshape(B, S, N_HEADS_A, HEAD_DIM)


def sample_attention(q, k_new, v_new, qi, wi, ki_new, layer, cache_k, cache_v, cache_kidx, page_table, rel_bias):
    B, T = q.shape[0], q.shape[1]
    P = page_table.shape[1] * PAGE_SIZE
    L = P + T
    n_top = min(TOPK_MAX, L // 4)
    bi = jnp.arange(B)[:, None, None]
    ki_past = cache_kidx[layer, page_table].reshape(B, P, D_IDX)
    ki_all = jnp.concatenate([ki_past, ki_new.astype(ki_past.dtype)], axis=1)
    q_pos = P + jnp.arange(T, dtype=jnp.int32)
    k_pos = jnp.arange(L, dtype=jnp.int32)
    sc = index_scores(qi, wi, ki_all, q_pos, k_pos)
    _, sel = lax.top_k(sc, n_top)
    sel = sel.astype(jnp.int32)
    in_past = (sel < P)[..., None, None]
    pidx = jnp.minimum(sel, P - 1)
    phys = page_table[bi, pidx // PAGE_SIZE]
    off = pidx % PAGE_SIZE
    nidx = jnp.clip(sel - P, 0, T - 1)
    ksel = jnp.where(in_past, cache_k[layer, phys, off], k_new[bi, nidx].astype(cache_k.dtype))
    vsel = jnp.where(in_past, cache_v[layer, phys, off], v_new[bi, nidx].astype(cache_v.dtype))
    return attend_selected(q, ksel, vsel, q_pos, sel, rel_bias)


def segsum(x):
    T = x.shape[-1]
    xr = jnp.broadcast_to(x[..., None], x.shape + (T,))
    xr = jnp.where(jnp.tril(jnp.ones((T, T), bool), -1), xr, 0.0)
    cs = jnp.cumsum(xr, axis=-2)
    return jnp.where(jnp.tril(jnp.ones((T, T), bool)), cs, -jnp.inf)


def ssd_scan(x, dt, A, Bm, Cm, h0):
    Bsz, T = x.shape[0], x.shape[1]
    q = min(CHUNK, T)
    pad = (-T) % q
    if pad:
        pw = ((0, 0), (0, pad), (0, 0), (0, 0))
        x, Bm, Cm = jnp.pad(x, pw), jnp.pad(Bm, pw), jnp.pad(Cm, pw)
        dt = jnp.pad(dt, ((0, 0), (0, pad), (0, 0)))
    nc = (T + pad) // q
    rep = SSM_HEADS // SSM_GROUPS
    Bh = jnp.repeat(Bm.astype(jnp.float32), rep, axis=2).reshape(Bsz, nc, q, SSM_HEADS, D_STATE)
    Ch = jnp.repeat(Cm.astype(jnp.float32), rep, axis=2).reshape(Bsz, nc, q, SSM_HEADS, D_STATE)
    X = (x.astype(jnp.float32) * dt[..., None]).reshape(Bsz, nc, q, SSM_HEADS, SSM_HEAD_DIM)
    Adt = (dt * A).reshape(Bsz, nc, q, SSM_HEADS).transpose(0, 3, 1, 2)
    A_cs = jnp.cumsum(Adt, axis=-1)
    Lm = jnp.exp(segsum(Adt))
    CB = jnp.einsum('bclhn,bcshn->bhcls', Ch, Bh)
    y_diag = jnp.einsum('bhcls,bcshp->bclhp', CB * Lm, X)
    decay_states = jnp.exp(A_cs[..., -1:] - A_cs)
    states = jnp.einsum('bclhn,bhcl,bclhp->bchpn', Bh, decay_states, X)
    states = jnp.concatenate([h0.astype(jnp.float32)[:, None], states], axis=1)
    chunk_decay = jnp.exp(segsum(jnp.pad(A_cs[..., -1], ((0, 0), (0, 0), (1, 0)))))
    new_states = jnp.einsum('bhzc,bchpn->bzhpn', chunk_decay, states)
    states, h_final = new_states[:, :-1], new_states[:, -1]
    y_off = jnp.einsum('bclhn,bchpn,bhcl->bclhp', Ch, states, jnp.exp(A_cs))
    y = (y_diag + y_off).reshape(Bsz, nc * q, SSM_HEADS, SSM_HEAD_DIM)[:, :T]
    return y, h_final


def causal_conv(xbc, prev, w, b):
    T = xbc.shape[1]
    xp = jnp.concatenate([prev.astype(xbc.dtype), xbc], axis=1)
    out = b.astype(xbc.dtype)
    for i in range(CONV_W):
        out = out + xp[:, i:i + T] * w[i].astype(xbc.dtype)
    return jax.nn.silu(out), xp[:, -(CONV_W - 1):]


def mixer_layer(x, pos, attn_fn, conv_prev, h0, g_pre, w_in, conv_w, conv_b, dt_bias, a_log, d_skip, g_ssm, w_out, g_post):
    B, T, _ = x.shape
    h = rmsnorm(x, g_pre)
    q, k, v, ga, qi, ki, wi, z, xbc, dt = split_proj(h @ w_in)
    q = q.reshape(B, T, N_HEADS_A, HEAD_DIM)
    k = k.reshape(B, T, N_HEADS_A, HEAD_DIM)
    v = v.reshape(B, T, N_HEADS_A, HEAD_DIM)
    cos, sin = rope_angles(pos)
    qi = apply_rope(qi.reshape(B, T, N_IDX_HEADS, D_IDX), cos[:, None, :], sin[:, None, :])
    ki = apply_rope(ki, cos, sin)
    wi = wi * (N_IDX_HEADS ** -0.5)
    att = attn_fn(q, k, v, qi, wi, ki).reshape(B, T, ATT_WIDTH) * jax.nn.silu(ga)
    xbc_c, conv_state = causal_conv(xbc, conv_prev, conv_w, conv_b)
    xs = xbc_c[..., :SSM_WIDTH].reshape(B, T, SSM_HEADS, SSM_HEAD_DIM)
    Bm = xbc_c[..., SSM_WIDTH:SSM_WIDTH + SSM_GROUPS * D_STATE].reshape(B, T, SSM_GROUPS, D_STATE)
    Cm = xbc_c[..., SSM_WIDTH + SSM_GROUPS * D_STATE:].reshape(B, T, SSM_GROUPS, D_STATE)
    dtf = jax.nn.softplus(dt.astype(jnp.float32) + dt_bias.astype(jnp.float32))
    A = -jnp.exp(a_log.astype(jnp.float32))
    y, h_final = ssd_scan(xs, dtf, A, Bm, Cm, h0)
    y = y + d_skip.astype(jnp.float32)[:, None] * xs.astype(jnp.float32)
    ssm = rmsnorm(y.reshape(B, T, SSM_WIDTH) * jax.nn.silu(z.astype(jnp.float32)), g_ssm).astype(x.dtype)
    out = jnp.concatenate([att.astype(x.dtype), ssm], axis=-1) @ w_out
    y_out = x + rmsnorm(out, g_post)
    return y_out, k, v, ki, h_final.astype(x.dtype), conv_state


def setup_inputs(seed: int = 0) -> dict:
    key = jax.random.key(seed)
    ks = jax.random.split(key, 20)
    f32 = jnp.float32
    n_pages = PAST_LEN // PAGE_SIZE
    n_pool = (5 * DEC_BATCH * n_pages + 3) // 4

    def nrm(k, shape, s):
        return jax.random.normal(k, shape, f32) * s

    dt0 = jnp.exp(jax.random.uniform(ks[12], (DEPTH, SSM_HEADS), f32, math.log(1e-3), math.log(1e-1)))
    return {
        "x_prompt": nrm(ks[0], (BATCH, SEQ, D_MODEL), 1.0),
        "x_sample": nrm(ks[1], (DEC_BATCH, DEC_SEQ, D_MODEL), 1.0),
        "cache_k": nrm(ks[2], (DEPTH, n_pool, PAGE_SIZE, N_HEADS_A, HEAD_DIM), 1.0),
        "cache_v": nrm(ks[3], (DEPTH, n_pool, PAGE_SIZE, N_HEADS_A, HEAD_DIM), 1.0),
        "cache_kidx": nrm(ks[4], (DEPTH, n_pool, PAGE_SIZE, D_IDX), 1.0),
        "state_ssm": nrm(ks[5], (DEPTH, DEC_BATCH, SSM_HEADS, SSM_HEAD_DIM, D_STATE), 0.5),
        "state_conv": nrm(ks[6], (DEPTH, DEC_BATCH, CONV_W - 1, CONV_CH), 1.0),
        "page_table": jax.random.permutation(ks[7], n_pool)[:DEC_BATCH * n_pages].reshape(DEC_BATCH, n_pages).astype(jnp.int32),
        "g_pre": 1.0 + nrm(ks[8], (DEPTH, D_MODEL), 0.05),
        "w_in": nrm(ks[9], (DEPTH, D_MODEL, IN_COLS), D_MODEL ** -0.5),
        "conv_w": nrm(ks[10], (DEPTH, CONV_W, CONV_CH), CONV_W ** -0.5),
        "conv_b": nrm(ks[11], (DEPTH, CONV_CH), 0.02),
        "dt_bias": dt0 + jnp.log(-jnp.expm1(-dt0)),
        "a_log": jnp.log(jax.random.uniform(ks[13], (DEPTH, SSM_HEADS), f32, 1.0, 16.0)),
        "d_skip": 1.0 + nrm(ks[14], (DEPTH, SSM_HEADS), 0.1),
        "g_ssm": 1.0 + nrm(ks[15], (DEPTH, SSM_WIDTH), 0.05),
        "w_out": nrm(ks[16], (DEPTH, MIX_WIDTH, D_MODEL), MIX_WIDTH ** -0.5),
        "g_post": 1.0 + nrm(ks[17], (DEPTH, D_MODEL), 0.05),
        "rel_bias": nrm(ks[18], (NUM_BUCKETS, N_HEADS_A), 0.5),
    }


def reference(x_prompt, x_sample, cache_k, cache_v, cache_kidx, state_ssm, state_conv, page_table, g_pre, w_in, conv_w, conv_b, dt_bias, a_log, d_skip, g_ssm, w_out, g_post, rel_bias):
    Bp, Sp = x_prompt.shape[0], x_prompt.shape[1]
    Ts = x_sample.shape[1]
    past = page_table.shape[1] * PAGE_SIZE
    pos_p = jnp.arange(Sp, dtype=jnp.int32)
    pos_s = past + jnp.arange(Ts, dtype=jnp.int32)
    conv0 = jnp.zeros((Bp, CONV_W - 1, CONV_CH), x_prompt.dtype)
    h00 = jnp.zeros((Bp, SSM_HEADS, SSM_HEAD_DIM, D_STATE), jnp.float32)

    def prompt_fn(q, k, v, qi, wi, ki):
        return prompt_attention(q, k, v, qi, wi, ki, rel_bias)

    yp, ys = x_prompt, x_sample
    kp_l, vp_l, kip_l, hp_l, cp_l = [], [], [], [], []
    ks_l, vs_l, kis_l, hs_l, cs_l = [], [], [], [], []
    for l in range(DEPTH):
        lw = (g_pre[l], w_in[l], conv_w[l], conv_b[l], dt_bias[l], a_log[l], d_skip[l], g_ssm[l], w_out[l], g_post[l])

        def sample_fn(q, k, v, qi, wi, ki, layer=l):
            return sample_attention(q, k, v, qi, wi, ki, layer, cache_k, cache_v, cache_kidx, page_table, rel_bias)

        yp, kp, vp, kip, hp, cp = mixer_layer(yp, pos_p, prompt_fn, conv0, h00, *lw)
        ys, ksm, vsm, kism, hsm, csm = mixer_layer(ys, pos_s, sample_fn, state_conv[l], state_ssm[l], *lw)
        kp_l.append(kp); vp_l.append(vp); kip_l.append(kip); hp_l.append(hp); cp_l.append(cp)
        ks_l.append(ksm); vs_l.append(vsm); kis_l.append(kism); hs_l.append(hsm); cs_l.append(csm)
    return (yp, ys, jnp.stack(kp_l), jnp.stack(vp_l), jnp.stack(kip_l), jnp.stack(hp_l), jnp.stack(cp_l), jnp.stack(ks_l), jnp.stack(vs_l), jnp.stack(kis_l), jnp.stack(hs_l), jnp.stack(cs_l))
```

```python
import functools
import math

import jax
import jax.numpy as jnp
import numpy as np
from jax import lax
from jax.experimental import pallas as pl
from jax.experimental.pallas import tpu as pltpu

F32 = jnp.float32
BF16 = jnp.bfloat16
I32 = jnp.int32
HIGHEST = lax.Precision.HIGHEST

D_MODEL = 1024
PAGE_SIZE = 128
HEAD_DIM = 64
ATT_WIDTH = 512
N_HEADS_A = 8
N_IDX_HEADS = 8
D_IDX = 64
IDX_ROPE = 32
ROPE_BASE = 10000.0
TOPK_MAX = 256
NUM_BUCKETS = 32
MAX_DISTANCE = 128
SSM_WIDTH = 512
SSM_HEAD_DIM = 64
SSM_HEADS = 8
SSM_GROUPS = 2
D_STATE = 128
CONV_W = 4
CONV_CH = 1024
CHUNK = 128
EPS = 1e-6

LANES = 128
SUBLANES = 8
VMEM_LIMIT = 56 * 1024 * 1024
NEG = -1e30
INT_MIN = -2 ** 31

SM_WI = D_IDX
SM_DT = D_IDX + N_IDX_HEADS
MAIN_COLS = 4 * ATT_WIDTH + SSM_WIDTH + CONV_CH
IDX_COLS = N_IDX_HEADS * D_IDX + LANES


def _nt(a, b, **kw):
    return lax.dot_general(a, b, (((1,), (1,)), ((), ())), preferred_element_type=F32, **kw)


def _sigmoid(x):
    return 1.0 / (1.0 + jnp.exp(-x))


def _cparams(sem):
    return pltpu.CompilerParams(dimension_semantics=sem, vmem_limit_bytes=VMEM_LIMIT)


def _rope_table_kernel(inv_ref, cos_ref, sin_ref, *, pos_off):
    rows = cos_ref.shape[0]
    pos = (lax.broadcasted_iota(I32, (rows, LANES), 0) + pos_off).astype(F32)
    ang = pos * inv_ref[...]
    cos_ref[...] = jnp.cos(ang)
    sin_ref[...] = jnp.sin(ang)


def _rope_tables(rows, pos_off):
    inv = ROPE_BASE ** (-jnp.arange(0, IDX_ROPE, 2, dtype=F32) / IDX_ROPE)
    l64 = np.arange(LANES) % D_IDX
    inv_row = jnp.where(l64 < IDX_ROPE, inv[l64 % (IDX_ROPE // 2)], 0.0).astype(F32)[None, :]
    return pl.pallas_call(
        functools.partial(_rope_table_kernel, pos_off=pos_off),
        out_shape=(jax.ShapeDtypeStruct((rows, LANES), F32),) * 2,
    )(inv_row)


def _rope_tile(x, c, s1, s2):
    return x * c + pltpu.roll(x, LANES - IDX_ROPE // 2, 1) * s1 + pltpu.roll(x, IDX_ROPE // 2, 1) * s2


def _inproj_kernel(x_ref, g_ref, wm_ref, wi_ref, cos_ref, sin_ref,
                   qe_ref, qo_ref, k_ref, kb_ref, v_ref, vb_ref, ga_ref, z_ref, xbc_ref, qi_ref, sm_ref):
    x = x_ref[...]
    hn = x * lax.rsqrt(jnp.mean(x * x, axis=-1, keepdims=True) + EPS) * g_ref[...]
    hb = hn.astype(BF16)

    def main(lo, width):
        return jnp.dot(hb, wm_ref[:, lo:lo + width], preferred_element_type=F32)

    q = main(0, ATT_WIDTH) * (HEAD_DIM ** -0.5)
    even = (lax.broadcasted_iota(I32, q.shape, 1) & HEAD_DIM) == 0
    qe_ref[...] = jnp.where(even, q, 0.0).astype(BF16)
    qo_ref[...] = jnp.where(even, 0.0, q).astype(BF16)
    k = main(ATT_WIDTH, ATT_WIDTH)
    k_ref[...] = k
    kb_ref[...] = k.astype(BF16)
    v = main(2 * ATT_WIDTH, ATT_WIDTH)
    v_ref[...] = v
    vb_ref[...] = v.astype(BF16)
    ga_ref[...] = main(3 * ATT_WIDTH, ATT_WIDTH)
    z_ref[...] = main(4 * ATT_WIDTH, SSM_WIDTH)
    xbc_ref[...] = main(4 * ATT_WIDTH + SSM_WIDTH, CONV_CH)

    idx = jnp.dot(hn, wi_ref[...], precision=HIGHEST, preferred_element_type=F32)
    tm = x.shape[0]
    lane = lax.broadcasted_iota(I32, (tm, LANES), 1)
    l64 = lane & (D_IDX - 1)
    c = cos_ref[...]
    s = sin_ref[...]
    first = l64 < IDX_ROPE // 2
    s1 = jnp.where(first, -s, 0.0)
    s2 = jnp.where(first, 0.0, s)
    for j in range(N_IDX_HEADS * D_IDX // LANES):
        qi_ref[:, j * LANES:(j + 1) * LANES] = _rope_tile(idx[:, j * LANES:(j + 1) * LANES], c, s1, s2)
    is_ki = lane < D_IDX
    sm = _rope_tile(idx[:, N_IDX_HEADS * D_IDX:], jnp.where(is_ki, c, 1.0),
                    jnp.where(is_ki, s1, 0.0), jnp.where(is_ki, s2, 0.0))
    is_wi = (lane >= SM_WI) & (lane < SM_DT)
    sm_ref[...] = jnp.where(is_wi, sm * (N_IDX_HEADS ** -0.5), sm)


def _inproj(x2d, g_pre, w_main, w_idx, cos_t, sin_t, tm):
    n = x2d.shape[0]
    tab_blocks = cos_t.shape[0] // tm
    row = lambda i: (i, 0)
    const = lambda i: (0, 0)
    tab = lambda i: (i % tab_blocks, 0)
    wide = lambda w, dt: jax.ShapeDtypeStruct((n, w), dt)
    return pl.pallas_call(
        _inproj_kernel,
        grid=(n // tm,),
        in_specs=[pl.BlockSpec((tm, D_MODEL), row), pl.BlockSpec((1, D_MODEL), const),
                  pl.BlockSpec((D_MODEL, MAIN_COLS), const), pl.BlockSpec((D_MODEL, IDX_COLS), const),
                  pl.BlockSpec((tm, LANES), tab), pl.BlockSpec((tm, LANES), tab)],
        out_specs=[pl.BlockSpec((tm, ATT_WIDTH), row)] * 8 + [pl.BlockSpec((tm, CONV_CH), row),
                   pl.BlockSpec((tm, N_IDX_HEADS * D_IDX), row), pl.BlockSpec((tm, LANES), row)],
        out_shape=[wide(ATT_WIDTH, BF16), wide(ATT_WIDTH, BF16), wide(ATT_WIDTH, F32), wide(ATT_WIDTH, BF16), wide(ATT_WIDTH, F32),
                   wide(ATT_WIDTH, BF16), wide(ATT_WIDTH, F32), wide(SSM_WIDTH, F32), wide(CONV_CH, F32),
                   wide(N_IDX_HEADS * D_IDX, F32), wide(LANES, F32)],
        compiler_params=_cparams(("arbitrary",)),
    )(x2d, g_pre, w_main, w_idx, cos_t, sin_t)


def _bucket(dist):
    max_exact = NUM_BUCKETS // 2
    n = jnp.maximum(dist, 0)
    nf = jnp.maximum(n, max_exact).astype(F32)
    large = max_exact + (jnp.log(nf / max_exact) / math.log(MAX_DISTANCE / max_exact)
                         * (NUM_BUCKETS - max_exact)).astype(I32)
    large = jnp.minimum(large, NUM_BUCKETS - 1)
    return jnp.where(n < max_exact, n, large)


def _bias_lookup(bucket, relb_ref, h):
    out = jnp.full(bucket.shape, relb_ref[0, h], F32)
    for b in range(1, NUM_BUCKETS):
        out = jnp.where(bucket == b, relb_ref[b, h], out)
    return out


def _ordered_key(score):
    bits = pltpu.bitcast(score + 0.0, I32)
    return jnp.where(bits < 0, bits ^ 0x7FFFFFFF, bits)


def _topk_threshold(count_fn, n_top, pos_bits, shape):
    k = float(n_top)

    def thr_body(it, tu):
        cand_u = tu | lax.shift_left(jnp.int32(1), 31 - it)
        cand = cand_u ^ INT_MIN
        cnt = count_fn(lambda key, pos: key >= cand)
        return jnp.where(cnt >= k, cand_u, tu)

    thr = lax.fori_loop(0, 32, thr_body, jnp.zeros(shape, I32)) ^ INT_MIN
    need = k - count_fn(lambda key, pos: key > thr)

    def last_body(it, q):
        cand = q | lax.shift_left(jnp.int32(1), pos_bits - 1 - it)
        cnt = count_fn(lambda key, pos: (key == thr) & (pos < cand))
        return jnp.where(cnt < need, cand, q)

    last = lax.fori_loop(0, pos_bits, last_body, jnp.zeros(shape, I32))
    return thr, last


def _select_madd(key, pos, thr, last):
    return jnp.where(key > thr, 0.0, jnp.where(key == thr, jnp.where(pos <= last, 0.0, NEG), NEG))


def _bias_tiles_kernel(relb_ref, o_ref, *, tq):
    ti = lax.broadcasted_iota(I32, (tq, tq), 0)
    ki = lax.broadcasted_iota(I32, (tq, tq), 1)
    for kind in range(2):
        bucket = _bucket(ti - ki + kind * tq)
        for h in range(N_HEADS_A):
            o_ref[h, kind] = _bias_lookup(bucket, relb_ref, h)


def _bias_tiles(rel_bias, tq):
    return pl.pallas_call(
        functools.partial(_bias_tiles_kernel, tq=tq),
        in_specs=[pl.BlockSpec(memory_space=pltpu.SMEM)],
        out_shape=jax.ShapeDtypeStruct((N_HEADS_A, 2, tq, tq), F32),
        compiler_params=pltpu.CompilerParams(vmem_limit_bytes=VMEM_LIMIT),
    )(rel_bias)


def _pattn_kernel(relb_ref, qi_ref, smq_ref, kis_ref, qe_ref, qo_ref, k_ref, v_ref, bt_ref, o_ref,
                  qh_ref, keys_ref, madd_ref, m_ref, l_ref, acc_ref, *, tq, n_top, pos_bits):
    i = pl.program_id(1)
    nch = i + 1
    lane = lax.broadcasted_iota(I32, (tq, LANES), 1)
    low = lane < HEAD_DIM
    row_pos = i * tq + lax.broadcasted_iota(I32, (tq, tq), 0)
    col = lax.broadcasted_iota(I32, (tq, tq), 1)

    def chunk(c):
        return pl.ds(pl.multiple_of(c * tq, tq), tq)

    sm = smq_ref[...]
    for h in range(N_IDX_HEADS):
        pair = qi_ref[:, (h // 2) * LANES:(h // 2 + 1) * LANES]
        if h % 2:
            pair = pltpu.roll(pair, HEAD_DIM, 1)
        qh_ref[h] = jnp.where(low, pair, 0.0)

    def score_body(c, carry):
        kc = kis_ref[chunk(c), :]
        sc = jnp.zeros((tq, tq), F32)
        for h in range(N_IDX_HEADS):
            d = _nt(qh_ref[h], kc, precision=HIGHEST)
            sc = sc + jnp.maximum(d * (D_IDX ** -0.5), 0.0) * sm[:, SM_WI + h:SM_WI + h + 1]
        sc = jnp.where(c * tq + col <= row_pos, sc, -jnp.inf)
        keys_ref[:, chunk(c)] = _ordered_key(sc)
        return carry

    lax.fori_loop(0, nch, score_body, 0)

    def count_fn(pred):
        def body(c, acc):
            for t in range(tq // LANES):
                key = keys_ref[:, pl.ds(pl.multiple_of(c * tq + t * LANES, LANES), LANES)]
                acc = acc + jnp.where(pred(key, c * tq + t * LANES + lane), 1.0, 0.0)
            return acc
        acc = lax.fori_loop(0, nch, body, jnp.zeros((tq, LANES), F32))
        return jnp.broadcast_to(jnp.sum(acc, axis=1, keepdims=True), (tq, LANES))

    thr, last = _topk_threshold(count_fn, n_top, pos_bits, (tq, LANES))

    def madd_body(c, carry):
        for t in range(tq // LANES):
            sl = pl.ds(pl.multiple_of(c * tq + t * LANES, LANES), LANES)
            pos = c * tq + t * LANES + lane
            madd = _select_madd(keys_ref[:, sl], pos, thr, last)
            madd_ref[:, sl] = jnp.where(pos <= i * tq + lax.broadcasted_iota(I32, (tq, LANES), 0), madd, NEG)
        return carry

    lax.fori_loop(0, nch, madd_body, 0)

    m_ref[...] = jnp.full(m_ref.shape, NEG, F32)
    l_ref[...] = jnp.zeros(l_ref.shape, F32)
    acc_ref[...] = jnp.zeros(acc_ref.shape, F32)

    def attend(c, bias_of_head):
        madd = madd_ref[:, chunk(c)]
        for h in range(N_HEADS_A):
            p2 = h // 2
            qh = (qe_ref if h % 2 == 0 else qo_ref)[:, p2 * LANES:(p2 + 1) * LANES]
            s = _nt(qh, k_ref[chunk(c), p2 * LANES:(p2 + 1) * LANES]) + bias_of_head(h) + madd
            m_prev = m_ref[h]
            m_new = jnp.maximum(m_prev, jnp.broadcast_to(jnp.max(s, axis=1, keepdims=True), (tq, LANES)))
            alpha = jnp.exp(m_prev - m_new)
            p = jnp.exp(s - m_new[:, :1])
            l_ref[h] = alpha * l_ref[h] + jnp.broadcast_to(jnp.sum(p, axis=1, keepdims=True), (tq, LANES))
            pv = jnp.dot(p.astype(BF16), v_ref[chunk(c), p2 * LANES:(p2 + 1) * LANES],
                         preferred_element_type=F32)
            acc_ref[h] = alpha * acc_ref[h] + pv
            m_ref[h] = m_new

    def far_body(c, carry):
        attend(c, lambda h: relb_ref[NUM_BUCKETS - 1, h])
        return carry

    lax.fori_loop(0, jnp.maximum(i - 1, 0), far_body, 0)

    @pl.when(i >= 1)
    def _():
        attend(i - 1, lambda h: bt_ref[h, 1])

    attend(i, lambda h: bt_ref[h, 0])

    for p2 in range(N_HEADS_A // 2):
        even = acc_ref[2 * p2] / l_ref[2 * p2]
        odd = acc_ref[2 * p2 + 1] / l_ref[2 * p2 + 1]
        o_ref[:, p2 * LANES:(p2 + 1) * LANES] = jnp.where(low, even, odd)


def _prompt_attention(rel_bias, qi, small, q_e, q_o, k_b, v_b, bias_t, nb, s, tq):
    nq = s // tq
    n_top = min(TOPK_MAX, s // 4)
    pos_bits = max(1, (s - 1).bit_length())
    qblk = lambda b, i: (b * nq + i, 0)
    seq = lambda b, i: (b, 0)
    return pl.pallas_call(
        functools.partial(_pattn_kernel, tq=tq, n_top=n_top, pos_bits=pos_bits),
        grid=(nb, nq),
        in_specs=[pl.BlockSpec(memory_space=pltpu.SMEM),
                  pl.BlockSpec((tq, N_IDX_HEADS * D_IDX), qblk), pl.BlockSpec((tq, LANES), qblk),
                  pl.BlockSpec((s, LANES), seq), pl.BlockSpec((tq, ATT_WIDTH), qblk), pl.BlockSpec((tq, ATT_WIDTH), qblk),
                  pl.BlockSpec((s, ATT_WIDTH), seq), pl.BlockSpec((s, ATT_WIDTH), seq),
                  pl.BlockSpec((N_HEADS_A, 2, tq, tq), lambda b, i: (0, 0, 0, 0))],
        out_specs=pl.BlockSpec((tq, ATT_WIDTH), qblk),
        out_shape=jax.ShapeDtypeStruct((nb * s, ATT_WIDTH), F32),
        scratch_shapes=[pltpu.VMEM((N_IDX_HEADS, tq, LANES), F32), pltpu.VMEM((tq, s), I32),
                        pltpu.VMEM((tq, s), F32), pltpu.VMEM((N_HEADS_A, tq, LANES), F32),
                        pltpu.VMEM((N_HEADS_A, tq, LANES), F32), pltpu.VMEM((N_HEADS_A, tq, LANES), F32)],
        compiler_params=_cparams(("arbitrary", "arbitrary")),
    )(rel_bias, qi, small, small, q_e, q_o, k_b, v_b, bias_t)


def _ssd_kernel(xbc_ref, z_ref, sm_ref, cw_ref, cb_ref, dtb_ref, alog_ref, dsk_ref, gs_ref, ex_ref, ext_ref,
                h0_ref, c0_ref, y_ref, hf_ref, xp_ref, st_ref, *, t_valid):
    c = pl.program_id(1)
    L = CHUNK

    @pl.when(c == 0)
    def _():
        st_ref[...] = h0_ref[...]
        xp_ref[0:SUBLANES, :] = c0_ref[...]

    xp_ref[SUBLANES:SUBLANES + L, :] = xbc_ref[...]
    conv = cb_ref[...]
    for j in range(CONV_W):
        lo = SUBLANES - (CONV_W - 1) + j
        conv = conv + xp_ref[lo:lo + L, :] * cw_ref[j:j + 1, :]
    xp_ref[0:SUBLANES, :] = xp_ref[L:L + SUBLANES, :]
    act = conv * _sigmoid(conv)
    xs = act[:, :SSM_WIDTH]
    bm = act[:, SSM_WIDTH:SSM_WIDTH + SSM_GROUPS * D_STATE]
    cm = act[:, SSM_WIDTH + SSM_GROUPS * D_STATE:]

    raw = sm_ref[...] + dtb_ref[...]
    dtf = jnp.maximum(raw, 0.0) + jnp.log1p(jnp.exp(-jnp.abs(raw)))
    row = lax.broadcasted_iota(I32, (L, LANES), 0)
    if t_valid < L:
        dtf = jnp.where(row < t_valid, dtf, 0.0)
    adt = dtf * (-jnp.exp(alog_ref[...]))
    tril = row >= lax.broadcasted_iota(I32, (L, LANES), 1)
    cs = jnp.dot(jnp.where(tril, 1.0, 0.0), adt, precision=HIGHEST, preferred_element_type=F32)
    ex = ex_ref[...]
    dtx = jnp.dot(dtf, ex, precision=HIGHEST, preferred_element_type=F32)
    csx = jnp.dot(cs, ex, precision=HIGHEST, preferred_element_type=F32)
    cst = cs.T
    x = xs * dtx
    w = x * jnp.exp(csx[L - 1:L, :] - csx)
    ecsx = jnp.exp(csx)
    dec = jnp.exp(jnp.dot(ext_ref[...], cst, precision=HIGHEST, preferred_element_type=F32)[:, L - 1:L])
    low = lax.broadcasted_iota(I32, (L, LANES), 1) < SSM_HEAD_DIM

    ys = []
    for p2 in range(SSM_HEADS // 2):
        g = (2 * p2) // (SSM_HEADS // SSM_GROUPS)
        cg = cm[:, g * D_STATE:(g + 1) * D_STATE].astype(BF16)
        bg = bm[:, g * D_STATE:(g + 1) * D_STATE].astype(BF16)
        cb_mat = _nt(cg, bg)
        lanes = slice(p2 * LANES, (p2 + 1) * LANES)
        xp = x[:, lanes].astype(BF16)
        yd = []
        for h in (2 * p2, 2 * p2 + 1):
            diff = cs[:, SM_DT + h:SM_DT + h + 1] - cst[SM_DT + h:SM_DT + h + 1, :]
            lm = jnp.exp(jnp.where(tril, diff, NEG))
            yd.append(jnp.dot((cb_mat * lm).astype(BF16), xp, preferred_element_type=F32))
        rows = slice(p2 * LANES, (p2 + 1) * LANES)
        st = st_ref[rows, :]
        y_off = _nt(cg, st.astype(BF16)) * ecsx[:, lanes]
        ys.append(jnp.where(low, yd[0], yd[1]) + y_off)
        upd = jnp.dot(w[:, lanes].T.astype(BF16), bg, preferred_element_type=F32)
        st_ref[rows, :] = st * dec[rows, :] + upd

    y = jnp.concatenate(ys, axis=1) + dsk_ref[...] * xs
    zz = z_ref[...]
    gated = y * (zz * _sigmoid(zz))
    y_ref[...] = gated * lax.rsqrt(jnp.mean(gated * gated, axis=-1, keepdims=True) + EPS) * gs_ref[...]

    @pl.when(c == pl.num_programs(1) - 1)
    def _():
        hf_ref[...] = st_ref[...]


def _ssd(xbc, z, small, conv_w, conv_b, dt_bias, a_log, d_skip, g_ssm, h0, c0, nb, s, t_valid):
    nc = s // CHUNK
    blk = lambda b, c: (b * nc + c, 0)
    const = lambda b, c: (0, 0)
    per_b = lambda b, c: (b, 0, 0)
    lanes = np.arange(LANES)
    dt_row = lambda v: jnp.zeros((1, LANES), F32).at[0, SM_DT:SM_DT + SSM_HEADS].set(v)
    expand = (lanes[:, None] == SM_DT + np.arange(SSM_WIDTH)[None, :] // SSM_HEAD_DIM).astype(np.float32)
    state_rows = SSM_HEADS * SSM_HEAD_DIM
    return pl.pallas_call(
        functools.partial(_ssd_kernel, t_valid=t_valid),
        grid=(nb, nc),
        in_specs=[pl.BlockSpec((CHUNK, CONV_CH), blk), pl.BlockSpec((CHUNK, SSM_WIDTH), blk),
                  pl.BlockSpec((CHUNK, LANES), blk),
                  pl.BlockSpec((CONV_W, CONV_CH), const), pl.BlockSpec((1, CONV_CH), const),
                  pl.BlockSpec((1, LANES), const), pl.BlockSpec((1, LANES), const),
                  pl.BlockSpec((1, SSM_WIDTH), const), pl.BlockSpec((1, SSM_WIDTH), const),
                  pl.BlockSpec((LANES, SSM_WIDTH), const), pl.BlockSpec((SSM_WIDTH, LANES), const),
                  pl.BlockSpec((None, state_rows, D_STATE), per_b),
                  pl.BlockSpec((None, SUBLANES, CONV_CH), per_b)],
        out_specs=[pl.BlockSpec((CHUNK, SSM_WIDTH), blk), pl.BlockSpec((None, state_rows, D_STATE), per_b)],
        out_shape=[jax.ShapeDtypeStruct((nb * s, SSM_WIDTH), F32),
                   jax.ShapeDtypeStruct((nb, state_rows, D_STATE), F32)],
        scratch_shapes=[pltpu.VMEM((CHUNK + SUBLANES, CONV_CH), F32), pltpu.VMEM((state_rows, D_STATE), F32)],
        compiler_params=_cparams(("arbitrary", "arbitrary")),
    )(xbc, z, small, conv_w, conv_b[None, :], dt_row(dt_bias), dt_row(a_log),
      jnp.repeat(d_skip, SSM_HEAD_DIM)[None, :], g_ssm[None, :], jnp.asarray(expand), jnp.asarray(expand.T),
      h0, c0)


def _outproj_kernel(att_ref, ga_ref, ssm_ref, x_ref, wt_ref, wb_ref, gp_ref, o_ref):
    ga = ga_ref[...]
    att = att_ref[...] * (ga * _sigmoid(ga))
    out = (jnp.dot(att.astype(BF16), wt_ref[...], preferred_element_type=F32)
           + jnp.dot(ssm_ref[...].astype(BF16), wb_ref[...], preferred_element_type=F32))
    o_ref[...] = x_ref[...] + out * lax.rsqrt(jnp.mean(out * out, axis=-1, keepdims=True) + EPS) * gp_ref[...]


def _outproj(att, ga, ssm, x2d, w_top, w_bot, g_post, tm):
    n = x2d.shape[0]
    row = lambda i: (i, 0)
    const = lambda i: (0, 0)
    return pl.pallas_call(
        _outproj_kernel,
        grid=(n // tm,),
        in_specs=[pl.BlockSpec((tm, ATT_WIDTH), row), pl.BlockSpec((tm, ATT_WIDTH), row),
                  pl.BlockSpec((tm, SSM_WIDTH), row), pl.BlockSpec((tm, D_MODEL), row),
                  pl.BlockSpec((ATT_WIDTH, D_MODEL), const), pl.BlockSpec((SSM_WIDTH, D_MODEL), const),
                  pl.BlockSpec((1, D_MODEL), const)],
        out_specs=pl.BlockSpec((tm, D_MODEL), row),
        out_shape=jax.ShapeDtypeStruct((n, D_MODEL), F32),
        compiler_params=_cparams(("arbitrary",)),
    )(att, ga, ssm, x2d, w_top, w_bot, g_post)


PAGES_PER_STEP = 8
ROWS_Q = N_HEADS_A * SUBLANES


def _sscore_kernel(pt_ref, qall_ref, wcol_ref, knew_ref, *rest, past, t_new, n_top, pos_bits):
    pages = rest[:PAGES_PER_STEP]
    madd_ref = rest[PAGES_PER_STEP]
    keys_ref = rest[PAGES_PER_STEP + 1]
    j = pl.program_id(1)
    kw = PAGES_PER_STEP * PAGE_SIZE
    total = past + LANES
    qall = qall_ref[...]
    wcol = wcol_ref[:, 0:1]

    def scores(kc, q):
        r = jnp.maximum(_nt(q, kc, precision=HIGHEST) * (D_IDX ** -0.5), 0.0) * wcol
        sc = r[0:SUBLANES]
        for h in range(1, N_IDX_HEADS):
            sc = sc + r[h * SUBLANES:(h + 1) * SUBLANES]
        return sc

    kcat = jnp.concatenate([p[...] for p in pages], axis=0)
    keys_ref[:, pl.ds(pl.multiple_of(j * kw, kw), kw)] = _ordered_key(scores(kcat, qall[:, :D_IDX]))

    @pl.when(j == pl.num_programs(1) - 1)
    def _():
        lane = lax.broadcasted_iota(I32, (SUBLANES, LANES), 1)
        row = lax.broadcasted_iota(I32, (SUBLANES, LANES), 0)
        sc = scores(knew_ref[...], qall)
        vis = (lane <= row) & (lane < t_new)
        keys_ref[:, past:total] = _ordered_key(jnp.where(vis, sc, -jnp.inf))

        def count_fn(pred):
            acc = jnp.zeros((SUBLANES, LANES), F32)
            for t in range(total // LANES):
                acc = acc + jnp.where(pred(keys_ref[:, t * LANES:(t + 1) * LANES], t * LANES + lane), 1.0, 0.0)
            return jnp.broadcast_to(jnp.sum(acc, axis=1, keepdims=True), (SUBLANES, LANES))

        thr, last = _topk_threshold(count_fn, n_top, pos_bits, (SUBLANES, LANES))
        for t in range(total // LANES):
            sl = slice(t * LANES, (t + 1) * LANES)
            pos = t * LANES + lane
            madd = _select_madd(keys_ref[:, sl], pos, thr, last)
            if t * LANES >= past:
                madd = jnp.where(vis, madd, NEG)
            madd_ref[:, sl] = madd


def _sattn_kernel(pt_ref, relb_ref, qbd_ref, madd_ref, maddn_ref, knew_ref, vnew_ref, *rest, past):
    kpages = rest[:PAGES_PER_STEP]
    vpages = rest[PAGES_PER_STEP:2 * PAGES_PER_STEP]
    o_ref, m_ref, l_ref, acc_ref = rest[2 * PAGES_PER_STEP:]
    j = pl.program_id(1)
    kw = PAGES_PER_STEP * PAGE_SIZE

    @pl.when(j == 0)
    def _():
        m_ref[...] = jnp.full(m_ref.shape, NEG, F32)
        l_ref[...] = jnp.zeros(l_ref.shape, F32)
        acc_ref[...] = jnp.zeros(acc_ref.shape, F32)

    qbd = qbd_ref[...]

    def update(kc, vc, madd8, pos0):
        width = kc.shape[0]
        tok = lax.broadcasted_iota(I32, (SUBLANES, width), 0)
        pos = pos0 + lax.broadcasted_iota(I32, (SUBLANES, width), 1)
        bucket = _bucket(past + tok - pos)
        extra = jnp.concatenate([_bias_lookup(bucket, relb_ref, h) + madd8 for h in range(N_HEADS_A)], axis=0)
        s = _nt(qbd, kc.astype(BF16)) + extra
        m_prev = m_ref[...]
        m_new = jnp.maximum(m_prev, jnp.broadcast_to(jnp.max(s, axis=1, keepdims=True), (ROWS_Q, LANES)))
        alpha = jnp.exp(m_prev - m_new)
        p = jnp.exp(s - m_new[:, :1])
        l_ref[...] = alpha * l_ref[...] + jnp.broadcast_to(jnp.sum(p, axis=1, keepdims=True), (ROWS_Q, LANES))
        pv = jnp.dot(p.astype(BF16), vc.astype(BF16), preferred_element_type=F32)
        acc_ref[...] = alpha[:, :1] * acc_ref[...] + pv
        m_ref[...] = m_new

    update(jnp.concatenate([p[...] for p in kpages], axis=0), jnp.concatenate([p[...] for p in vpages], axis=0),
           madd_ref[...], j * kw)

    @pl.when(j == pl.num_programs(1) - 1)
    def _():
        update(knew_ref[...], vnew_ref[...], maddn_ref[...], past)
        o = acc_ref[...] / l_ref[:, :1]
        own = (lax.broadcasted_iota(I32, (ROWS_Q, ATT_WIDTH), 0) // SUBLANES
               == lax.broadcasted_iota(I32, (ROWS_Q, ATT_WIDTH), 1) // HEAD_DIM)
        o = jnp.where(own, o, 0.0)
        out = o[0:SUBLANES]
        for h in range(1, N_HEADS_A):
            out = out + o[h * SUBLANES:(h + 1) * SUBLANES]
        o_ref[...] = out


def _sample_attention(page_table, rel_bias, qi, small, q_s, k_new, v_new, cache_k, cache_v, cache_kidx, nb, t_new):
    n_pages = page_table.shape[1]
    past = n_pages * PAGE_SIZE
    nj = n_pages // PAGES_PER_STEP
    total = past + LANES
    n_top = min(TOPK_MAX, (past + t_new) // 4)
    pos_bits = max(1, (total - 1).bit_length())
    pt = page_table.reshape(-1)
    pool = cache_kidx.shape[0]

    tok = lambda a: a.reshape(nb, CHUNK, -1)[:, :SUBLANES]
    qi_t = tok(qi).reshape(nb, SUBLANES, N_IDX_HEADS, D_IDX).transpose(0, 2, 1, 3).reshape(nb, ROWS_Q, D_IDX)
    qall = jnp.pad(qi_t, ((0, 0), (0, 0), (0, LANES - D_IDX)))
    wi_t = tok(small)[:, :, SM_WI:SM_WI + N_IDX_HEADS].transpose(0, 2, 1).reshape(nb, ROWS_Q, 1)
    wcol = jnp.broadcast_to(wi_t, (nb, ROWS_Q, LANES))
    q_t = tok(q_s).reshape(nb, SUBLANES, N_HEADS_A, HEAD_DIM).transpose(0, 2, 1, 3)
    eye = jnp.eye(N_HEADS_A, dtype=q_t.dtype)
    qbd = (q_t[:, :, :, None, :] * eye[None, :, None, :, None]).reshape(nb, ROWS_Q, ATT_WIDTH)

    def page(r):
        return lambda b, j, pt_ref: (pt_ref[b * n_pages + j * PAGES_PER_STEP + r], 0, 0)

    seq3 = lambda b, j, pt_ref: (b, 0, 0)
    newblk = lambda b, j, pt_ref: (b, 0)

    madd = pl.pallas_call(
        functools.partial(_sscore_kernel, past=past, t_new=t_new, n_top=n_top, pos_bits=pos_bits),
        grid_spec=pltpu.PrefetchScalarGridSpec(
            num_scalar_prefetch=1, grid=(nb, nj),
            in_specs=[pl.BlockSpec((None, ROWS_Q, LANES), seq3), pl.BlockSpec((None, ROWS_Q, LANES), seq3),
                      pl.BlockSpec((CHUNK, LANES), newblk)]
                     + [pl.BlockSpec((None, PAGE_SIZE, D_IDX), page(r)) for r in range(PAGES_PER_STEP)],
            out_specs=pl.BlockSpec((None, SUBLANES, total), seq3),
            scratch_shapes=[pltpu.VMEM((SUBLANES, total), I32)]),
        out_shape=jax.ShapeDtypeStruct((nb, SUBLANES, total), F32),
        compiler_params=_cparams(("arbitrary", "arbitrary")),
    )(pt, qall, wcol, small, *([cache_kidx.reshape(pool, PAGE_SIZE, D_IDX)] * PAGES_PER_STEP))

    kw = PAGES_PER_STEP * PAGE_SIZE
    ck = cache_k.reshape(pool, PAGE_SIZE, ATT_WIDTH)
    cv = cache_v.reshape(pool, PAGE_SIZE, ATT_WIDTH)
    return pl.pallas_call(
        functools.partial(_sattn_kernel, past=past),
        grid_spec=pltpu.PrefetchScalarGridSpec(
            num_scalar_prefetch=1, grid=(nb, nj),
            in_specs=[pl.BlockSpec(memory_space=pltpu.SMEM),
                      pl.BlockSpec((None, ROWS_Q, ATT_WIDTH), seq3),
                      pl.BlockSpec((None, SUBLANES, kw), lambda b, j, pt_ref: (b, 0, j)),
                      pl.BlockSpec((None, SUBLANES, LANES), lambda b, j, pt_ref: (b, 0, past // LANES)),
                      pl.BlockSpec((CHUNK, ATT_WIDTH), newblk), pl.BlockSpec((CHUNK, ATT_WIDTH), newblk)]
                     + [pl.BlockSpec((None, PAGE_SIZE, ATT_WIDTH), page(r)) for r in range(PAGES_PER_STEP)] * 2,
            out_specs=pl.BlockSpec((None, SUBLANES, ATT_WIDTH), seq3),
            scratch_shapes=[pltpu.VMEM((ROWS_Q, LANES), F32), pltpu.VMEM((ROWS_Q, LANES), F32),
                            pltpu.VMEM((ROWS_Q, ATT_WIDTH), F32)]),
        out_shape=jax.ShapeDtypeStruct((nb, SUBLANES, ATT_WIDTH), F32),
        compiler_params=_cparams(("arbitrary", "arbitrary")),
    )(pt, rel_bias, qbd, madd, madd, k_new, v_new, *([ck] * PAGES_PER_STEP), *([cv] * PAGES_PER_STEP))


TM_PROJ = 256
TQ_PROMPT = 256


def _layer_weights(g_pre, w_in, conv_w, conv_b, dt_bias, a_log, d_skip, g_ssm, w_out, g_post):
    offs = np.cumsum([0, ATT_WIDTH, ATT_WIDTH, ATT_WIDTH, ATT_WIDTH, N_IDX_HEADS * D_IDX, D_IDX, N_IDX_HEADS,
                      SSM_WIDTH, CONV_CH, SSM_HEADS])
    q, k, v, ga, qi, ki, wi, z, xbc, dt = [w_in[:, offs[n]:offs[n + 1]] for n in range(10)]
    w_main = jnp.concatenate([q, k, v, ga, z, xbc], axis=1).astype(BF16)
    pad = jnp.zeros((D_MODEL, LANES - D_IDX - N_IDX_HEADS - SSM_HEADS), F32)
    w_idx = jnp.concatenate([qi, ki, wi, dt, pad], axis=1)
    return dict(g_pre=g_pre[None, :], w_main=w_main, w_idx=w_idx, conv_w=conv_w, conv_b=conv_b, dt_bias=dt_bias,
                a_log=a_log, d_skip=d_skip, g_ssm=g_ssm, w_top=w_out[:ATT_WIDTH].astype(BF16),
                w_bot=w_out[ATT_WIDTH:].astype(BF16), g_post=g_post[None, :])


def _mixer(x, lw, pos_off, t_valid, h0, c0, attn_fn):
    nb, s, _ = x.shape
    tm = min(TM_PROJ, s)
    x2d = x.reshape(nb * s, D_MODEL)
    cos_t, sin_t = _rope_tables(s, pos_off)
    q_e, q_o, k, k_b, v, v_b, ga, z, xbc, qi, small = _inproj(x2d, lw["g_pre"], lw["w_main"], lw["w_idx"], cos_t, sin_t, tm)
    att = attn_fn(qi, small, q_e, q_o, k, v, k_b, v_b)
    ssm, h_final = _ssd(xbc, z, small, lw["conv_w"], lw["conv_b"], lw["dt_bias"], lw["a_log"], lw["d_skip"],
                        lw["g_ssm"], h0, c0, nb, s, t_valid)
    y = _outproj(att, ga, ssm, x2d, lw["w_top"], lw["w_bot"], lw["g_post"], tm)
    r = lambda a: a.reshape(nb, s, -1)[:, :t_valid]
    heads = lambda a: r(a).reshape(nb, t_valid, N_HEADS_A, HEAD_DIM)
    conv_state = r(xbc)[:, t_valid - (CONV_W - 1):]
    return (r(y), heads(k), heads(v), r(small)[..., :D_IDX],
            h_final.reshape(nb, SSM_HEADS, SSM_HEAD_DIM, D_STATE), conv_state)


def kernel(x_prompt, x_sample, cache_k, cache_v, cache_kidx, state_ssm, state_conv, page_table, g_pre, w_in, conv_w, conv_b, dt_bias, a_log, d_skip, g_ssm, w_out, g_post, rel_bias):
    depth = w_in.shape[0]
    bp, sp, _ = x_prompt.shape
    bs, ts, _ = x_sample.shape
    past = page_table.shape[1] * PAGE_SIZE
    assert ts <= SUBLANES and ts >= CONV_W - 1 and sp % TQ_PROMPT == 0 and page_table.shape[1] % PAGES_PER_STEP == 0
    state_rows = SSM_HEADS * SSM_HEAD_DIM
    bias_t = _bias_tiles(rel_bias, TQ_PROMPT)

    yp = x_prompt
    ys = jnp.pad(x_sample, ((0, 0), (0, CHUNK - ts), (0, 0)))
    outs_p, outs_s = [], []
    for l in range(depth):
        lw = _layer_weights(g_pre[l], w_in[l], conv_w[l], conv_b[l], dt_bias[l], a_log[l], d_skip[l], g_ssm[l],
                            w_out[l], g_post[l])

        def prompt_attn(qi, small, q_e, q_o, k, v, k_b, v_b):
            return _prompt_attention(rel_bias, qi, small, q_e, q_o, k_b, v_b, bias_t, bp, sp, TQ_PROMPT)

        def sample_attn(qi, small, q_e, q_o, k, v, k_b, v_b, layer=l):
            att8 = _sample_attention(page_table, rel_bias, qi, small, q_e + q_o, k, v, cache_k[layer], cache_v[layer],
                                     cache_kidx[layer], bs, ts)
            return jnp.pad(att8, ((0, 0), (0, CHUNK - SUBLANES), (0, 0))).reshape(bs * CHUNK, ATT_WIDTH)

        op = _mixer(yp, lw, 0, sp, jnp.zeros((bp, state_rows, D_STATE), F32),
                    jnp.zeros((bp, SUBLANES, CONV_CH), F32), prompt_attn)
        c0 = jnp.pad(state_conv[l], ((0, 0), (SUBLANES - (CONV_W - 1), 0), (0, 0)))
        os_ = _mixer(ys, lw, past, ts, state_ssm[l].reshape(bs, state_rows, D_STATE), c0, sample_attn)
        yp = op[0]
        ys = jnp.pad(os_[0], ((0, 0), (0, CHUNK - ts), (0, 0)))
        outs_p.append(op[1:])
        outs_s.append(os_[1:])
    stack = lambda outs, n: jnp.stack([o[n] for o in outs])
    return (yp, ys[:, :ts], *[stack(outs_p, n) for n in range(5)], *[stack(outs_s, n) for n in range(5)])
```

```python
import functools
import math

import jax
import jax.numpy as jnp
import numpy as np
from jax import lax
from jax.experimental import pallas as pl
from jax.experimental.pallas import tpu as pltpu

F32 = jnp.float32
BF16 = jnp.bfloat16
I32 = jnp.int32
HIGHEST = lax.Precision.HIGHEST

D_MODEL = 1024
PAGE_SIZE = 128
HEAD_DIM = 64
ATT_WIDTH = 512
N_HEADS_A = 8
N_IDX_HEADS = 8
D_IDX = 64
IDX_ROPE = 32
ROPE_BASE = 10000.0
TOPK_MAX = 256
NUM_BUCKETS = 32
MAX_DISTANCE = 128
SSM_WIDTH = 512
SSM_HEAD_DIM = 64
SSM_HEADS = 8
SSM_GROUPS = 2
D_STATE = 128
CONV_W = 4
CONV_CH = 1024
CHUNK = 128
EPS = 1e-6

LANES = 128
SUBLANES = 8
VMEM_LIMIT = 56 * 1024 * 1024
NEG = -1e30
INT_MIN = -2 ** 31
LOG2E = 1.4426950408889634

SM_WI = D_IDX
SM_DT = D_IDX + N_IDX_HEADS
MAIN_COLS = 4 * ATT_WIDTH + SSM_WIDTH + CONV_CH
IDX_COLS = N_IDX_HEADS * D_IDX + LANES
VP_WIDTH = N_HEADS_A * LANES


def _nt(a, b, **kw):
    return lax.dot_general(a, b, (((1,), (1,)), ((), ())), preferred_element_type=F32, **kw)


def _sigmoid(x):
    return 1.0 / (1.0 + jnp.exp(-x))


def _cparams(sem):
    return pltpu.CompilerParams(dimension_semantics=sem, vmem_limit_bytes=VMEM_LIMIT)


def _rope_table_kernel(inv_ref, cos_ref, sin_ref, *, pos_off):
    rows = cos_ref.shape[0]
    pos = (lax.broadcasted_iota(I32, (rows, LANES), 0) + pos_off).astype(F32)
    ang = pos * inv_ref[...]
    cos_ref[...] = jnp.cos(ang)
    sin_ref[...] = jnp.sin(ang)


def _rope_tables(rows, pos_off):
    inv = ROPE_BASE ** (-jnp.arange(0, IDX_ROPE, 2, dtype=F32) / IDX_ROPE)
    l64 = np.arange(LANES) % D_IDX
    inv_row = jnp.where(l64 < IDX_ROPE, inv[l64 % (IDX_ROPE // 2)], 0.0).astype(F32)[None, :]
    return pl.pallas_call(
        functools.partial(_rope_table_kernel, pos_off=pos_off),
        out_shape=(jax.ShapeDtypeStruct((rows, LANES), F32),) * 2,
    )(inv_row)


def _rope_tile(x, c, s1, s2):
    return x * c + pltpu.roll(x, LANES - IDX_ROPE // 2, 1) * s1 + pltpu.roll(x, IDX_ROPE // 2, 1) * s2


def _inproj_kernel(x_ref, g_ref, wm_ref, wi_ref, cos_ref, sin_ref,
                   qe_ref, qo_ref, k_ref, kt_ref, v_ref, vp_ref, ga_ref, z_ref, xbc_ref, qi_ref, sm_ref, smt_ref):
    x = x_ref[...]
    hn = x * lax.rsqrt(jnp.mean(x * x, axis=-1, keepdims=True) + EPS) * g_ref[...]
    hb = hn.astype(BF16)
    tm = x.shape[0]
    lane = lax.broadcasted_iota(I32, (tm, LANES), 1)
    low = lane < HEAD_DIM

    def main(lo, width):
        return jnp.dot(hb, wm_ref[:, lo:lo + width], preferred_element_type=F32)

    q = main(0, ATT_WIDTH) * (HEAD_DIM ** -0.5 * LOG2E)
    even = (lax.broadcasted_iota(I32, q.shape, 1) & HEAD_DIM) == 0
    qe_ref[...] = jnp.where(even, q, 0.0).astype(BF16)
    qo_ref[...] = jnp.where(even, 0.0, q).astype(BF16)
    k = main(ATT_WIDTH, ATT_WIDTH)
    k_ref[...] = k
    kt_ref[...] = k.T.astype(BF16)
    v = main(2 * ATT_WIDTH, ATT_WIDTH)
    v_ref[...] = v
    for p2 in range(N_HEADS_A // 2):
        pair = v[:, p2 * LANES:(p2 + 1) * LANES]
        vp_ref[:, (2 * p2) * LANES:(2 * p2 + 1) * LANES] = jnp.where(low, pair, 1.0).astype(BF16)
        vp_ref[:, (2 * p2 + 1) * LANES:(2 * p2 + 2) * LANES] = jnp.where(
            low, pltpu.roll(pair, HEAD_DIM, 1), 1.0).astype(BF16)
    ga_ref[...] = main(3 * ATT_WIDTH, ATT_WIDTH)
    z_ref[...] = main(4 * ATT_WIDTH, SSM_WIDTH)
    xbc_ref[...] = main(4 * ATT_WIDTH + SSM_WIDTH, CONV_CH)

    idx = jnp.dot(hn, wi_ref[...], precision=HIGHEST, preferred_element_type=F32)
    l64 = lane & (D_IDX - 1)
    c = cos_ref[...]
    s = sin_ref[...]
    first = l64 < IDX_ROPE // 2
    s1 = jnp.where(first, -s, 0.0)
    s2 = jnp.where(first, 0.0, s)
    for j in range(N_IDX_HEADS * D_IDX // LANES):
        qi_ref[:, j * LANES:(j + 1) * LANES] = _rope_tile(idx[:, j * LANES:(j + 1) * LANES], c, s1, s2)
    sm = _rope_tile(idx[:, N_IDX_HEADS * D_IDX:], jnp.where(low, c, 1.0),
                    jnp.where(low, s1, 0.0), jnp.where(low, s2, 0.0))
    is_wi = (lane >= SM_WI) & (lane < SM_DT)
    sm = jnp.where(is_wi, sm * (N_IDX_HEADS ** -0.5), sm)
    sm_ref[...] = sm
    smt_ref[...] = sm.T


def _inproj(x2d, g_pre, w_main, w_idx, cos_t, sin_t, tm):
    n = x2d.shape[0]
    tab_blocks = cos_t.shape[0] // tm
    row = lambda i: (i, 0)
    col = lambda i: (0, i)
    const = lambda i: (0, 0)
    tab = lambda i: (i % tab_blocks, 0)
    rows = lambda w, dt: (jax.ShapeDtypeStruct((n, w), dt), pl.BlockSpec((tm, w), row))
    cols = lambda w, dt: (jax.ShapeDtypeStruct((w, n), dt), pl.BlockSpec((w, tm), col))
    outs = [rows(ATT_WIDTH, BF16), rows(ATT_WIDTH, BF16), rows(ATT_WIDTH, F32), cols(ATT_WIDTH, BF16),
            rows(ATT_WIDTH, F32), rows(VP_WIDTH, BF16), rows(ATT_WIDTH, F32), rows(SSM_WIDTH, F32),
            rows(CONV_CH, F32), rows(N_IDX_HEADS * D_IDX, F32), rows(LANES, F32), cols(LANES, F32)]
    return pl.pallas_call(
        _inproj_kernel,
        grid=(n // tm,),
        in_specs=[pl.BlockSpec((tm, D_MODEL), row), pl.BlockSpec((1, D_MODEL), const),
                  pl.BlockSpec((D_MODEL, MAIN_COLS), const), pl.BlockSpec((D_MODEL, IDX_COLS), const),
                  pl.BlockSpec((tm, LANES), tab), pl.BlockSpec((tm, LANES), tab)],
        out_specs=[o[1] for o in outs],
        out_shape=[o[0] for o in outs],
        compiler_params=_cparams(("arbitrary",)),
    )(x2d, g_pre, w_main, w_idx, cos_t, sin_t)


def _bucket(dist):
    max_exact = NUM_BUCKETS // 2
    n = jnp.maximum(dist, 0)
    nf = jnp.maximum(n, max_exact).astype(F32)
    large = max_exact + (jnp.log(nf / max_exact) / math.log(MAX_DISTANCE / max_exact)
                         * (NUM_BUCKETS - max_exact)).astype(I32)
    large = jnp.minimum(large, NUM_BUCKETS - 1)
    return jnp.where(n < max_exact, n, large)


def _bias_lookup(bucket, relb_ref, h):
    out = jnp.full(bucket.shape, relb_ref[0, h], F32)
    for b in range(1, NUM_BUCKETS):
        out = jnp.where(bucket == b, relb_ref[b, h], out)
    return out


def _ordered_key(score):
    bits = pltpu.bitcast(score + 0.0, I32)
    return jnp.where(bits < 0, bits ^ 0x7FFFFFFF, bits)


def _topk_threshold(count_fn, n_top, pos_bits, shape):
    k = float(n_top)

    def thr_body(it, thr):
        cand = thr ^ lax.shift_left(jnp.int32(1), 31 - it)
        cnt = count_fn(lambda key, pos: key >= cand)
        return jnp.where(cnt >= k, cand, thr)

    thr = lax.fori_loop(0, 32, thr_body, jnp.full(shape, INT_MIN, I32))
    need = k - count_fn(lambda key, pos: key > thr)

    def last_body(it, q):
        cand = q | lax.shift_left(jnp.int32(1), pos_bits - 1 - it)
        cnt = count_fn(lambda key, pos: (key == thr) & (pos < cand))
        return jnp.where(cnt < need, cand, q)

    last = lax.fori_loop(0, pos_bits, last_body, jnp.zeros(shape, I32))
    return thr, last


def _select_madd(key, pos, thr, last):
    return jnp.where(key > thr, 0.0, jnp.where(key == thr, jnp.where(pos <= last, 0.0, NEG), NEG))


def _bias_tiles_kernel(relb_ref, o_ref, *, tq):
    ti = lax.broadcasted_iota(I32, (tq, tq), 0)
    ki = lax.broadcasted_iota(I32, (tq, tq), 1)
    for kind in range(2):
        bucket = _bucket(ti - ki + kind * tq)
        for h in range(N_HEADS_A):
            o_ref[h, kind] = (_bias_lookup(bucket, relb_ref, h) - relb_ref[NUM_BUCKETS - 1, h]) * LOG2E


def _bias_tiles(rel_bias, tq):
    return pl.pallas_call(
        functools.partial(_bias_tiles_kernel, tq=tq),
        in_specs=[pl.BlockSpec(memory_space=pltpu.SMEM)],
        out_shape=jax.ShapeDtypeStruct((N_HEADS_A, 2, tq, tq), F32),
        compiler_params=pltpu.CompilerParams(vmem_limit_bytes=VMEM_LIMIT),
    )(rel_bias)


def _pattn_kernel(qi_ref, smq_ref, smt_ref, qe_ref, qo_ref, kt_ref, vp_ref, bt_ref, o_ref,
                  ki3_ref, qh3_ref, wb_ref, keys_ref, cand_ref, last_ref, m_ref, acc_ref, *, tq, n_top, pos_bits):
    i = pl.program_id(1)
    nch = i + 1
    k_top = float(n_top)
    lane = lax.broadcasted_iota(I32, (tq, LANES), 1)
    low = lane < HEAD_DIM
    row_pos = i * tq + lax.broadcasted_iota(I32, (tq, LANES), 0)
    tiles = tq // LANES

    def chunk(c, width=tq):
        return pl.ds(pl.multiple_of(c * tq, tq), width)

    def tile(c, t):
        return pl.ds(pl.multiple_of(c * tq + t * LANES, LANES), LANES)

    @pl.when(i == 0)
    def _():
        kt = smt_ref[0:D_IDX, :]
        hi = kt.astype(BF16)
        ki3_ref[0:D_IDX, :] = hi
        ki3_ref[D_IDX:2 * D_IDX, :] = hi
        ki3_ref[2 * D_IDX:3 * D_IDX, :] = (kt - hi.astype(F32)).astype(BF16)
        ki3_ref[3 * D_IDX:, :] = jnp.zeros((D_IDX, kt.shape[1]), BF16)

    sm = smq_ref[...]
    for h in range(N_IDX_HEADS):
        pair = qi_ref[:, (h // 2) * LANES:(h // 2 + 1) * LANES]
        hi = pair.astype(BF16).astype(F32)
        lo = pair - hi
        if h % 2 == 0:
            first = jnp.where(low, hi, pltpu.roll(lo, HEAD_DIM, 1))
        else:
            hi = pltpu.roll(hi, HEAD_DIM, 1)
            first = jnp.where(low, hi, lo)
        qh3_ref[h, :, 0:LANES] = first.astype(BF16)
        qh3_ref[h, :, LANES:2 * LANES] = jnp.where(low, hi, 0.0).astype(BF16)
        wb_ref[h] = jnp.broadcast_to(sm[:, SM_WI + h:SM_WI + h + 1] * (D_IDX ** -0.5), (tq, LANES))

    def score_body(c, carry):
        kc3 = ki3_ref[:, chunk(c)]
        sc = None
        for h in range(N_IDX_HEADS):
            d = jnp.dot(qh3_ref[h], kc3, preferred_element_type=F32)
            term = jnp.maximum(d, 0.0) * jnp.concatenate([wb_ref[h]] * tiles, axis=1)
            sc = term if sc is None else sc + term
        for t in range(tiles):
            part = jnp.where(c * tq + t * LANES + lane <= row_pos, sc[:, t * LANES:(t + 1) * LANES], -jnp.inf)
            keys_ref[:, tile(c, t)] = _ordered_key(part)
        return carry

    lax.fori_loop(0, nch, score_body, 0)

    def count(pred):
        parts = []
        lane_rows = lax.broadcasted_iota(I32, (COUNT_ROWS, LANES), 1)
        for r0 in range(0, tq, COUNT_ROWS):
            rows = slice(r0, r0 + COUNT_ROWS)

            def body(c, acc, rows=rows):
                for t in range(tiles):
                    hit = pred(keys_ref[rows, tile(c, t)], c * tq + t * LANES + lane_rows, rows)
                    acc = acc + jnp.where(hit, 1.0, 0.0)
                return acc
            parts.append(lax.fori_loop(0, nch, body, jnp.zeros((COUNT_ROWS, LANES), F32)))
        acc = jnp.concatenate(parts, axis=0)
        return jnp.broadcast_to(jnp.sum(acc, axis=1, keepdims=True), (tq, LANES))

    def thr_body(it, thr):
        cand = thr ^ lax.shift_left(jnp.int32(1), 31 - it)
        cand_ref[...] = cand
        cnt = count(lambda key, pos, rows: key >= cand_ref[rows, :])
        return jnp.where(cnt >= k_top, cand, thr)

    thr = lax.fori_loop(0, 32, thr_body, jnp.full((tq, LANES), INT_MIN, I32))
    cand_ref[...] = thr
    need = k_top - count(lambda key, pos, rows: key > cand_ref[rows, :])
    n_eq = count(lambda key, pos, rows: key == cand_ref[rows, :])
    last_ref[...] = jnp.full((tq, LANES), 2 ** pos_bits - 1, I32)

    @pl.when(jnp.max(n_eq - need) > 0.0)
    def _():
        def last_body(it, q):
            cand = q | lax.shift_left(jnp.int32(1), pos_bits - 1 - it)
            last_ref[...] = cand
            cnt = count(lambda key, pos, rows: (key == cand_ref[rows, :]) & (pos < last_ref[rows, :]))
            return jnp.where(cnt < need, cand, q)
        last_ref[...] = lax.fori_loop(0, pos_bits, last_body, jnp.zeros((tq, LANES), I32))

    def madd_body(c, carry):
        for t in range(tiles):
            pos = c * tq + t * LANES + lane
            madd = _select_madd(keys_ref[:, tile(c, t)], pos, cand_ref[...], last_ref[...])
            keys_ref[:, tile(c, t)] = pltpu.bitcast(jnp.where(pos <= row_pos, madd, NEG), I32)
        return carry

    lax.fori_loop(0, nch, madd_body, 0)

    m_ref[...] = jnp.full(m_ref.shape, NEG, F32)
    acc_ref[...] = jnp.zeros(acc_ref.shape, F32)

    def attend(c0, width, bias_of_head):
        sl = chunk(c0, width)
        madd = pltpu.bitcast(keys_ref[:, sl], F32)
        for h in range(N_HEADS_A):
            p2 = h // 2
            qh = (qe_ref if h % 2 == 0 else qo_ref)[:, p2 * LANES:(p2 + 1) * LANES]
            s = jnp.dot(qh, kt_ref[p2 * LANES:(p2 + 1) * LANES, sl], preferred_element_type=F32) + madd
            if bias_of_head is not None:
                s = s + bias_of_head(h)
            m_prev = m_ref[h]
            m_new = jnp.maximum(m_prev, jnp.broadcast_to(jnp.max(s, axis=1, keepdims=True), (tq, LANES)))
            p = jnp.exp2(s - jnp.concatenate([m_new] * (width // LANES), axis=1))
            pv = jnp.dot(p.astype(BF16), vp_ref[sl, h * LANES:(h + 1) * LANES], preferred_element_type=F32)
            acc_ref[h] = jnp.exp2(m_prev - m_new) * acc_ref[h] + pv
            m_ref[h] = m_new

    n_far = jnp.maximum(i - 1, 0)

    def far_body(c, carry):
        attend(2 * c, 2 * tq, None)
        return carry

    lax.fori_loop(0, n_far // 2, far_body, 0)

    @pl.when(n_far % 2 == 1)
    def _():
        attend(n_far - 1, tq, None)

    @pl.when(i >= 1)
    def _():
        attend(i - 1, tq, lambda h: bt_ref[h, 1])

    attend(i, tq, lambda h: bt_ref[h, 0])

    outs = []
    for h in range(N_HEADS_A):
        acc = acc_ref[h]
        outs.append(acc / pltpu.roll(acc, HEAD_DIM, 1))
    for p2 in range(N_HEADS_A // 2):
        o_ref[:, p2 * LANES:(p2 + 1) * LANES] = jnp.where(low, outs[2 * p2], pltpu.roll(outs[2 * p2 + 1], HEAD_DIM, 1))


def _prompt_attention(qi, small, small_t, q_e, q_o, k_t, v_p, bias_t, nb, s, tq):
    nq = s // tq
    n_top = min(TOPK_MAX, s // 4)
    pos_bits = max(1, (s - 1).bit_length())
    qblk = lambda b, i: (b * nq + i, 0)
    seq_rows = lambda b, i: (b, 0)
    seq_cols = lambda b, i: (0, b)
    return pl.pallas_call(
        functools.partial(_pattn_kernel, tq=tq, n_top=n_top, pos_bits=pos_bits),
        grid=(nb, nq),
        in_specs=[pl.BlockSpec((tq, N_IDX_HEADS * D_IDX), qblk), pl.BlockSpec((tq, LANES), qblk),
                  pl.BlockSpec((LANES, s), seq_cols),
                  pl.BlockSpec((tq, ATT_WIDTH), qblk), pl.BlockSpec((tq, ATT_WIDTH), qblk),
                  pl.BlockSpec((ATT_WIDTH, s), seq_cols), pl.BlockSpec((s, VP_WIDTH), seq_rows),
                  pl.BlockSpec((N_HEADS_A, 2, tq, tq), lambda b, i: (0, 0, 0, 0),
                               pipeline_mode=pl.Buffered(1))],
        out_specs=pl.BlockSpec((tq, ATT_WIDTH), qblk),
        out_shape=jax.ShapeDtypeStruct((nb * s, ATT_WIDTH), F32),
        scratch_shapes=[pltpu.VMEM((4 * D_IDX, s), BF16), pltpu.VMEM((N_IDX_HEADS, tq, 2 * LANES), BF16),
                        pltpu.VMEM((N_IDX_HEADS, tq, LANES), F32), pltpu.VMEM((tq, s), I32),
                        pltpu.VMEM((tq, LANES), I32), pltpu.VMEM((tq, LANES), I32),
                        pltpu.VMEM((N_HEADS_A, tq, LANES), F32), pltpu.VMEM((N_HEADS_A, tq, LANES), F32)],
        compiler_params=_cparams(("arbitrary", "arbitrary")),
    )(qi, small, small_t, q_e, q_o, k_t, v_p, bias_t)


def _ssd_kernel(xbc_ref, z_ref, sm_ref, cw_ref, cb_ref, dtb_ref, alog_ref, dsk_ref, gs_ref, ex_ref, ext_ref,
                h0_ref, c0_ref, y_ref, hf_ref, xp_ref, st_ref, *, t_valid):
    c = pl.program_id(1)
    L = CHUNK

    @pl.when(c == 0)
    def _():
        st_ref[...] = h0_ref[...]
        xp_ref[0:SUBLANES, :] = c0_ref[...]

    xp_ref[SUBLANES:SUBLANES + L, :] = xbc_ref[...]
    conv = cb_ref[...]
    for j in range(CONV_W):
        lo = SUBLANES - (CONV_W - 1) + j
        conv = conv + xp_ref[lo:lo + L, :] * cw_ref[j:j + 1, :]
    xp_ref[0:SUBLANES, :] = xp_ref[L:L + SUBLANES, :]
    act = conv * _sigmoid(conv)
    xs = act[:, :SSM_WIDTH]
    bm = act[:, SSM_WIDTH:SSM_WIDTH + SSM_GROUPS * D_STATE]
    cm = act[:, SSM_WIDTH + SSM_GROUPS * D_STATE:]

    raw = sm_ref[...] + dtb_ref[...]
    dtf = jnp.maximum(raw, 0.0) + jnp.log1p(jnp.exp(-jnp.abs(raw)))
    row = lax.broadcasted_iota(I32, (L, LANES), 0)
    if t_valid < L:
        dtf = jnp.where(row < t_valid, dtf, 0.0)
    adt = dtf * (-jnp.exp(alog_ref[...]))
    tril = row >= lax.broadcasted_iota(I32, (L, LANES), 1)
    cs = jnp.dot(jnp.where(tril, 1.0, 0.0), adt, precision=HIGHEST, preferred_element_type=F32)
    ex = ex_ref[...]
    dtx = jnp.dot(dtf, ex, precision=HIGHEST, preferred_element_type=F32)
    csx = jnp.dot(cs, ex, precision=HIGHEST, preferred_element_type=F32)
    cst = cs.T
    x = xs * dtx
    w = x * jnp.exp(csx[L - 1:L, :] - csx)
    ecsx = jnp.exp(csx)
    dec = jnp.exp(jnp.dot(ext_ref[...], cst, precision=HIGHEST, preferred_element_type=F32)[:, L - 1:L])
    low = lax.broadcasted_iota(I32, (L, LANES), 1) < SSM_HEAD_DIM

    ys = []
    for p2 in range(SSM_HEADS // 2):
        g = (2 * p2) // (SSM_HEADS // SSM_GROUPS)
        cg = cm[:, g * D_STATE:(g + 1) * D_STATE].astype(BF16)
        bg = bm[:, g * D_STATE:(g + 1) * D_STATE].astype(BF16)
        cb_mat = _nt(cg, bg)
        lanes = slice(p2 * LANES, (p2 + 1) * LANES)
        xp = x[:, lanes].astype(BF16)
        yd = []
        for h in (2 * p2, 2 * p2 + 1):
            diff = cs[:, SM_DT + h:SM_DT + h + 1] - cst[SM_DT + h:SM_DT + h + 1, :]
            lm = jnp.exp(jnp.where(tril, diff, NEG))
            yd.append(jnp.dot((cb_mat * lm).astype(BF16), xp, preferred_element_type=F32))
        rows = slice(p2 * LANES, (p2 + 1) * LANES)
        st = st_ref[rows, :]
        y_off = _nt(cg, st.astype(BF16)) * ecsx[:, lanes]
        ys.append(jnp.where(low, yd[0], yd[1]) + y_off)
        upd = jnp.dot(w[:, lanes].T.astype(BF16), bg, preferred_element_type=F32)
        st_ref[rows, :] = st * dec[rows, :] + upd

    y = jnp.concatenate(ys, axis=1) + dsk_ref[...] * xs
    zz = z_ref[...]
    gated = y * (zz * _sigmoid(zz))
    y_ref[...] = gated * lax.rsqrt(jnp.mean(gated * gated, axis=-1, keepdims=True) + EPS) * gs_ref[...]

    @pl.when(c == pl.num_programs(1) - 1)
    def _():
        hf_ref[...] = st_ref[...]


def _ssd(xbc, z, small, conv_w, conv_b, dt_bias, a_log, d_skip, g_ssm, h0, c0, nb, s, t_valid):
    nc = s // CHUNK
    blk = lambda b, c: (b * nc + c, 0)
    const = lambda b, c: (0, 0)
    per_b = lambda b, c: (b, 0, 0)
    lanes = np.arange(LANES)
    dt_row = lambda v: jnp.zeros((1, LANES), F32).at[0, SM_DT:SM_DT + SSM_HEADS].set(v)
    expand = (lanes[:, None] == SM_DT + np.arange(SSM_WIDTH)[None, :] // SSM_HEAD_DIM).astype(np.float32)
    state_rows = SSM_HEADS * SSM_HEAD_DIM
    return pl.pallas_call(
        functools.partial(_ssd_kernel, t_valid=t_valid),
        grid=(nb, nc),
        in_specs=[pl.BlockSpec((CHUNK, CONV_CH), blk), pl.BlockSpec((CHUNK, SSM_WIDTH), blk),
                  pl.BlockSpec((CHUNK, LANES), blk),
                  pl.BlockSpec((CONV_W, CONV_CH), const), pl.BlockSpec((1, CONV_CH), const),
                  pl.BlockSpec((1, LANES), const), pl.BlockSpec((1, LANES), const),
                  pl.BlockSpec((1, SSM_WIDTH), const), pl.BlockSpec((1, SSM_WIDTH), const),
                  pl.BlockSpec((LANES, SSM_WIDTH), const), pl.BlockSpec((SSM_WIDTH, LANES), const),
                  pl.BlockSpec((None, state_rows, D_STATE), per_b),
                  pl.BlockSpec((None, SUBLANES, CONV_CH), per_b)],
        out_specs=[pl.BlockSpec((CHUNK, SSM_WIDTH), blk), pl.BlockSpec((None, state_rows, D_STATE), per_b)],
        out_shape=[jax.ShapeDtypeStruct((nb * s, SSM_WIDTH), F32),
                   jax.ShapeDtypeStruct((nb, state_rows, D_STATE), F32)],
        scratch_shapes=[pltpu.VMEM((CHUNK + SUBLANES, CONV_CH), F32), pltpu.VMEM((state_rows, D_STATE), F32)],
        compiler_params=_cparams(("arbitrary", "arbitrary")),
    )(xbc, z, small, conv_w, conv_b[None, :], dt_row(dt_bias), dt_row(a_log),
      jnp.repeat(d_skip, SSM_HEAD_DIM)[None, :], g_ssm[None, :], jnp.asarray(expand), jnp.asarray(expand.T),
      h0, c0)


def _outproj_kernel(att_ref, ga_ref, ssm_ref, x_ref, wt_ref, wb_ref, gp_ref, o_ref):
    ga = ga_ref[...]
    att = att_ref[...] * (ga * _sigmoid(ga))
    out = (jnp.dot(att.astype(BF16), wt_ref[...], preferred_element_type=F32)
           + jnp.dot(ssm_ref[...].astype(BF16), wb_ref[...], preferred_element_type=F32))
    o_ref[...] = x_ref[...] + out * lax.rsqrt(jnp.mean(out * out, axis=-1, keepdims=True) + EPS) * gp_ref[...]


def _outproj(att, ga, ssm, x2d, w_top, w_bot, g_post, tm):
    n = x2d.shape[0]
    row = lambda i: (i, 0)
    const = lambda i: (0, 0)
    return pl.pallas_call(
        _outproj_kernel,
        grid=(n // tm,),
        in_specs=[pl.BlockSpec((tm, ATT_WIDTH), row), pl.BlockSpec((tm, ATT_WIDTH), row),
                  pl.BlockSpec((tm, SSM_WIDTH), row), pl.BlockSpec((tm, D_MODEL), row),
                  pl.BlockSpec((ATT_WIDTH, D_MODEL), const), pl.BlockSpec((SSM_WIDTH, D_MODEL), const),
                  pl.BlockSpec((1, D_MODEL), const)],
        out_specs=pl.BlockSpec((tm, D_MODEL), row),
        out_shape=jax.ShapeDtypeStruct((n, D_MODEL), F32),
        compiler_params=_cparams(("arbitrary",)),
    )(att, ga, ssm, x2d, w_top, w_bot, g_post)


PAGES_PER_STEP = 8
ROWS_Q = N_HEADS_A * SUBLANES


def _sscore_kernel(pt_ref, qall_ref, wcol_ref, smt_ref, *rest, past, t_new, n_top, pos_bits):
    pages = rest[:PAGES_PER_STEP]
    madd_ref = rest[PAGES_PER_STEP]
    keys_ref = rest[PAGES_PER_STEP + 1]
    j = pl.program_id(1)
    kw = PAGES_PER_STEP * PAGE_SIZE
    total = past + LANES
    qall = qall_ref[...]
    wcol = wcol_ref[...]

    def scores(kt):
        d = jnp.dot(qall, kt, precision=HIGHEST, preferred_element_type=F32)
        r = jnp.maximum(d * (D_IDX ** -0.5), 0.0) * wcol
        sc = r[0:SUBLANES]
        for h in range(1, N_IDX_HEADS):
            sc = sc + r[h * SUBLANES:(h + 1) * SUBLANES]
        return sc

    for r, page in enumerate(pages):
        sl = pl.ds(pl.multiple_of(j * kw + r * PAGE_SIZE, PAGE_SIZE), PAGE_SIZE)
        keys_ref[:, sl] = _ordered_key(scores(page[...]))

    @pl.when(j == pl.num_programs(1) - 1)
    def _():
        lane = lax.broadcasted_iota(I32, (SUBLANES, LANES), 1)
        row = lax.broadcasted_iota(I32, (SUBLANES, LANES), 0)
        sc = scores(smt_ref[0:D_IDX, :])
        vis = (lane <= row) & (lane < t_new)
        keys_ref[:, past:total] = _ordered_key(jnp.where(vis, sc, -jnp.inf))

        def count_fn(pred):
            acc = jnp.zeros((SUBLANES, LANES), F32)
            for t in range(total // LANES):
                acc = acc + jnp.where(pred(keys_ref[:, t * LANES:(t + 1) * LANES], t * LANES + lane), 1.0, 0.0)
            return jnp.broadcast_to(jnp.sum(acc, axis=1, keepdims=True), (SUBLANES, LANES))

        thr, last = _topk_threshold(count_fn, n_top, pos_bits, (SUBLANES, LANES))
        for t in range(total // LANES):
            sl = slice(t * LANES, (t + 1) * LANES)
            pos = t * LANES + lane
            madd = _select_madd(keys_ref[:, sl], pos, thr, last)
            if t * LANES >= past:
                madd = jnp.where(vis, madd, NEG)
            madd_ref[:, sl] = madd


def _sattn_kernel(pt_ref, relb_ref, qbd_ref, madd_ref, maddn_ref, ktnew_ref, vnew_ref, *rest, past):
    kpages = rest[:PAGES_PER_STEP]
    vpages = rest[PAGES_PER_STEP:2 * PAGES_PER_STEP]
    o_ref, m_ref, l_ref, acc_ref = rest[2 * PAGES_PER_STEP:]
    j = pl.program_id(1)
    kw = PAGES_PER_STEP * PAGE_SIZE

    @pl.when(j == 0)
    def _():
        m_ref[...] = jnp.full(m_ref.shape, NEG, F32)
        l_ref[...] = jnp.zeros(l_ref.shape, F32)
        acc_ref[...] = jnp.zeros(acc_ref.shape, F32)

    qbd = qbd_ref[...]

    def far_bias(width):
        row_head = lax.broadcasted_iota(I32, (ROWS_Q, width), 0) // SUBLANES
        out = jnp.full((ROWS_Q, width), relb_ref[NUM_BUCKETS - 1, 0], F32)
        for h in range(1, N_HEADS_A):
            out = jnp.where(row_head == h, relb_ref[NUM_BUCKETS - 1, h], out)
        return out

    def near_bias(width, pos0):
        tok = lax.broadcasted_iota(I32, (SUBLANES, width), 0)
        pos = pos0 + lax.broadcasted_iota(I32, (SUBLANES, width), 1)
        bucket = _bucket(past + tok - pos)
        return jnp.concatenate([_bias_lookup(bucket, relb_ref, h) for h in range(N_HEADS_A)], axis=0)

    def update(logits, bias, madd8, pv_fn):
        s = logits + (bias * LOG2E + jnp.concatenate([madd8] * N_HEADS_A, axis=0))
        m_prev = m_ref[...]
        m_new = jnp.maximum(m_prev, jnp.broadcast_to(jnp.max(s, axis=1, keepdims=True), (ROWS_Q, LANES)))
        alpha = jnp.exp2(m_prev - m_new)
        p = jnp.exp2(s - m_new[:, :1])
        l_ref[...] = alpha * l_ref[...] + jnp.broadcast_to(jnp.sum(p, axis=1, keepdims=True), (ROWS_Q, LANES))
        acc_ref[...] = alpha[:, :1] * acc_ref[...] + pv_fn(p.astype(BF16))
        m_ref[...] = m_new

    def paged(bias):
        logits = jnp.concatenate([jnp.dot(qbd, kp[...].astype(BF16), preferred_element_type=F32) for kp in kpages],
                                 axis=1)

        def pv_fn(p):
            pv = _nt(p[:, 0:PAGE_SIZE], vpages[0][...].astype(BF16))
            for r in range(1, PAGES_PER_STEP):
                pv = pv + _nt(p[:, r * PAGE_SIZE:(r + 1) * PAGE_SIZE], vpages[r][...].astype(BF16))
            return pv

        update(logits, bias, madd_ref[...], pv_fn)

    far = (j + 1) * kw + MAX_DISTANCE <= past + 1

    @pl.when(far)
    def _():
        paged(far_bias(kw))

    @pl.when(jnp.logical_not(far))
    def _():
        paged(near_bias(kw, j * kw))

    @pl.when(j == pl.num_programs(1) - 1)
    def _():
        logits = jnp.dot(qbd, ktnew_ref[...], preferred_element_type=F32)
        update(logits, near_bias(LANES, past), maddn_ref[...],
               lambda p: jnp.dot(p, vnew_ref[...].astype(BF16), preferred_element_type=F32))
        o = acc_ref[...] / l_ref[:, :1]
        own = (lax.broadcasted_iota(I32, (ROWS_Q, ATT_WIDTH), 0) // SUBLANES
               == lax.broadcasted_iota(I32, (ROWS_Q, ATT_WIDTH), 1) // HEAD_DIM)
        o = jnp.where(own, o, 0.0)
        out = o[0:SUBLANES]
        for h in range(1, N_HEADS_A):
            out = out + o[h * SUBLANES:(h + 1) * SUBLANES]
        o_ref[...] = out


def _sample_attention(page_table, rel_bias, qi, small, small_t, q_s, k_t, v_new, cache_k, cache_v, cache_kidx,
                      nb, t_new):
    n_pages = page_table.shape[1]
    past = n_pages * PAGE_SIZE
    nj = n_pages // PAGES_PER_STEP
    total = past + LANES
    n_top = min(TOPK_MAX, (past + t_new) // 4)
    pos_bits = max(1, (total - 1).bit_length())
    pt = page_table.reshape(-1)
    pool = cache_kidx.shape[0]

    tok = lambda a: a.reshape(nb, CHUNK, -1)[:, :SUBLANES]
    qall = tok(qi).reshape(nb, SUBLANES, N_IDX_HEADS, D_IDX).transpose(0, 2, 1, 3).reshape(nb, ROWS_Q, D_IDX)
    wi_t = tok(small)[:, :, SM_WI:SM_WI + N_IDX_HEADS].transpose(0, 2, 1).reshape(nb, ROWS_Q, 1)
    wcol = jnp.broadcast_to(wi_t, (nb, ROWS_Q, LANES))
    q_t = tok(q_s).reshape(nb, SUBLANES, N_HEADS_A, HEAD_DIM).transpose(0, 2, 1, 3)
    eye = jnp.eye(N_HEADS_A, dtype=q_t.dtype)
    qbd = (q_t[:, :, :, None, :] * eye[None, :, None, :, None]).reshape(nb, ROWS_Q, ATT_WIDTH)
    ckt = cache_k.transpose(0, 2, 3, 1).reshape(pool, ATT_WIDTH, PAGE_SIZE)
    cvt = cache_v.transpose(0, 2, 3, 1).reshape(pool, ATT_WIDTH, PAGE_SIZE)
    cit = cache_kidx.transpose(0, 2, 1)

    def page(r):
        return lambda b, j, pt_ref: (pt_ref[b * n_pages + j * PAGES_PER_STEP + r], 0, 0)

    seq3 = lambda b, j, pt_ref: (b, 0, 0)
    tcol = lambda b, j, pt_ref: (0, b)

    madd = pl.pallas_call(
        functools.partial(_sscore_kernel, past=past, t_new=t_new, n_top=n_top, pos_bits=pos_bits),
        grid_spec=pltpu.PrefetchScalarGridSpec(
            num_scalar_prefetch=1, grid=(nb, nj),
            in_specs=[pl.BlockSpec((None, ROWS_Q, D_IDX), seq3), pl.BlockSpec((None, ROWS_Q, LANES), seq3),
                      pl.BlockSpec((LANES, CHUNK), tcol)]
                     + [pl.BlockSpec((None, D_IDX, PAGE_SIZE), page(r)) for r in range(PAGES_PER_STEP)],
            out_specs=pl.BlockSpec((None, SUBLANES, total), seq3),
            scratch_shapes=[pltpu.VMEM((SUBLANES, total), I32)]),
        out_shape=jax.ShapeDtypeStruct((nb, SUBLANES, total), F32),
        compiler_params=_cparams(("arbitrary", "arbitrary")),
    )(pt, qall, wcol, small_t, *([cit] * PAGES_PER_STEP))

    kw = PAGES_PER_STEP * PAGE_SIZE
    return pl.pallas_call(
        functools.partial(_sattn_kernel, past=past),
        grid_spec=pltpu.PrefetchScalarGridSpec(
            num_scalar_prefetch=1, grid=(nb, nj),
            in_specs=[pl.BlockSpec(memory_space=pltpu.SMEM),
                      pl.BlockSpec((None, ROWS_Q, ATT_WIDTH), seq3),
                      pl.BlockSpec((None, SUBLANES, kw), lambda b, j, pt_ref: (b, 0, j)),
                      pl.BlockSpec((None, SUBLANES, LANES), lambda b, j, pt_ref: (b, 0, past // LANES)),
                      pl.BlockSpec((ATT_WIDTH, CHUNK), tcol),
                      pl.BlockSpec((CHUNK, ATT_WIDTH), lambda b, j, pt_ref: (b, 0))]
                     + [pl.BlockSpec((None, ATT_WIDTH, PAGE_SIZE), page(r)) for r in range(PAGES_PER_STEP)] * 2,
            out_specs=pl.BlockSpec((None, SUBLANES, ATT_WIDTH), seq3),
            scratch_shapes=[pltpu.VMEM((ROWS_Q, LANES), F32), pltpu.VMEM((ROWS_Q, LANES), F32),
                            pltpu.VMEM((ROWS_Q, ATT_WIDTH), F32)]),
        out_shape=jax.ShapeDtypeStruct((nb, SUBLANES, ATT_WIDTH), F32),
        compiler_params=_cparams(("arbitrary", "arbitrary")),
    )(pt, rel_bias, qbd, madd, madd, k_t, v_new, *([ckt] * PAGES_PER_STEP), *([cvt] * PAGES_PER_STEP))


TM_PROJ = 256
TQ_PROMPT = 256
COUNT_ROWS = 128


def _layer_weights(g_pre, w_in, conv_w, conv_b, dt_bias, a_log, d_skip, g_ssm, w_out, g_post):
    offs = np.cumsum([0, ATT_WIDTH, ATT_WIDTH, ATT_WIDTH, ATT_WIDTH, N_IDX_HEADS * D_IDX, D_IDX, N_IDX_HEADS,
                      SSM_WIDTH, CONV_CH, SSM_HEADS])
    q, k, v, ga, qi, ki, wi, z, xbc, dt = [w_in[:, offs[n]:offs[n + 1]] for n in range(10)]
    w_main = jnp.concatenate([q, k, v, ga, z, xbc], axis=1).astype(BF16)
    pad = jnp.zeros((D_MODEL, LANES - D_IDX - N_IDX_HEADS - SSM_HEADS), F32)
    w_idx = jnp.concatenate([qi, ki, wi, dt, pad], axis=1)
    return dict(g_pre=g_pre[None, :], w_main=w_main, w_idx=w_idx, conv_w=conv_w, conv_b=conv_b, dt_bias=dt_bias,
                a_log=a_log, d_skip=d_skip, g_ssm=g_ssm, w_top=w_out[:ATT_WIDTH].astype(BF16),
                w_bot=w_out[ATT_WIDTH:].astype(BF16), g_post=g_post[None, :])


def _mixer(x, lw, pos_off, t_valid, h0, c0, attn_fn):
    nb, s, _ = x.shape
    tm = min(TM_PROJ, s)
    x2d = x.reshape(nb * s, D_MODEL)
    cos_t, sin_t = _rope_tables(s, pos_off)
    proj = _inproj(x2d, lw["g_pre"], lw["w_main"], lw["w_idx"], cos_t, sin_t, tm)
    q_e, q_o, k, k_t, v, v_p, ga, z, xbc, qi, small, small_t = proj
    att = attn_fn(proj)
    ssm, h_final = _ssd(xbc, z, small, lw["conv_w"], lw["conv_b"], lw["dt_bias"], lw["a_log"], lw["d_skip"],
                        lw["g_ssm"], h0, c0, nb, s, t_valid)
    y = _outproj(att, ga, ssm, x2d, lw["w_top"], lw["w_bot"], lw["g_post"], tm)
    r = lambda a: a.reshape(nb, s, -1)[:, :t_valid]
    heads = lambda a: r(a).reshape(nb, t_valid, N_HEADS_A, HEAD_DIM)
    conv_state = r(xbc)[:, t_valid - (CONV_W - 1):]
    return (r(y), heads(k), heads(v), r(small)[..., :D_IDX],
            h_final.reshape(nb, SSM_HEADS, SSM_HEAD_DIM, D_STATE), conv_state)


def kernel(x_prompt, x_sample, cache_k, cache_v, cache_kidx, state_ssm, state_conv, page_table, g_pre, w_in, conv_w, conv_b, dt_bias, a_log, d_skip, g_ssm, w_out, g_post, rel_bias):
    depth = w_in.shape[0]
    bp, sp, _ = x_prompt.shape
    bs, ts, _ = x_sample.shape
    past = page_table.shape[1] * PAGE_SIZE
    assert ts <= SUBLANES and ts >= CONV_W - 1 and sp % (2 * TQ_PROMPT) == 0
    assert page_table.shape[1] % PAGES_PER_STEP == 0
    state_rows = SSM_HEADS * SSM_HEAD_DIM
    bias_t = _bias_tiles(rel_bias, TQ_PROMPT)

    yp = x_prompt
    ys = jnp.pad(x_sample, ((0, 0), (0, CHUNK - ts), (0, 0)))
    outs_p, outs_s = [], []
    for l in range(depth):
        lw = _layer_weights(g_pre[l], w_in[l], conv_w[l], conv_b[l], dt_bias[l], a_log[l], d_skip[l], g_ssm[l],
                            w_out[l], g_post[l])

        def prompt_attn(proj):
            q_e, q_o, k, k_t, v, v_p, ga, z, xbc, qi, small, small_t = proj
            return _prompt_attention(qi, small, small_t, q_e, q_o, k_t, v_p, bias_t, bp, sp, TQ_PROMPT)

        def sample_attn(proj, layer=l):
            q_e, q_o, k, k_t, v, v_p, ga, z, xbc, qi, small, small_t = proj
            att8 = _sample_attention(page_table, rel_bias, qi, small, small_t, q_e + q_o, k_t, v, cache_k[layer],
                                     cache_v[layer], cache_kidx[layer], bs, ts)
            return jnp.pad(att8, ((0, 0), (0, CHUNK - SUBLANES), (0, 0))).reshape(bs * CHUNK, ATT_WIDTH)

        op = _mixer(yp, lw, 0, sp, jnp.zeros((bp, state_rows, D_STATE), F32),
                    jnp.zeros((bp, SUBLANES, CONV_CH), F32), prompt_attn)
        c0 = jnp.pad(state_conv[l], ((0, 0), (SUBLANES - (CONV_W - 1), 0), (0, 0)))
        os_ = _mixer(ys, lw, past, ts, state_ssm[l].reshape(bs, state_rows, D_STATE), c0, sample_attn)
        yp = op[0]
        ys = jnp.pad(os_[0], ((0, 0), (0, CHUNK - ts), (0, 0)))
        outs_p.append(op[1:])
        outs_s.append(os_[1:])
    stack = lambda outs, n: jnp.stack([o[n] for o in outs])
    return (yp, ys[:, :ts], *[stack(outs_p, n) for n in range(5)], *[stack(outs_s, n) for n in range(5)])
```

```python
import functools
import math

import jax
import jax.numpy as jnp
import numpy as np
from jax import lax
from jax.experimental import pallas as pl
from jax.experimental.pallas import tpu as pltpu

F32 = jnp.float32
BF16 = jnp.bfloat16
I32 = jnp.int32
HIGHEST = lax.Precision.HIGHEST

D_MODEL = 1024
PAGE_SIZE = 128
HEAD_DIM = 64
ATT_WIDTH = 512
N_HEADS_A = 8
N_IDX_HEADS = 8
D_IDX = 64
IDX_ROPE = 32
ROPE_BASE = 10000.0
TOPK_MAX = 256
NUM_BUCKETS = 32
MAX_DISTANCE = 128
SSM_WIDTH = 512
SSM_HEAD_DIM = 64
SSM_HEADS = 8
SSM_GROUPS = 2
D_STATE = 128
CONV_W = 4
CONV_CH = 1024
CHUNK = 128
EPS = 1e-6

LANES = 128
SUBLANES = 8
VMEM_LIMIT = 56 * 1024 * 1024
NEG = -1e30
INT_MIN = -2 ** 31
LOG2E = 1.4426950408889634

SM_WI = D_IDX
SM_DT = D_IDX + N_IDX_HEADS
ROW_COLS = 3 * ATT_WIDTH + SSM_WIDTH + CONV_CH
QI_WIDTH = N_IDX_HEADS * D_IDX
VP_ROWS = N_HEADS_A * LANES
SPLIT3 = 4 * D_IDX


def _nt(a, b, **kw):
    return lax.dot_general(a, b, (((1,), (1,)), ((), ())), preferred_element_type=F32, **kw)


def _dot(a, b, **kw):
    return jnp.dot(a, b, preferred_element_type=F32, **kw)


def _split(x):
    hi = x.astype(BF16)
    return hi, (x - hi.astype(F32)).astype(BF16)


def _sigmoid(x):
    return 1.0 / (1.0 + jnp.exp(-x))


def _cparams(sem):
    return pltpu.CompilerParams(dimension_semantics=sem, vmem_limit_bytes=VMEM_LIMIT)


def _rope_table_kernel(inv_ref, cos_ref, sin_ref, cost_ref, sint_ref, *, pos_off):
    rows = cos_ref.shape[0]
    pos = (lax.broadcasted_iota(I32, (rows, LANES), 0) + pos_off).astype(F32)
    ang = pos * inv_ref[...]
    c = jnp.cos(ang)
    s = jnp.sin(ang)
    cos_ref[...] = c
    sin_ref[...] = s
    cost_ref[...] = c.T
    sint_ref[...] = s.T


def _rope_tables(rows, pos_off):
    inv = ROPE_BASE ** (-jnp.arange(0, IDX_ROPE, 2, dtype=F32) / IDX_ROPE)
    l64 = np.arange(LANES) % D_IDX
    inv_row = jnp.where(l64 < IDX_ROPE, inv[l64 % (IDX_ROPE // 2)], 0.0).astype(F32)[None, :]
    return pl.pallas_call(
        functools.partial(_rope_table_kernel, pos_off=pos_off),
        out_shape=(jax.ShapeDtypeStruct((rows, LANES), F32),) * 2 + (jax.ShapeDtypeStruct((LANES, rows), F32),) * 2,
    )(inv_row)


def _inproj_kernel(x_ref, g_ref, wr_ref, wqt_ref, wqih_ref, wqil_ref, wsh_ref, wsl_ref,
                   cos_ref, sin_ref, cost_ref, sint_ref,
                   qet_ref, qot_ref, k3_ref, kb_ref, v_ref, v3_ref, vpt_ref, ga_ref, z_ref, xbc_ref, qit_ref, sm_ref,
                   smt_ref):
    x = x_ref[...]
    hn = x * lax.rsqrt(jnp.mean(x * x, axis=-1, keepdims=True) + EPS) * g_ref[...]
    hb, hlo = _split(hn)
    tm = x.shape[0]

    def rows(lo, width):
        return _dot(hb, wr_ref[:, lo:lo + width])

    k = rows(0, ATT_WIDTH)
    kb_ref[...] = k.astype(BF16)
    v = rows(ATT_WIDTH, ATT_WIDTH)
    v_ref[...] = v
    for h in range(N_HEADS_A):
        k3_ref[:, h, :] = k[:, h * HEAD_DIM:(h + 1) * HEAD_DIM]
        v3_ref[:, h, :] = v[:, h * HEAD_DIM:(h + 1) * HEAD_DIM]
    ga_ref[...] = rows(2 * ATT_WIDTH, ATT_WIDTH)
    z_ref[...] = rows(3 * ATT_WIDTH, SSM_WIDTH)
    xbc_ref[...] = rows(3 * ATT_WIDTH + SSM_WIDTH, CONV_CH)

    vt = v.T
    ones = jnp.ones((HEAD_DIM, tm), BF16)
    for h in range(N_HEADS_A):
        vpt_ref[h * LANES:h * LANES + HEAD_DIM, :] = vt[h * HEAD_DIM:(h + 1) * HEAD_DIM, :].astype(BF16)
        vpt_ref[h * LANES + HEAD_DIM:(h + 1) * LANES, :] = ones

    qt = _nt(wqt_ref[...], hb) * (HEAD_DIM ** -0.5 * LOG2E)
    even = (lax.broadcasted_iota(I32, qt.shape, 0) & HEAD_DIM) == 0
    qet_ref[...] = jnp.where(even, qt, 0.0).astype(BF16)
    qot_ref[...] = jnp.where(even, 0.0, qt).astype(BF16)

    qit = _nt(wqih_ref[...], hb) + (_nt(wqih_ref[...], hlo) + _nt(wqil_ref[...], hb))
    ct = cost_ref[0:D_IDX, :]
    st = sint_ref[0:D_IDX, :]
    first_t = lax.broadcasted_iota(I32, (D_IDX, tm), 0) < IDX_ROPE // 2
    s1t = jnp.where(first_t, -st, 0.0)
    s2t = jnp.where(first_t, 0.0, st)
    for h in range(N_IDX_HEADS):
        xh = qit[h * D_IDX:(h + 1) * D_IDX, :]
        qit_ref[h * D_IDX:(h + 1) * D_IDX, :] = (xh * ct + pltpu.roll(xh, D_IDX - IDX_ROPE // 2, 0) * s1t
                                                 + pltpu.roll(xh, IDX_ROPE // 2, 0) * s2t)

    sm = _dot(hb, wsh_ref[...]) + (_dot(hlo, wsh_ref[...]) + _dot(hb, wsl_ref[...]))
    lane = lax.broadcasted_iota(I32, (tm, LANES), 1)
    is_ki = lane < D_IDX
    first = (lane & (D_IDX - 1)) < IDX_ROPE // 2
    c = jnp.where(is_ki, cos_ref[...], 1.0)
    s = jnp.where(is_ki, sin_ref[...], 0.0)
    sm = (sm * c + pltpu.roll(sm, LANES - IDX_ROPE // 2, 1) * jnp.where(first, -s, 0.0)
          + pltpu.roll(sm, IDX_ROPE // 2, 1) * jnp.where(first, 0.0, s))
    is_wi = (lane >= SM_WI) & (lane < SM_DT)
    sm = jnp.where(is_wi, sm * (N_IDX_HEADS ** -0.5), sm)
    sm_ref[...] = sm
    smt_ref[...] = sm.T


def _inproj(x2d, lw, tables, tm):
    n = x2d.shape[0]
    cos_t, sin_t, cos_tt, sin_tt = tables
    tab_blocks = cos_t.shape[0] // tm
    row = lambda i: (i, 0)
    col = lambda i: (0, i)
    const = lambda i: (0, 0)
    rows = lambda w, dt: (jax.ShapeDtypeStruct((n, w), dt), pl.BlockSpec((tm, w), row))
    cols = lambda w, dt: (jax.ShapeDtypeStruct((w, n), dt), pl.BlockSpec((w, tm), col))
    full = lambda a: pl.BlockSpec(a.shape, const)
    heads = (jax.ShapeDtypeStruct((n, N_HEADS_A, HEAD_DIM), F32),
             pl.BlockSpec((tm, N_HEADS_A, HEAD_DIM), lambda i: (i, 0, 0)))
    outs = [cols(ATT_WIDTH, BF16), cols(ATT_WIDTH, BF16), heads, rows(ATT_WIDTH, BF16),
            rows(ATT_WIDTH, F32), heads, cols(VP_ROWS, BF16), rows(ATT_WIDTH, F32), rows(SSM_WIDTH, F32),
            rows(CONV_CH, F32), cols(QI_WIDTH, F32), rows(LANES, F32), cols(LANES, F32)]
    weights = [lw["g_pre"], lw["w_rows"], lw["wq_t"], lw["wqi_t_hi"], lw["wqi_t_lo"], lw["ws_hi"], lw["ws_lo"]]
    return pl.pallas_call(
        _inproj_kernel,
        grid=(n // tm,),
        in_specs=[pl.BlockSpec((tm, D_MODEL), row)] + [full(w) for w in weights]
                 + [pl.BlockSpec((tm, LANES), lambda i: (i % tab_blocks, 0))] * 2
                 + [pl.BlockSpec((LANES, tm), lambda i: (0, i % tab_blocks))] * 2,
        out_specs=[o[1] for o in outs],
        out_shape=[o[0] for o in outs],
        compiler_params=_cparams(("arbitrary",)),
    )(x2d, *weights, cos_t, sin_t, cos_tt, sin_tt)


def _bucket(dist):
    max_exact = NUM_BUCKETS // 2
    n = jnp.maximum(dist, 0)
    nf = jnp.maximum(n, max_exact).astype(F32)
    large = max_exact + jnp.floor(jnp.log(nf / max_exact) / math.log(MAX_DISTANCE / max_exact)
                                  * (NUM_BUCKETS - max_exact)).astype(I32)
    large = jnp.minimum(large, NUM_BUCKETS - 1)
    return jnp.where(n < max_exact, n, large)


def _bias_lookup(bucket, relb_ref, h):
    out = jnp.full(bucket.shape, relb_ref[0, h], F32)
    for b in range(1, NUM_BUCKETS):
        out = jnp.where(bucket == b, relb_ref[b, h], out)
    return out


KEY_NEG_INF = INT_MIN + 0x7FFFFF


def _bit(n):
    return lax.shift_left(jnp.int32(1), jnp.asarray(n, I32))


def _key_to_float(key):
    return pltpu.bitcast(jnp.where(key < 0, key ^ 0x7FFFFFFF, key), F32)


def _search_threshold(count_ge, n_top, n_keys, shape):
    def body(it, thr):
        cand = thr ^ _bit(31 - it)
        cnt = jnp.where(cand < KEY_NEG_INF, jnp.asarray(n_keys, F32), count_ge(_key_to_float(cand)))
        return jnp.where(cnt >= float(n_top), cand, thr)

    return _key_to_float(lax.fori_loop(0, 32, body, jnp.full(shape, INT_MIN, I32)))


def _search_last_tie(count_ties_before, need, pos_bits, shape):
    def body(it, q):
        cand = q | _bit(pos_bits - 1 - it)
        return jnp.where(count_ties_before(cand) < need, cand, q)

    return lax.fori_loop(0, pos_bits, body, jnp.zeros(shape, I32))


def _select_madd(score, pos, thr, last):
    return jnp.where(score > thr, 0.0, jnp.where(score == thr, jnp.where(pos <= last, 0.0, NEG), NEG))


def _bias_tiles_kernel(relb_ref, o_ref, *, tq):
    ki = lax.broadcasted_iota(I32, (tq, tq), 0)
    qi = lax.broadcasted_iota(I32, (tq, tq), 1)
    for kind in range(2):
        bucket = _bucket(qi - ki + kind * tq)
        for h in range(N_HEADS_A):
            o_ref[h, kind] = (_bias_lookup(bucket, relb_ref, h) - relb_ref[NUM_BUCKETS - 1, h]) * LOG2E


def _bias_tiles(rel_bias, tq):
    return pl.pallas_call(
        functools.partial(_bias_tiles_kernel, tq=tq),
        in_specs=[pl.BlockSpec(memory_space=pltpu.SMEM)],
        out_shape=jax.ShapeDtypeStruct((N_HEADS_A, 2, tq, tq), F32),
        compiler_params=pltpu.CompilerParams(vmem_limit_bytes=VMEM_LIMIT),
    )(rel_bias)


KI3_BUILD_ROWS = 512


def _pattn_kernel(qit_ref, smtq_ref, sm_ref, qet_ref, qot_ref, kb_ref, vpt_ref, bt_ref, o_ref,
                  ki3_ref, qh3_ref, sc_ref, last_ref, m_ref, acc_ref, *, tq, n_top, pos_bits):
    i = pl.program_id(1)
    nch = i + 1
    s_len = sc_ref.shape[0]
    groups = tq // SUBLANES
    kiota = lax.broadcasted_iota(I32, (tq, tq), 0)
    qpos = i * tq + lax.broadcasted_iota(I32, (tq, tq), 1)

    def rows(c, width=tq):
        return pl.ds(pl.multiple_of(c * tq, tq), width)

    @pl.when(i == 0)
    def _():
        low = lax.broadcasted_iota(I32, (KI3_BUILD_ROWS, LANES), 1) < D_IDX

        def body(r, carry):
            sl = pl.ds(pl.multiple_of(r * KI3_BUILD_ROWS, KI3_BUILD_ROWS), KI3_BUILD_ROWS)
            x = sm_ref[sl, :]
            hi = x.astype(BF16).astype(F32)
            ki3_ref[sl, 0:LANES] = jnp.where(low, hi, pltpu.roll(hi, D_IDX, 1)).astype(BF16)
            ki3_ref[sl, LANES:2 * LANES] = jnp.where(low, x - hi, 0.0).astype(BF16)
            return carry
        lax.fori_loop(0, s_len // KI3_BUILD_ROWS, body, 0)

    for h in range(N_IDX_HEADS):
        hi, lo = _split(qit_ref[h * D_IDX:(h + 1) * D_IDX, :])
        qh3_ref[h, 0:D_IDX, :] = hi
        qh3_ref[h, D_IDX:2 * D_IDX, :] = lo
        qh3_ref[h, 2 * D_IDX:3 * D_IDX, :] = hi
        qh3_ref[h, 3 * D_IDX:, :] = jnp.zeros((D_IDX, tq), BF16)
    w8 = smtq_ref[SM_WI:SM_WI + N_IDX_HEADS, :] * (D_IDX ** -0.5)

    def score_body(c, carry):
        kc3 = ki3_ref[rows(c), :]
        dots = [_dot(kc3, qh3_ref[h]) for h in range(N_IDX_HEADS)]
        terms = [jnp.maximum(dots[h], 0.0) * w8[h:h + 1, :] for h in range(N_IDX_HEADS)]
        while len(terms) > 1:
            terms = [terms[j] + terms[j + 1] for j in range(0, len(terms), 2)]
        sc_ref[rows(c), :] = jnp.where(c * tq + kiota <= qpos, terms[0], -jnp.inf)
        return carry

    lax.fori_loop(0, nch, score_body, 0)

    def over_keys(x, op):
        x = x.reshape(x.shape[0] // SUBLANES, SUBLANES, tq)
        while x.shape[0] > 1:
            half = x.shape[0] // 2
            x = op(x[:half], x[half:])
        return x[0]

    def count(pred):
        def body(c, acc):
            hit = jnp.where(pred(sc_ref[rows(c), :], c * tq + kiota), 1.0, 0.0)
            return acc + over_keys(hit, jnp.add)
        acc = lax.fori_loop(0, nch, body, jnp.zeros((SUBLANES, tq), F32))
        return jnp.broadcast_to(jnp.sum(acc, axis=0, keepdims=True), (SUBLANES, tq))

    thr = _search_threshold(lambda t: count(lambda sc, pos: sc >= t[0:1, :]), n_top, nch * tq, (SUBLANES, tq))
    thr_row = thr[0:1, :]
    need = float(n_top) - count(lambda sc, pos: sc > thr_row)
    n_eq = count(lambda sc, pos: sc == thr_row)
    last_ref[...] = jnp.full((SUBLANES, tq), 2 ** pos_bits - 1, I32)

    @pl.when(jnp.max(n_eq - need) > 0.0)
    def _():
        last_ref[...] = _search_last_tie(
            lambda q: count(lambda sc, pos: (sc == thr_row) & (pos < q[0:1, :])), need, pos_bits, (SUBLANES, tq))

    last_row = last_ref[0:1, :]

    def madd_body(c, carry):
        pos = c * tq + kiota
        madd = _select_madd(sc_ref[rows(c), :], pos, thr_row, last_row)
        sc_ref[rows(c), :] = jnp.where(pos <= qpos, madd, NEG)
        return carry

    lax.fori_loop(0, nch, madd_body, 0)

    m_ref[...] = jnp.full(m_ref.shape, NEG, F32)
    acc_ref[...] = jnp.zeros(acc_ref.shape, F32)

    def attend(c0, width, bias_of_head):
        sl = rows(c0, width)
        madd = sc_ref[sl, :]
        logits = []
        for h in range(N_HEADS_A):
            p2 = h // 2
            qt = (qet_ref if h % 2 == 0 else qot_ref)[p2 * LANES:(p2 + 1) * LANES, :]
            logits.append(_dot(kb_ref[sl, p2 * LANES:(p2 + 1) * LANES], qt))
        probs, alphas = [], []
        for h in range(N_HEADS_A):
            s = logits[h] + madd
            if bias_of_head is not None:
                s = s + bias_of_head(h)
            m_prev = m_ref[h]
            cmax = over_keys(s, jnp.maximum)
            m_new =jnp.maximum(m_prev, jnp.broadcast_to(jnp.max(cmax, axis=0, keepdims=True), (SUBLANES, tq)))
            probs.append(jnp.exp2(s - m_new[0:1, :]).astype(BF16))
            alphas.append(jnp.exp2(m_prev - m_new)[0:1, :])
            m_ref[h] = m_new
        for h in range(N_HEADS_A):
            pv = _dot(vpt_ref[h * LANES:(h + 1) * LANES, sl], probs[h])
            acc_ref[h] = alphas[h] * acc_ref[h] + pv

    n_far = jnp.maximum(i - 1, 0)

    def far_body(c, carry):
        attend(2 * c, 2 * tq, None)
        return carry

    lax.fori_loop(0, n_far // 2, far_body, 0)

    @pl.when(n_far % 2 == 1)
    def _():
        attend(n_far - 1, tq, None)

    @pl.when(i >= 1)
    def _():
        attend(i - 1, tq, lambda h: bt_ref[h, 1])

    attend(i, tq, lambda h: bt_ref[h, 0])

    outs = []
    for h in range(N_HEADS_A):
        acc = acc_ref[h]
        outs.append(acc[0:HEAD_DIM, :] / acc[HEAD_DIM:, :])
    o_ref[...] = jnp.concatenate(outs, axis=0).T


def _prompt_attention(proj, bias_t, nb, s, tq):
    nq = s // tq
    n_top = min(TOPK_MAX, s // 4)
    pos_bits = max(1, (s - 1).bit_length())
    qcols = lambda b, i: (0, b * nq + i)
    seq_rows = lambda b, i: (b, 0)
    seq_cols = lambda b, i: (0, b)
    return pl.pallas_call(
        functools.partial(_pattn_kernel, tq=tq, n_top=n_top, pos_bits=pos_bits),
        grid=(nb, nq),
        in_specs=[pl.BlockSpec((QI_WIDTH, tq), qcols), pl.BlockSpec((LANES, tq), qcols),
                  pl.BlockSpec((s, LANES), seq_rows),
                  pl.BlockSpec((ATT_WIDTH, tq), qcols), pl.BlockSpec((ATT_WIDTH, tq), qcols),
                  pl.BlockSpec((s, ATT_WIDTH), seq_rows), pl.BlockSpec((VP_ROWS, s), seq_cols),
                  pl.BlockSpec((N_HEADS_A, 2, tq, tq), lambda b, i: (0, 0, 0, 0),
                               pipeline_mode=pl.Buffered(1))],
        out_specs=pl.BlockSpec((tq, ATT_WIDTH), lambda b, i: (b * nq + i, 0)),
        out_shape=jax.ShapeDtypeStruct((nb * s, ATT_WIDTH), F32),
        scratch_shapes=[pltpu.VMEM((s, SPLIT3), BF16), pltpu.VMEM((N_IDX_HEADS, SPLIT3, tq), BF16),
                        pltpu.VMEM((s, tq), F32), pltpu.VMEM((SUBLANES, tq), I32),
                        pltpu.VMEM((N_HEADS_A, SUBLANES, tq), F32), pltpu.VMEM((N_HEADS_A, LANES, tq), F32)],
        compiler_params=_cparams(("arbitrary", "arbitrary")),
    )(proj["qi_t"], proj["small_t"], proj["small"], proj["qe_t"], proj["qo_t"], proj["k_b"], proj["vp_t"], bias_t)


def _ssd_kernel(xbc_ref, z_ref, sm_ref, cw_ref, cb_ref, dtb_ref, alog_ref, dsk_ref, gs_ref, ex_ref, ext_ref,
                h0_ref, c0_ref, y_ref, hf_ref, xp_ref, st_ref, *, t_valid):
    c = pl.program_id(1)
    L = CHUNK

    @pl.when(c == 0)
    def _():
        st_ref[...] = h0_ref[...]
        xp_ref[0:SUBLANES, :] = c0_ref[...]

    xp_ref[SUBLANES:SUBLANES + L, :] = xbc_ref[...]
    conv = cb_ref[...]
    for j in range(CONV_W):
        lo = SUBLANES - (CONV_W - 1) + j
        conv = conv + xp_ref[lo:lo + L, :] * cw_ref[j:j + 1, :]
    xp_ref[0:SUBLANES, :] = xp_ref[L:L + SUBLANES, :]
    act = conv * _sigmoid(conv)
    xs = act[:, :SSM_WIDTH]
    bm = act[:, SSM_WIDTH:SSM_WIDTH + SSM_GROUPS * D_STATE]
    cm = act[:, SSM_WIDTH + SSM_GROUPS * D_STATE:]

    raw = sm_ref[...] + dtb_ref[...]
    dtf = jnp.maximum(raw, 0.0) + jnp.log1p(jnp.exp(-jnp.abs(raw)))
    row = lax.broadcasted_iota(I32, (L, LANES), 0)
    if t_valid < L:
        dtf = jnp.where(row < t_valid, dtf, 0.0)
    adt = dtf * (-jnp.exp(alog_ref[...]))
    tril = row >= lax.broadcasted_iota(I32, (L, LANES), 1)
    cs = _dot(jnp.where(tril, 1.0, 0.0), adt, precision=HIGHEST)
    ex = ex_ref[...]
    dtx = _dot(dtf, ex, precision=HIGHEST)
    csx = _dot(cs, ex, precision=HIGHEST)
    cst = cs.T
    x = xs * dtx
    w = x * jnp.exp(csx[L - 1:L, :] - csx)
    ecsx = jnp.exp(csx)
    dec = jnp.exp(_dot(ext_ref[...], cst, precision=HIGHEST)[:, L - 1:L])
    low = lax.broadcasted_iota(I32, (L, LANES), 1) < SSM_HEAD_DIM

    ys = []
    for p2 in range(SSM_HEADS // 2):
        g = (2 * p2) // (SSM_HEADS // SSM_GROUPS)
        cg = cm[:, g * D_STATE:(g + 1) * D_STATE].astype(BF16)
        bg = bm[:, g * D_STATE:(g + 1) * D_STATE].astype(BF16)
        cb_mat = _nt(cg, bg)
        lanes = slice(p2 * LANES, (p2 + 1) * LANES)
        xp = x[:, lanes].astype(BF16)
        yd = []
        for h in (2 * p2, 2 * p2 + 1):
            diff = cs[:, SM_DT + h:SM_DT + h + 1] - cst[SM_DT + h:SM_DT + h + 1, :]
            lm = jnp.exp(jnp.where(tril, diff, NEG))
            yd.append(_dot((cb_mat * lm).astype(BF16), xp))
        rows = slice(p2 * LANES, (p2 + 1) * LANES)
        st = st_ref[rows, :]
        y_off = _nt(cg, st.astype(BF16)) * ecsx[:, lanes]
        ys.append(jnp.where(low, yd[0], yd[1]) + y_off)
        upd = _dot(w[:, lanes].T.astype(BF16), bg)
        st_ref[rows, :] = st * dec[rows, :] + upd

    y = jnp.concatenate(ys, axis=1) + dsk_ref[...] * xs
    zz = z_ref[...]
    gated = y * (zz * _sigmoid(zz))
    y_ref[...] = gated * lax.rsqrt(jnp.mean(gated * gated, axis=-1, keepdims=True) + EPS) * gs_ref[...]

    @pl.when(c == pl.num_programs(1) - 1)
    def _():
        hf_ref[...] = st_ref[...]


def _ssd(xbc, z, small, lw, h0, c0, nb, s, t_valid):
    nc = s // CHUNK
    blk = lambda b, c: (b * nc + c, 0)
    const = lambda b, c: (0, 0)
    per_b = lambda b, c: (b, 0, 0)
    lanes = np.arange(LANES)
    dt_row = lambda v: jnp.zeros((1, LANES), F32).at[0, SM_DT:SM_DT + SSM_HEADS].set(v)
    expand = (lanes[:, None] == SM_DT + np.arange(SSM_WIDTH)[None, :] // SSM_HEAD_DIM).astype(np.float32)
    state_rows = SSM_HEADS * SSM_HEAD_DIM
    return pl.pallas_call(
        functools.partial(_ssd_kernel, t_valid=t_valid),
        grid=(nb, nc),
        in_specs=[pl.BlockSpec((CHUNK, CONV_CH), blk), pl.BlockSpec((CHUNK, SSM_WIDTH), blk),
                  pl.BlockSpec((CHUNK, LANES), blk),
                  pl.BlockSpec((CONV_W, CONV_CH), const), pl.BlockSpec((1, CONV_CH), const),
                  pl.BlockSpec((1, LANES), const), pl.BlockSpec((1, LANES), const),
                  pl.BlockSpec((1, SSM_WIDTH), const), pl.BlockSpec((1, SSM_WIDTH), const),
                  pl.BlockSpec((LANES, SSM_WIDTH), const), pl.BlockSpec((SSM_WIDTH, LANES), const),
                  pl.BlockSpec((None, state_rows, D_STATE), per_b),
                  pl.BlockSpec((None, SUBLANES, CONV_CH), per_b)],
        out_specs=[pl.BlockSpec((CHUNK, SSM_WIDTH), blk), pl.BlockSpec((None, state_rows, D_STATE), per_b)],
        out_shape=[jax.ShapeDtypeStruct((nb * s, SSM_WIDTH), F32),
                   jax.ShapeDtypeStruct((nb, state_rows, D_STATE), F32)],
        scratch_shapes=[pltpu.VMEM((CHUNK + SUBLANES, CONV_CH), F32), pltpu.VMEM((state_rows, D_STATE), F32)],
        compiler_params=_cparams(("arbitrary", "arbitrary")),
    )(xbc, z, small, lw["conv_w"], lw["conv_b"][None, :], dt_row(lw["dt_bias"]), dt_row(lw["a_log"]),
      jnp.repeat(lw["d_skip"], SSM_HEAD_DIM)[None, :], lw["g_ssm"][None, :], jnp.asarray(expand),
      jnp.asarray(expand.T), h0, c0)


def _outproj_kernel(att_ref, ga_ref, ssm_ref, x_ref, wt_ref, wb_ref, gp_ref, o_ref):
    ga = ga_ref[...]
    att = att_ref[...] * (ga * _sigmoid(ga))
    out = _dot(att.astype(BF16), wt_ref[...]) + _dot(ssm_ref[...].astype(BF16), wb_ref[...])
    o_ref[...] = x_ref[...] + out * lax.rsqrt(jnp.mean(out * out, axis=-1, keepdims=True) + EPS) * gp_ref[...]


def _outproj(att, ga, ssm, x2d, w_top, w_bot, g_post, tm):
    n = x2d.shape[0]
    row = lambda i: (i, 0)
    const = lambda i: (0, 0)
    return pl.pallas_call(
        _outproj_kernel,
        grid=(n // tm,),
        in_specs=[pl.BlockSpec((tm, ATT_WIDTH), row), pl.BlockSpec((tm, ATT_WIDTH), row),
                  pl.BlockSpec((tm, SSM_WIDTH), row), pl.BlockSpec((tm, D_MODEL), row),
                  pl.BlockSpec((ATT_WIDTH, D_MODEL), const), pl.BlockSpec((SSM_WIDTH, D_MODEL), const),
                  pl.BlockSpec((1, D_MODEL), const)],
        out_specs=pl.BlockSpec((tm, D_MODEL), row),
        out_shape=jax.ShapeDtypeStruct((n, D_MODEL), F32),
        compiler_params=_cparams(("arbitrary",)),
    )(att, ga, ssm, x2d, w_top, w_bot, g_post)


SCORE_PAGES = 32
ATTN_PAGES = 16
ROWS_Q = N_HEADS_A * SUBLANES


def _sscore_kernel(pt_ref, qall3_ref, wcol_ref, smt_ref, *rest, past, t_new, n_top, pos_bits):
    pages = rest[:SCORE_PAGES]
    madd_ref = rest[SCORE_PAGES]
    sc_ref = rest[SCORE_PAGES + 1]
    j = pl.program_id(1)
    kw = SCORE_PAGES * PAGE_SIZE
    total = past + LANES
    qall3 = qall3_ref[...]
    wcol = wcol_ref[...]
    zeros = jnp.zeros((D_IDX, PAGE_SIZE), BF16)

    def dots(kt):
        hi, lo = _split(kt)
        return _dot(qall3, jnp.concatenate([hi, hi, lo, zeros], axis=0))

    def weigh(d):
        r = jnp.maximum(d * (D_IDX ** -0.5), 0.0) * wcol
        sc = r[0:SUBLANES]
        for h in range(1, N_IDX_HEADS):
            sc = sc + r[h * SUBLANES:(h + 1) * SUBLANES]
        return sc

    def scores(kt):
        return weigh(dots(kt))

    page_dots = [dots(page[...]) for page in pages]
    for r, d in enumerate(page_dots):
        sl = pl.ds(pl.multiple_of(j * kw + r * PAGE_SIZE, PAGE_SIZE), PAGE_SIZE)
        sc_ref[:, sl] = weigh(d)

    @pl.when(j == pl.num_programs(1) - 1)
    def _():
        lane = lax.broadcasted_iota(I32, (SUBLANES, LANES), 1)
        row = lax.broadcasted_iota(I32, (SUBLANES, LANES), 0)
        vis = (lane <= row) & (lane < t_new)
        sc_ref[:, past:total] = jnp.where(vis, scores(smt_ref[0:D_IDX, :]), -jnp.inf)

        def count(pred):
            acc = jnp.zeros((SUBLANES, LANES), F32)
            for t in range(total // LANES):
                acc = acc + jnp.where(pred(sc_ref[:, t * LANES:(t + 1) * LANES], t * LANES + lane), 1.0, 0.0)
            return jnp.broadcast_to(jnp.sum(acc, axis=1, keepdims=True), (SUBLANES, LANES))

        shape = (SUBLANES, LANES)
        thr = _search_threshold(lambda t: count(lambda sc, pos: sc >= t), n_top, total, shape)
        need = float(n_top) - count(lambda sc, pos: sc > thr)
        last = _search_last_tie(lambda q: count(lambda sc, pos: (sc == thr) & (pos < q)), need, pos_bits, shape)
        for t in range(total // LANES):
            sl = slice(t * LANES, (t + 1) * LANES)
            pos = t * LANES + lane
            madd = _select_madd(sc_ref[:, sl], pos, thr, last)
            if t * LANES >= past:
                madd = jnp.where(vis, madd, NEG)
            madd_ref[:, sl] = madd


def _sattn_kernel(pt_ref, relb_ref, qbd_ref, madd_ref, maddn_ref, kbnew_ref, vnew_ref, *rest, past):
    kpages = rest[:ATTN_PAGES]
    vpages = rest[ATTN_PAGES:2 * ATTN_PAGES]
    o_ref, m_ref, l_ref, acc_ref = rest[2 * ATTN_PAGES:]
    j = pl.program_id(1)
    kw = ATTN_PAGES * PAGE_SIZE

    @pl.when(j == 0)
    def _():
        m_ref[...] = jnp.full(m_ref.shape, NEG, F32)
        l_ref[...] = jnp.zeros(l_ref.shape, F32)
        acc_ref[...] = jnp.zeros(acc_ref.shape, F32)

    qbd = qbd_ref[...]

    def far_bias(width):
        row_head = lax.broadcasted_iota(I32, (ROWS_Q, width), 0) // SUBLANES
        out = jnp.full((ROWS_Q, width), relb_ref[NUM_BUCKETS - 1, 0], F32)
        for h in range(1, N_HEADS_A):
            out = jnp.where(row_head == h, relb_ref[NUM_BUCKETS - 1, h], out)
        return out

    def near_bias(width, pos0):
        tok = lax.broadcasted_iota(I32, (SUBLANES, width), 0)
        pos = pos0 + lax.broadcasted_iota(I32, (SUBLANES, width), 1)
        bucket = _bucket(past + tok - pos)
        return jnp.concatenate([_bias_lookup(bucket, relb_ref, h) for h in range(N_HEADS_A)], axis=0)

    def update(logits, bias, madd8, pv_fn):
        s = logits + (bias * LOG2E + jnp.concatenate([madd8] * N_HEADS_A, axis=0))
        m_prev = m_ref[...]
        m_new = jnp.maximum(m_prev, jnp.broadcast_to(jnp.max(s, axis=1, keepdims=True), (ROWS_Q, LANES)))
        alpha = jnp.exp2(m_prev - m_new)
        p = jnp.exp2(s - m_new[:, :1])
        l_ref[...] = alpha * l_ref[...] + jnp.broadcast_to(jnp.sum(p, axis=1, keepdims=True), (ROWS_Q, LANES))
        acc_ref[...] = alpha[:, :1] * acc_ref[...] + pv_fn(p.astype(BF16))
        m_ref[...] = m_new

    def paged(bias):
        logits = jnp.concatenate([_dot(qbd, kp[...].astype(BF16)) for kp in kpages], axis=1)

        def pv_fn(p):
            pv = _nt(p[:, 0:PAGE_SIZE], vpages[0][...].astype(BF16))
            for r in range(1, ATTN_PAGES):
                pv = pv + _nt(p[:, r * PAGE_SIZE:(r + 1) * PAGE_SIZE], vpages[r][...].astype(BF16))
            return pv

        update(logits, bias, madd_ref[...], pv_fn)

    far = (j + 1) * kw + MAX_DISTANCE <= past + 1

    @pl.when(far)
    def _():
        paged(far_bias(kw))

    @pl.when(jnp.logical_not(far))
    def _():
        paged(near_bias(kw, j * kw))

    @pl.when(j == pl.num_programs(1) - 1)
    def _():
        update(_nt(qbd, kbnew_ref[...]), near_bias(LANES, past), maddn_ref[...],
               lambda p: _dot(p, vnew_ref[...].astype(BF16)))
        o = acc_ref[...] / l_ref[:, :1]
        own = (lax.broadcasted_iota(I32, (ROWS_Q, ATT_WIDTH), 0) // SUBLANES
               == lax.broadcasted_iota(I32, (ROWS_Q, ATT_WIDTH), 1) // HEAD_DIM)
        o = jnp.where(own, o, 0.0)
        out = o[0:SUBLANES]
        for h in range(1, N_HEADS_A):
            out = out + o[h * SUBLANES:(h + 1) * SUBLANES]
        o_ref[...] = out


def _sample_attention(page_table, rel_bias, proj, cache_k, cache_v, cache_kidx, nb, t_new):
    n_pages = page_table.shape[1]
    past = n_pages * PAGE_SIZE
    total = past + LANES
    n_top = min(TOPK_MAX, (past + t_new) // 4)
    pos_bits = max(1, (total - 1).bit_length())
    pt = page_table.reshape(-1)
    pool = cache_kidx.shape[0]

    def tok(a_t):
        return a_t.reshape(a_t.shape[0], nb, CHUNK)[:, :, :SUBLANES].transpose(1, 2, 0)

    by_head = lambda a, d: a.reshape(nb, SUBLANES, -1, d).transpose(0, 2, 1, 3)
    qall = by_head(tok(proj["qi_t"]), D_IDX).reshape(nb, ROWS_Q, D_IDX)
    hi, lo = _split(qall)
    qall3 = jnp.concatenate([hi, lo, hi, jnp.zeros_like(hi)], axis=-1)
    wi_t = tok(proj["small_t"])[:, :, SM_WI:SM_WI + N_IDX_HEADS].transpose(0, 2, 1).reshape(nb, ROWS_Q, 1)
    wcol = jnp.broadcast_to(wi_t, (nb, ROWS_Q, LANES))
    q_t = by_head(tok(proj["qe_t"] + proj["qo_t"]), HEAD_DIM)
    eye = jnp.eye(N_HEADS_A, dtype=q_t.dtype)
    qbd = (q_t[:, :, :, None, :] * eye[None, :, None, :, None]).reshape(nb, ROWS_Q, ATT_WIDTH)
    ckt = cache_k.transpose(0, 2, 3, 1).reshape(pool, ATT_WIDTH, PAGE_SIZE)
    cvt = cache_v.transpose(0, 2, 3, 1).reshape(pool, ATT_WIDTH, PAGE_SIZE)
    cit = cache_kidx.transpose(0, 2, 1)

    def page(r, per_step):
        return lambda b, j, pt_ref: (pt_ref[b * n_pages + j * per_step + r], 0, 0)

    seq3 = lambda b, j, pt_ref: (b, 0, 0)
    tcol = lambda b, j, pt_ref: (0, b)
    trow = lambda b, j, pt_ref: (b, 0)

    madd = pl.pallas_call(
        functools.partial(_sscore_kernel, past=past, t_new=t_new, n_top=n_top, pos_bits=pos_bits),
        grid_spec=pltpu.PrefetchScalarGridSpec(
            num_scalar_prefetch=1, grid=(nb, n_pages // SCORE_PAGES),
            in_specs=[pl.BlockSpec((None, ROWS_Q, SPLIT3), seq3), pl.BlockSpec((None, ROWS_Q, LANES), seq3),
                      pl.BlockSpec((LANES, CHUNK), tcol)]
                     + [pl.BlockSpec((None, D_IDX, PAGE_SIZE), page(r, SCORE_PAGES)) for r in range(SCORE_PAGES)],
            out_specs=pl.BlockSpec((None, SUBLANES, total), seq3),
            scratch_shapes=[pltpu.VMEM((SUBLANES, total), F32)]),
        out_shape=jax.ShapeDtypeStruct((nb, SUBLANES, total), F32),
        compiler_params=_cparams(("arbitrary", "arbitrary")),
    )(pt, qall3, wcol, proj["small_t"], *([cit] * SCORE_PAGES))

    kw = ATTN_PAGES * PAGE_SIZE
    kv_specs = [pl.BlockSpec((None, ATT_WIDTH, PAGE_SIZE), page(r, ATTN_PAGES)) for r in range(ATTN_PAGES)]
    return pl.pallas_call(
        functools.partial(_sattn_kernel, past=past),
        grid_spec=pltpu.PrefetchScalarGridSpec(
            num_scalar_prefetch=1, grid=(nb, n_pages // ATTN_PAGES),
            in_specs=[pl.BlockSpec(memory_space=pltpu.SMEM),
                      pl.BlockSpec((None, ROWS_Q, ATT_WIDTH), seq3),
                      pl.BlockSpec((None, SUBLANES, kw), lambda b, j, pt_ref: (b, 0, j)),
                      pl.BlockSpec((None, SUBLANES, LANES), lambda b, j, pt_ref: (b, 0, past // LANES)),
                      pl.BlockSpec((CHUNK, ATT_WIDTH), trow), pl.BlockSpec((CHUNK, ATT_WIDTH), trow)]
                     + kv_specs * 2,
            out_specs=pl.BlockSpec((None, SUBLANES, ATT_WIDTH), seq3),
            scratch_shapes=[pltpu.VMEM((ROWS_Q, LANES), F32), pltpu.VMEM((ROWS_Q, LANES), F32),
                            pltpu.VMEM((ROWS_Q, ATT_WIDTH), F32)]),
        out_shape=jax.ShapeDtypeStruct((nb, SUBLANES, ATT_WIDTH), F32),
        compiler_params=_cparams(("arbitrary", "arbitrary")),
    )(pt, rel_bias, qbd, madd, madd, proj["k_b"], proj["v"], *([ckt] * ATTN_PAGES), *([cvt] * ATTN_PAGES))


TM_PROJ = 256
TQ_PROMPT = 256
PROJ_NAMES = ("qe_t", "qo_t", "k3", "k_b", "v", "v3", "vp_t", "ga", "z", "xbc", "qi_t", "small", "small_t")


def _layer_weights(g_pre, w_in, conv_w, conv_b, dt_bias, a_log, d_skip, g_ssm, w_out, g_post):
    offs = np.cumsum([0, ATT_WIDTH, ATT_WIDTH, ATT_WIDTH, ATT_WIDTH, QI_WIDTH, D_IDX, N_IDX_HEADS,
                      SSM_WIDTH, CONV_CH, SSM_HEADS])
    q, k, v, ga, qi, ki, wi, z, xbc, dt = [w_in[:, offs[n]:offs[n + 1]] for n in range(10)]
    pad = jnp.zeros((D_MODEL, LANES - D_IDX - N_IDX_HEADS - SSM_HEADS), F32)
    wqi_hi, wqi_lo = _split(qi.T)
    ws_hi, ws_lo = _split(jnp.concatenate([ki, wi, dt, pad], axis=1))
    return dict(g_pre=g_pre[None, :], w_rows=jnp.concatenate([k, v, ga, z, xbc], axis=1).astype(BF16),
                wq_t=q.T.astype(BF16), wqi_t_hi=wqi_hi, wqi_t_lo=wqi_lo, ws_hi=ws_hi, ws_lo=ws_lo,
                conv_w=conv_w, conv_b=conv_b, dt_bias=dt_bias, a_log=a_log, d_skip=d_skip, g_ssm=g_ssm,
                w_top=w_out[:ATT_WIDTH].astype(BF16), w_bot=w_out[ATT_WIDTH:].astype(BF16), g_post=g_post[None, :])


def _mixer(x, lw, pos_off, t_valid, h0, c0, attn_fn):
    nb, s, _ = x.shape
    tm = min(TM_PROJ, s)
    x2d = x.reshape(nb * s, D_MODEL)
    proj = dict(zip(PROJ_NAMES, _inproj(x2d, lw, _rope_tables(s, pos_off), tm)))
    att = attn_fn(proj)
    ssm, h_final = _ssd(proj["xbc"], proj["z"], proj["small"], lw, h0, c0, nb, s, t_valid)
    y = _outproj(att, proj["ga"], ssm, x2d, lw["w_top"], lw["w_bot"], lw["g_post"], tm)
    r = lambda a: a.reshape(nb, s, -1)[:, :t_valid]
    heads = lambda a: a.reshape(nb, s, N_HEADS_A, HEAD_DIM)[:, :t_valid]
    conv_state = r(proj["xbc"])[:, t_valid - (CONV_W - 1):]
    return (r(y), heads(proj["k3"]), heads(proj["v3"]), r(proj["small"])[..., :D_IDX],
            h_final.reshape(nb, SSM_HEADS, SSM_HEAD_DIM, D_STATE), conv_state)


def kernel(x_prompt, x_sample, cache_k, cache_v, cache_kidx, state_ssm, state_conv, page_table, g_pre, w_in, conv_w, conv_b, dt_bias, a_log, d_skip, g_ssm, w_out, g_post, rel_bias):
    depth = w_in.shape[0]
    bp, sp, _ = x_prompt.shape
    bs, ts, _ = x_sample.shape
    past = page_table.shape[1] * PAGE_SIZE
    assert ts <= SUBLANES and ts >= CONV_W - 1 and sp % max(TQ_PROMPT, KI3_BUILD_ROWS) == 0
    assert page_table.shape[1] % SCORE_PAGES == 0 and page_table.shape[1] % ATTN_PAGES == 0
    state_rows = SSM_HEADS * SSM_HEAD_DIM
    bias_t = _bias_tiles(rel_bias, TQ_PROMPT)

    yp = x_prompt
    ys = jnp.pad(x_sample, ((0, 0), (0, CHUNK - ts), (0, 0)))
    outs_p, outs_s = [], []
    for l in range(depth):
        lw = _layer_weights(g_pre[l], w_in[l], conv_w[l], conv_b[l], dt_bias[l], a_log[l], d_skip[l], g_ssm[l],
                            w_out[l], g_post[l])

        def prompt_attn(proj):
            return _prompt_attention(proj, bias_t, bp, sp, TQ_PROMPT)

        def sample_attn(proj, layer=l):
            att8 = _sample_attention(page_table, rel_bias, proj, cache_k[layer], cache_v[layer], cache_kidx[layer],
                                     bs, ts)
            return jnp.pad(att8, ((0, 0), (0, CHUNK - SUBLANES), (0, 0))).reshape(bs * CHUNK, ATT_WIDTH)

        op = _mixer(yp, lw, 0, sp, jnp.zeros((bp, state_rows, D_STATE), F32),
                    jnp.zeros((bp, SUBLANES, CONV_CH), F32), prompt_attn)
        c0 = jnp.pad(state_conv[l], ((0, 0), (SUBLANES - (CONV_W - 1), 0), (0, 0)))
        os_ = _mixer(ys, lw, past, ts, state_ssm[l].reshape(bs, state_rows, D_STATE), c0, sample_attn)
        yp = op[0]
        ys = jnp.pad(os_[0], ((0, 0), (0, CHUNK - ts), (0, 0)))
        outs_p.append(op[1:])
        outs_s.append(os_[1:])
    stack = lambda outs, n: jnp.stack([o[n] for o in outs])
    return (yp, ys[:, :ts], *[stack(outs_p, n) for n in range(5)], *[stack(outs_s, n) for n in range(5)])
```

```python
import functools
import math

import jax
import jax.numpy as jnp
import numpy as np
from jax import lax
from jax.experimental import pallas as pl
from jax.experimental.pallas import tpu as pltpu

F32 = jnp.float32
BF16 = jnp.bfloat16
I32 = jnp.int32
HIGHEST = lax.Precision.HIGHEST

D_MODEL = 1024
PAGE_SIZE = 128
HEAD_DIM = 64
ATT_WIDTH = 512
N_HEADS_A = 8
N_IDX_HEADS = 8
D_IDX = 64
IDX_ROPE = 32
ROPE_BASE = 10000.0
TOPK_MAX = 256
NUM_BUCKETS = 32
MAX_DISTANCE = 128
SSM_WIDTH = 512
SSM_HEAD_DIM = 64
SSM_HEADS = 8
SSM_GROUPS = 2
D_STATE = 128
CONV_W = 4
CONV_CH = 1024
CHUNK = 128
EPS = 1e-6

LANES = 128
SUBLANES = 8
VMEM_LIMIT = 56 * 1024 * 1024
NEG = -1e30
INT_MIN = -2 ** 31
LOG2E = 1.4426950408889634

SM_WI = D_IDX
SM_DT = D_IDX + N_IDX_HEADS
ROW_COLS = 3 * ATT_WIDTH + SSM_WIDTH + CONV_CH
QI_WIDTH = N_IDX_HEADS * D_IDX
VP_ROWS = N_HEADS_A * LANES
SPLIT3 = 4 * D_IDX


def _nt(a, b, **kw):
    return lax.dot_general(a, b, (((1,), (1,)), ((), ())), preferred_element_type=F32, **kw)


def _dot(a, b, **kw):
    return jnp.dot(a, b, preferred_element_type=F32, **kw)


def _split(x):
    hi = x.astype(BF16)
    return hi, (x - hi.astype(F32)).astype(BF16)


def _split3(x):
    hi = x.astype(BF16)
    rest = x - hi.astype(F32)
    mid = rest.astype(BF16)
    return hi, mid, (rest - mid.astype(F32)).astype(BF16)


def _sigmoid(x):
    return 1.0 / (1.0 + jnp.exp(-x))


def _cparams(sem):
    return pltpu.CompilerParams(dimension_semantics=sem, vmem_limit_bytes=VMEM_LIMIT)


def _rope_table_kernel(inv_ref, cos_ref, sin_ref, cost_ref, sint_ref, *, pos_off):
    rows = cos_ref.shape[0]
    pos = (lax.broadcasted_iota(I32, (rows, LANES), 0) + pos_off).astype(F32)
    ang = pos * inv_ref[...]
    c = jnp.cos(ang)
    s = jnp.sin(ang)
    cos_ref[...] = c
    sin_ref[...] = s
    cost_ref[...] = c.T
    sint_ref[...] = s.T


def _rope_tables(rows, pos_off):
    inv = ROPE_BASE ** (-jnp.arange(0, IDX_ROPE, 2, dtype=F32) / IDX_ROPE)
    l64 = np.arange(LANES) % D_IDX
    inv_row = jnp.where(l64 < IDX_ROPE, inv[l64 % (IDX_ROPE // 2)], 0.0).astype(F32)[None, :]
    return pl.pallas_call(
        functools.partial(_rope_table_kernel, pos_off=pos_off),
        out_shape=(jax.ShapeDtypeStruct((rows, LANES), F32),) * 2 + (jax.ShapeDtypeStruct((LANES, rows), F32),) * 2,
    )(inv_row)


def _inproj_kernel(x_ref, g_ref, wr_ref, wqt_ref, wqih_ref, wqil_ref, wsh_ref, wsl_ref,
                   cos_ref, sin_ref, cost_ref, sint_ref,
                   qet_ref, qot_ref, k3_ref, kb_ref, v_ref, v3_ref, vpt_ref, ga_ref, z_ref, xbc_ref, qit_ref, sm_ref,
                   smt_ref):
    x = x_ref[...]
    hn = x * lax.rsqrt(jnp.mean(x * x, axis=-1, keepdims=True) + EPS) * g_ref[...]
    hb, hlo = _split(hn)
    tm = x.shape[0]

    def rows(lo, width):
        return _dot(hb, wr_ref[:, lo:lo + width])

    k = rows(0, ATT_WIDTH)
    kb_ref[...] = k.astype(BF16)
    v = rows(ATT_WIDTH, ATT_WIDTH)
    v_ref[...] = v
    for h in range(N_HEADS_A):
        k3_ref[:, h, :] = k[:, h * HEAD_DIM:(h + 1) * HEAD_DIM]
        v3_ref[:, h, :] = v[:, h * HEAD_DIM:(h + 1) * HEAD_DIM]
    ga_ref[...] = rows(2 * ATT_WIDTH, ATT_WIDTH)
    z_ref[...] = rows(3 * ATT_WIDTH, SSM_WIDTH)
    xbc_ref[...] = rows(3 * ATT_WIDTH + SSM_WIDTH, CONV_CH)

    vt = v.T
    ones = jnp.ones((HEAD_DIM, tm), BF16)
    for h in range(N_HEADS_A):
        vpt_ref[h * LANES:h * LANES + HEAD_DIM, :] = vt[h * HEAD_DIM:(h + 1) * HEAD_DIM, :].astype(BF16)
        vpt_ref[h * LANES + HEAD_DIM:(h + 1) * LANES, :] = ones

    qt = _nt(wqt_ref[...], hb) * (HEAD_DIM ** -0.5 * LOG2E)
    even = (lax.broadcasted_iota(I32, qt.shape, 0) & HEAD_DIM) == 0
    qet_ref[...] = jnp.where(even, qt, 0.0).astype(BF16)
    qot_ref[...] = jnp.where(even, 0.0, qt).astype(BF16)

    qit = _nt(wqih_ref[...], hb) + (_nt(wqih_ref[...], hlo) + _nt(wqil_ref[...], hb))
    ct = cost_ref[0:D_IDX, :]
    st = sint_ref[0:D_IDX, :]
    first_t = lax.broadcasted_iota(I32, (D_IDX, tm), 0) < IDX_ROPE // 2
    s1t = jnp.where(first_t, -st, 0.0)
    s2t = jnp.where(first_t, 0.0, st)
    for h in range(N_IDX_HEADS):
        xh = qit[h * D_IDX:(h + 1) * D_IDX, :]
        qit_ref[h * D_IDX:(h + 1) * D_IDX, :] = (xh * ct + pltpu.roll(xh, D_IDX - IDX_ROPE // 2, 0) * s1t
                                                 + pltpu.roll(xh, IDX_ROPE // 2, 0) * s2t)

    sm = _dot(hb, wsh_ref[...]) + (_dot(hlo, wsh_ref[...]) + _dot(hb, wsl_ref[...]))
    lane = lax.broadcasted_iota(I32, (tm, LANES), 1)
    is_ki = lane < D_IDX
    first = (lane & (D_IDX - 1)) < IDX_ROPE // 2
    c = jnp.where(is_ki, cos_ref[...], 1.0)
    s = jnp.where(is_ki, sin_ref[...], 0.0)
    sm = (sm * c + pltpu.roll(sm, LANES - IDX_ROPE // 2, 1) * jnp.where(first, -s, 0.0)
          + pltpu.roll(sm, IDX_ROPE // 2, 1) * jnp.where(first, 0.0, s))
    is_wi = (lane >= SM_WI) & (lane < SM_DT)
    sm = jnp.where(is_wi, sm * (N_IDX_HEADS ** -0.5), sm)
    sm_ref[...] = sm
    smt_ref[...] = sm.T


def _inproj(x2d, lw, tables, tm):
    n = x2d.shape[0]
    cos_t, sin_t, cos_tt, sin_tt = tables
    tab_blocks = cos_t.shape[0] // tm
    row = lambda i: (i, 0)
    col = lambda i: (0, i)
    const = lambda i: (0, 0)
    rows = lambda w, dt: (jax.ShapeDtypeStruct((n, w), dt), pl.BlockSpec((tm, w), row))
    cols = lambda w, dt: (jax.ShapeDtypeStruct((w, n), dt), pl.BlockSpec((w, tm), col))
    full = lambda a: pl.BlockSpec(a.shape, const)
    heads = (jax.ShapeDtypeStruct((n, N_HEADS_A, HEAD_DIM), F32),
             pl.BlockSpec((tm, N_HEADS_A, HEAD_DIM), lambda i: (i, 0, 0)))
    outs = [cols(ATT_WIDTH, BF16), cols(ATT_WIDTH, BF16), heads, rows(ATT_WIDTH, BF16),
            rows(ATT_WIDTH, F32), heads, cols(VP_ROWS, BF16), rows(ATT_WIDTH, F32), rows(SSM_WIDTH, F32),
            rows(CONV_CH, F32), cols(QI_WIDTH, F32), rows(LANES, F32), cols(LANES, F32)]
    weights = [lw["g_pre"], lw["w_rows"], lw["wq_t"], lw["wqi_t_hi"], lw["wqi_t_lo"], lw["ws_hi"], lw["ws_lo"]]
    return pl.pallas_call(
        _inproj_kernel,
        grid=(n // tm,),
        in_specs=[pl.BlockSpec((tm, D_MODEL), row)] + [full(w) for w in weights]
                 + [pl.BlockSpec((tm, LANES), lambda i: (i % tab_blocks, 0))] * 2
                 + [pl.BlockSpec((LANES, tm), lambda i: (0, i % tab_blocks))] * 2,
        out_specs=[o[1] for o in outs],
        out_shape=[o[0] for o in outs],
        compiler_params=_cparams(("arbitrary",)),
    )(x2d, *weights, cos_t, sin_t, cos_tt, sin_tt)


def _bucket(dist):
    max_exact = NUM_BUCKETS // 2
    n = jnp.maximum(dist, 0)
    nf = jnp.maximum(n, max_exact).astype(F32)
    large = max_exact + jnp.floor(jnp.log(nf / max_exact) / math.log(MAX_DISTANCE / max_exact)
                                  * (NUM_BUCKETS - max_exact)).astype(I32)
    large = jnp.minimum(large, NUM_BUCKETS - 1)
    return jnp.where(n < max_exact, n, large)


def _bias_lookup(bucket, relb_ref, h):
    out = jnp.full(bucket.shape, relb_ref[0, h], F32)
    for b in range(1, NUM_BUCKETS):
        out = jnp.where(bucket == b, relb_ref[b, h], out)
    return out


KEY_NEG_INF = INT_MIN + 0x7FFFFF


def _bit(n):
    return lax.shift_left(jnp.int32(1), jnp.asarray(n, I32))


def _key_to_float(key):
    return pltpu.bitcast(jnp.where(key < 0, key ^ 0x7FFFFFFF, key), F32)


def _search_threshold(count_ge, n_top, n_keys, shape):
    def body(it, thr):
        cand = thr ^ _bit(31 - it)
        cnt = jnp.where(cand < KEY_NEG_INF, jnp.asarray(n_keys, F32), count_ge(_key_to_float(cand)))
        return jnp.where(cnt >= float(n_top), cand, thr)

    return _key_to_float(lax.fori_loop(0, 32, body, jnp.full(shape, INT_MIN, I32)))


def _search_last_tie(count_ties_before, need, pos_bits, shape):
    def body(it, q):
        cand = q | _bit(pos_bits - 1 - it)
        return jnp.where(count_ties_before(cand) < need, cand, q)

    return lax.fori_loop(0, pos_bits, body, jnp.zeros(shape, I32))


def _select_madd(score, pos, thr, last):
    return jnp.where(score > thr, 0.0, jnp.where(score == thr, jnp.where(pos <= last, 0.0, NEG), NEG))


def _bias_tiles_kernel(relb_ref, o_ref, *, tq):
    ki = lax.broadcasted_iota(I32, (tq, tq), 0)
    qi = lax.broadcasted_iota(I32, (tq, tq), 1)
    for kind in range(2):
        bucket = _bucket(qi - ki + kind * tq)
        for h in range(N_HEADS_A):
            o_ref[h, kind] = (_bias_lookup(bucket, relb_ref, h) - relb_ref[NUM_BUCKETS - 1, h]) * LOG2E


def _bias_tiles(rel_bias, tq):
    return pl.pallas_call(
        functools.partial(_bias_tiles_kernel, tq=tq),
        in_specs=[pl.BlockSpec(memory_space=pltpu.SMEM)],
        out_shape=jax.ShapeDtypeStruct((N_HEADS_A, 2, tq, tq), F32),
        compiler_params=pltpu.CompilerParams(vmem_limit_bytes=VMEM_LIMIT),
    )(rel_bias)


KI3_BUILD_ROWS = 512


def _pattn_kernel(qit_ref, smtq_ref, sm_ref, qet_ref, qot_ref, kb_ref, vpt_ref, bt_ref, o_ref,
                  ki3_ref, qh3_ref, sc_ref, last_ref, m_ref, acc_ref, *, tq, n_top, pos_bits):
    i = pl.program_id(1)
    nch = i + 1
    s_len = sc_ref.shape[0]
    groups = tq // SUBLANES
    kiota = lax.broadcasted_iota(I32, (tq, tq), 0)
    qpos = i * tq + lax.broadcasted_iota(I32, (tq, tq), 1)

    def rows(c, width=tq):
        return pl.ds(pl.multiple_of(c * tq, tq), width)

    @pl.when(i == 0)
    def _():
        low = lax.broadcasted_iota(I32, (KI3_BUILD_ROWS, LANES), 1) < D_IDX

        def body(r, carry):
            sl = pl.ds(pl.multiple_of(r * KI3_BUILD_ROWS, KI3_BUILD_ROWS), KI3_BUILD_ROWS)
            x = sm_ref[sl, :]
            hi = x.astype(BF16).astype(F32)
            ki3_ref[sl, 0:LANES] = jnp.where(low, hi, pltpu.roll(hi, D_IDX, 1)).astype(BF16)
            ki3_ref[sl, LANES:2 * LANES] = jnp.where(low, x - hi, 0.0).astype(BF16)
            return carry
        lax.fori_loop(0, s_len // KI3_BUILD_ROWS, body, 0)

    for h in range(N_IDX_HEADS):
        hi, lo = _split(qit_ref[h * D_IDX:(h + 1) * D_IDX, :])
        qh3_ref[h, 0:D_IDX, :] = hi
        qh3_ref[h, D_IDX:2 * D_IDX, :] = lo
        qh3_ref[h, 2 * D_IDX:3 * D_IDX, :] = hi
        qh3_ref[h, 3 * D_IDX:, :] = jnp.zeros((D_IDX, tq), BF16)
    w8 = smtq_ref[SM_WI:SM_WI + N_IDX_HEADS, :] * (D_IDX ** -0.5)

    def score_body(c, carry):
        kc3 = ki3_ref[rows(c), :]
        dots = [_dot(kc3, qh3_ref[h]) for h in range(N_IDX_HEADS)]
        terms = [jnp.maximum(dots[h], 0.0) * w8[h:h + 1, :] for h in range(N_IDX_HEADS)]
        while len(terms) > 1:
            terms = [terms[j] + terms[j + 1] for j in range(0, len(terms), 2)]
        sc_ref[rows(c), :] = jnp.where(c * tq + kiota <= qpos, terms[0], -jnp.inf)
        return carry

    lax.fori_loop(0, nch, score_body, 0)

    def over_keys(x, op):
        x = x.reshape(x.shape[0] // SUBLANES, SUBLANES, tq)
        while x.shape[0] > 1:
            half = x.shape[0] // 2
            x = op(x[:half], x[half:])
        return x[0]

    def count(pred):
        def body(c, acc):
            hit = jnp.where(pred(sc_ref[rows(c), :], c * tq + kiota), 1.0, 0.0)
            return acc + over_keys(hit, jnp.add)
        acc = lax.fori_loop(0, nch, body, jnp.zeros((SUBLANES, tq), F32))
        return jnp.broadcast_to(jnp.sum(acc, axis=0, keepdims=True), (SUBLANES, tq))

    thr = _search_threshold(lambda t: count(lambda sc, pos: sc >= t[0:1, :]), n_top, nch * tq, (SUBLANES, tq))
    thr_row = thr[0:1, :]
    need = float(n_top) - count(lambda sc, pos: sc > thr_row)
    n_eq = count(lambda sc, pos: sc == thr_row)
    last_ref[...] = jnp.full((SUBLANES, tq), 2 ** pos_bits - 1, I32)

    @pl.when(jnp.max(n_eq - need) > 0.0)
    def _():
        last_ref[...] = _search_last_tie(
            lambda q: count(lambda sc, pos: (sc == thr_row) & (pos < q[0:1, :])), need, pos_bits, (SUBLANES, tq))

    last_row = last_ref[0:1, :]

    def madd_body(c, carry):
        pos = c * tq + kiota
        madd = _select_madd(sc_ref[rows(c), :], pos, thr_row, last_row)
        sc_ref[rows(c), :] = jnp.where(pos <= qpos, madd, NEG)
        return carry

    lax.fori_loop(0, nch, madd_body, 0)

    m_ref[...] = jnp.full(m_ref.shape, NEG, F32)
    acc_ref[...] = jnp.zeros(acc_ref.shape, F32)

    def attend(c0, width, bias_of_head):
        sl = rows(c0, width)
        madd = sc_ref[sl, :]
        logits = []
        for h in range(N_HEADS_A):
            p2 = h // 2
            qt = (qet_ref if h % 2 == 0 else qot_ref)[p2 * LANES:(p2 + 1) * LANES, :]
            logits.append(_dot(kb_ref[sl, p2 * LANES:(p2 + 1) * LANES], qt))
        probs, alphas = [], []
        for h in range(N_HEADS_A):
            s = logits[h] + madd
            if bias_of_head is not None:
                s = s + bias_of_head(h)
            m_prev = m_ref[h]
            cmax = over_keys(s, jnp.maximum)
            m_new =jnp.maximum(m_prev, jnp.broadcast_to(jnp.max(cmax, axis=0, keepdims=True), (SUBLANES, tq)))
            probs.append(jnp.exp2(s - m_new[0:1, :]).astype(BF16))
            alphas.append(jnp.exp2(m_prev - m_new)[0:1, :])
            m_ref[h] = m_new
        for h in range(N_HEADS_A):
            pv = _dot(vpt_ref[h * LANES:(h + 1) * LANES, sl], probs[h])
            acc_ref[h] = alphas[h] * acc_ref[h] + pv

    n_far = jnp.maximum(i - 1, 0)

    def far_body(c, carry):
        attend(2 * c, 2 * tq, None)
        return carry

    lax.fori_loop(0, n_far // 2, far_body, 0)

    @pl.when(n_far % 2 == 1)
    def _():
        attend(n_far - 1, tq, None)

    @pl.when(i >= 1)
    def _():
        attend(i - 1, tq, lambda h: bt_ref[h, 1])

    attend(i, tq, lambda h: bt_ref[h, 0])

    outs = []
    for h in range(N_HEADS_A):
        acc = acc_ref[h]
        outs.append(acc[0:HEAD_DIM, :] / acc[HEAD_DIM:, :])
    o_ref[...] = jnp.concatenate(outs, axis=0).T


def _prompt_attention(proj, bias_t, nb, s, tq):
    nq = s // tq
    n_top = min(TOPK_MAX, s // 4)
    pos_bits = max(1, (s - 1).bit_length())
    qcols = lambda b, i: (0, b * nq + i)
    seq_rows = lambda b, i: (b, 0)
    seq_cols = lambda b, i: (0, b)
    return pl.pallas_call(
        functools.partial(_pattn_kernel, tq=tq, n_top=n_top, pos_bits=pos_bits),
        grid=(nb, nq),
        in_specs=[pl.BlockSpec((QI_WIDTH, tq), qcols), pl.BlockSpec((LANES, tq), qcols),
                  pl.BlockSpec((s, LANES), seq_rows),
                  pl.BlockSpec((ATT_WIDTH, tq), qcols), pl.BlockSpec((ATT_WIDTH, tq), qcols),
                  pl.BlockSpec((s, ATT_WIDTH), seq_rows), pl.BlockSpec((VP_ROWS, s), seq_cols),
                  pl.BlockSpec((N_HEADS_A, 2, tq, tq), lambda b, i: (0, 0, 0, 0),
                               pipeline_mode=pl.Buffered(1))],
        out_specs=pl.BlockSpec((tq, ATT_WIDTH), lambda b, i: (b * nq + i, 0)),
        out_shape=jax.ShapeDtypeStruct((nb * s, ATT_WIDTH), F32),
        scratch_shapes=[pltpu.VMEM((s, SPLIT3), BF16), pltpu.VMEM((N_IDX_HEADS, SPLIT3, tq), BF16),
                        pltpu.VMEM((s, tq), F32), pltpu.VMEM((SUBLANES, tq), I32),
                        pltpu.VMEM((N_HEADS_A, SUBLANES, tq), F32), pltpu.VMEM((N_HEADS_A, LANES, tq), F32)],
        compiler_params=_cparams(("arbitrary", "arbitrary")),
    )(proj["qi_t"], proj["small_t"], proj["small"], proj["qe_t"], proj["qo_t"], proj["k_b"], proj["vp_t"], bias_t)


def _ssd_kernel(xbc_ref, z_ref, sm_ref, cw_ref, cb_ref, dtb_ref, alog_ref, dsk_ref, gs_ref, ex_ref, ext_ref,
                h0_ref, c0_ref, att_ref, ga_ref, x_ref, wt_ref, wb_ref, gp_ref, y_ref, hf_ref, xp_ref, st_ref, *,
                t_valid):
    c = pl.program_id(1)
    L = CHUNK

    @pl.when(c == 0)
    def _():
        st_ref[...] = h0_ref[...]
        xp_ref[0:SUBLANES, :] = c0_ref[...]

    xp_ref[SUBLANES:SUBLANES + L, :] = xbc_ref[...]
    conv = cb_ref[...]
    for j in range(CONV_W):
        lo = SUBLANES - (CONV_W - 1) + j
        conv = conv + xp_ref[lo:lo + L, :] * cw_ref[j:j + 1, :]
    xp_ref[0:SUBLANES, :] = xp_ref[L:L + SUBLANES, :]
    act = conv * _sigmoid(conv)
    xs = act[:, :SSM_WIDTH]
    bm = act[:, SSM_WIDTH:SSM_WIDTH + SSM_GROUPS * D_STATE]
    cm = act[:, SSM_WIDTH + SSM_GROUPS * D_STATE:]

    raw = sm_ref[...] + dtb_ref[...]
    dtf = jnp.maximum(raw, 0.0) + jnp.log1p(jnp.exp(-jnp.abs(raw)))
    row = lax.broadcasted_iota(I32, (L, LANES), 0)
    if t_valid < L:
        dtf = jnp.where(row < t_valid, dtf, 0.0)
    adt = dtf * (-jnp.exp(alog_ref[...]))
    tril = row >= lax.broadcasted_iota(I32, (L, LANES), 1)
    tril01 = jnp.where(tril, 1.0, 0.0).astype(BF16)
    cs = sum(_dot(tril01, part) for part in _split3(adt))
    ex = ex_ref[...]
    dtx = sum(_dot(part, ex) for part in _split3(dtf))
    csx = sum(_dot(part, ex) for part in _split3(cs))
    cst = cs.T
    x = xs * dtx
    w = x * jnp.exp(csx[L - 1:L, :] - csx)
    ecsx = jnp.exp(csx)
    dec = jnp.exp(jnp.sum(ext_ref[...] * cs[L - 1:L, :], axis=1, keepdims=True))
    low = lax.broadcasted_iota(I32, (L, LANES), 1) < SSM_HEAD_DIM

    ys = []
    for p2 in range(SSM_HEADS // 2):
        g = (2 * p2) // (SSM_HEADS // SSM_GROUPS)
        cg = cm[:, g * D_STATE:(g + 1) * D_STATE].astype(BF16)
        bg = bm[:, g * D_STATE:(g + 1) * D_STATE].astype(BF16)
        cb_mat = _nt(cg, bg)
        lanes = slice(p2 * LANES, (p2 + 1) * LANES)
        xp = x[:, lanes].astype(BF16)
        yd = []
        for h in (2 * p2, 2 * p2 + 1):
            diff = cs[:, SM_DT + h:SM_DT + h + 1] - cst[SM_DT + h:SM_DT + h + 1, :]
            lm = jnp.exp(jnp.where(tril, diff, NEG))
            yd.append(_dot((cb_mat * lm).astype(BF16), xp))
        rows = slice(p2 * LANES, (p2 + 1) * LANES)
        st = st_ref[rows, :]
        y_off = _nt(cg, st.astype(BF16)) * ecsx[:, lanes]
        ys.append(jnp.where(low, yd[0], yd[1]) + y_off)
        upd = _dot(w[:, lanes].T.astype(BF16), bg)
        st_ref[rows, :] = st * dec[rows, :] + upd

    y = jnp.concatenate(ys, axis=1) + dsk_ref[...] * xs
    zz = z_ref[...]
    gated = y * (zz * _sigmoid(zz))
    ssm = gated * lax.rsqrt(jnp.mean(gated * gated, axis=-1, keepdims=True) + EPS) * gs_ref[...]

    ga = ga_ref[...]
    att = att_ref[...] * (ga * _sigmoid(ga))
    out = _dot(att.astype(BF16), wt_ref[...]) + _dot(ssm.astype(BF16), wb_ref[...])
    y_ref[...] = x_ref[...] + out * lax.rsqrt(jnp.mean(out * out, axis=-1, keepdims=True) + EPS) * gp_ref[...]

    @pl.when(c == pl.num_programs(1) - 1)
    def _():
        hf_ref[...] = st_ref[...]


def _ssd_out(proj, att, x2d, lw, h0, c0, nb, s, t_valid):
    xbc, z, small, ga = proj["xbc"], proj["z"], proj["small"], proj["ga"]
    nc = s // CHUNK
    blk = lambda b, c: (b * nc + c, 0)
    const = lambda b, c: (0, 0)
    per_b = lambda b, c: (b, 0, 0)
    lanes = np.arange(LANES)
    dt_row = lambda v: jnp.zeros((1, LANES), F32).at[0, SM_DT:SM_DT + SSM_HEADS].set(v)
    expand = (lanes[:, None] == SM_DT + np.arange(SSM_WIDTH)[None, :] // SSM_HEAD_DIM).astype(np.float32)
    state_rows = SSM_HEADS * SSM_HEAD_DIM
    return pl.pallas_call(
        functools.partial(_ssd_kernel, t_valid=t_valid),
        grid=(nb, nc),
        in_specs=[pl.BlockSpec((CHUNK, CONV_CH), blk), pl.BlockSpec((CHUNK, SSM_WIDTH), blk),
                  pl.BlockSpec((CHUNK, LANES), blk),
                  pl.BlockSpec((CONV_W, CONV_CH), const), pl.BlockSpec((1, CONV_CH), const),
                  pl.BlockSpec((1, LANES), const), pl.BlockSpec((1, LANES), const),
                  pl.BlockSpec((1, SSM_WIDTH), const), pl.BlockSpec((1, SSM_WIDTH), const),
                  pl.BlockSpec((LANES, SSM_WIDTH), const), pl.BlockSpec((SSM_WIDTH, LANES), const),
                  pl.BlockSpec((None, state_rows, D_STATE), per_b),
                  pl.BlockSpec((None, SUBLANES, CONV_CH), per_b),
                  pl.BlockSpec((CHUNK, ATT_WIDTH), blk), pl.BlockSpec((CHUNK, ATT_WIDTH), blk),
                  pl.BlockSpec((CHUNK, D_MODEL), blk),
                  pl.BlockSpec((ATT_WIDTH, D_MODEL), const), pl.BlockSpec((SSM_WIDTH, D_MODEL), const),
                  pl.BlockSpec((1, D_MODEL), const)],
        out_specs=[pl.BlockSpec((CHUNK, D_MODEL), blk), pl.BlockSpec((None, state_rows, D_STATE), per_b)],
        out_shape=[jax.ShapeDtypeStruct((nb * s, D_MODEL), F32),
                   jax.ShapeDtypeStruct((nb, state_rows, D_STATE), F32)],
        scratch_shapes=[pltpu.VMEM((CHUNK + SUBLANES, CONV_CH), F32), pltpu.VMEM((state_rows, D_STATE), F32)],
        compiler_params=_cparams(("arbitrary", "arbitrary")),
    )(xbc, z, small, lw["conv_w"], lw["conv_b"][None, :], dt_row(lw["dt_bias"]), dt_row(lw["a_log"]),
      jnp.repeat(lw["d_skip"], SSM_HEAD_DIM)[None, :], lw["g_ssm"][None, :], jnp.asarray(expand, BF16),
      jnp.asarray(expand.T), h0, c0, att, ga, x2d, lw["w_top"], lw["w_bot"], lw["g_post"])


SCORE_PAGES = 32
ATTN_PAGES = 16
ROWS_Q = N_HEADS_A * SUBLANES
COUNT_CHAINS = 4
SELECT_GROUP = 4


def _sscore_kernel(pt_ref, qall3_ref, wcol_ref, smt_ref, *rest, past, t_new, n_top, pos_bits, group):
    pages = rest[:SCORE_PAGES]
    madd_ref = rest[SCORE_PAGES]
    sc_ref, last_ref = rest[SCORE_PAGES + 1:]
    j = pl.program_id(1)
    member = pl.program_id(0) % group
    mine = pl.ds(pl.multiple_of(member * SUBLANES, SUBLANES), SUBLANES)
    rows_g = group * SUBLANES
    kw = SCORE_PAGES * PAGE_SIZE
    total = past + LANES
    qall3 = qall3_ref[...]
    wcol = wcol_ref[...]
    zeros = jnp.zeros((D_IDX, PAGE_SIZE), BF16)

    def dots(kt):
        hi, lo = _split(kt)
        return _dot(qall3, jnp.concatenate([hi, hi, lo, zeros], axis=0))

    def weigh(d):
        r = jnp.maximum(d * (D_IDX ** -0.5), 0.0) * wcol
        sc = r[0:SUBLANES]
        for h in range(1, N_IDX_HEADS):
            sc = sc + r[h * SUBLANES:(h + 1) * SUBLANES]
        return sc

    def scores(kt):
        return weigh(dots(kt))

    page_dots = [dots(page[...]) for page in pages]
    for r, d in enumerate(page_dots):
        sl = pl.ds(pl.multiple_of(j * kw + r * PAGE_SIZE, PAGE_SIZE), PAGE_SIZE)
        sc_ref[mine, sl] = weigh(d)

    @pl.when(j == pl.num_programs(1) - 1)
    def _():
        lane8 = lax.broadcasted_iota(I32, (SUBLANES, LANES), 1)
        row8 = lax.broadcasted_iota(I32, (SUBLANES, LANES), 0)
        vis8 = (lane8 <= row8) & (lane8 < t_new)
        sc_ref[mine, past:total] = jnp.where(vis8, scores(smt_ref[0:D_IDX, :]), -jnp.inf)

    @pl.when(jnp.logical_and(j == pl.num_programs(1) - 1, member == group - 1))
    def _():
        shape = (rows_g, LANES)
        lane = lax.broadcasted_iota(I32, shape, 1)
        tok = lax.broadcasted_iota(I32, shape, 0) % SUBLANES
        vis = (lane <= tok) & (lane < t_new)

        def count(pred):
            accs = [jnp.zeros(shape, F32)] * COUNT_CHAINS
            for t in range(total // LANES):
                hit = pred(sc_ref[:, t * LANES:(t + 1) * LANES], t * LANES + lane)
                accs[t % COUNT_CHAINS] = accs[t % COUNT_CHAINS] + jnp.where(hit, 1.0, 0.0)
            while len(accs) > 1:
                accs = [accs[n] + accs[n + 1] for n in range(0, len(accs), 2)]
            return jnp.broadcast_to(jnp.sum(accs[0], axis=1, keepdims=True), shape)

        thr = _search_threshold(lambda t: count(lambda sc, pos: sc >= t), n_top, total, shape)
        need = float(n_top) - count(lambda sc, pos: sc > thr)
        n_eq = count(lambda sc, pos: sc == thr)
        last_ref[...] = jnp.full(shape, 2 ** pos_bits - 1, I32)

        @pl.when(jnp.max(n_eq - need) > 0.0)
        def _():
            last_ref[...] = _search_last_tie(
                lambda q: count(lambda sc, pos: (sc == thr) & (pos < q)), need, pos_bits, shape)

        last = last_ref[...]
        for t in range(total // LANES):
            sl = slice(t * LANES, (t + 1) * LANES)
            pos = t * LANES + lane
            madd = _select_madd(sc_ref[:, sl], pos, thr, last)
            if t * LANES >= past:
                madd = jnp.where(vis, madd, NEG)
            madd_ref[:, sl] = madd


def _sattn_kernel(pt_ref, relb_ref, qbd_ref, madd_ref, maddn_ref, kbnew_ref, vnew_ref, *rest, past):
    kpages = rest[:ATTN_PAGES]
    vpages = rest[ATTN_PAGES:2 * ATTN_PAGES]
    o_ref, m_ref, l_ref, acc_ref = rest[2 * ATTN_PAGES:]
    j = pl.program_id(1)
    kw = ATTN_PAGES * PAGE_SIZE

    @pl.when(j == 0)
    def _():
        m_ref[...] = jnp.full(m_ref.shape, NEG, F32)
        l_ref[...] = jnp.zeros(l_ref.shape, F32)
        acc_ref[...] = jnp.zeros(acc_ref.shape, F32)

    qbd = qbd_ref[...]

    def far_bias(width):
        row_head = lax.broadcasted_iota(I32, (ROWS_Q, width), 0) // SUBLANES
        out = jnp.full((ROWS_Q, width), relb_ref[NUM_BUCKETS - 1, 0], F32)
        for h in range(1, N_HEADS_A):
            out = jnp.where(row_head == h, relb_ref[NUM_BUCKETS - 1, h], out)
        return out

    def near_bias(width, pos0):
        tok = lax.broadcasted_iota(I32, (SUBLANES, width), 0)
        pos = pos0 + lax.broadcasted_iota(I32, (SUBLANES, width), 1)
        bucket = _bucket(past + tok - pos)
        return jnp.concatenate([_bias_lookup(bucket, relb_ref, h) for h in range(N_HEADS_A)], axis=0)

    def update(logits, bias, madd8, pv_fn):
        s = logits + (bias * LOG2E + jnp.concatenate([madd8] * N_HEADS_A, axis=0))
        m_prev = m_ref[...]
        m_new = jnp.maximum(m_prev, jnp.broadcast_to(jnp.max(s, axis=1, keepdims=True), (ROWS_Q, LANES)))
        alpha = jnp.exp2(m_prev - m_new)
        p = jnp.exp2(s - m_new[:, :1])
        l_ref[...] = alpha * l_ref[...] + jnp.broadcast_to(jnp.sum(p, axis=1, keepdims=True), (ROWS_Q, LANES))
        acc_ref[...] = alpha[:, :1] * acc_ref[...] + pv_fn(p.astype(BF16))
        m_ref[...] = m_new

    def paged(bias):
        logits = jnp.concatenate([_dot(qbd, kp[...].astype(BF16)) for kp in kpages], axis=1)

        def pv_fn(p):
            pv = _nt(p[:, 0:PAGE_SIZE], vpages[0][...].astype(BF16))
            for r in range(1, ATTN_PAGES):
                pv = pv + _nt(p[:, r * PAGE_SIZE:(r + 1) * PAGE_SIZE], vpages[r][...].astype(BF16))
            return pv

        update(logits, bias, madd_ref[...], pv_fn)

    far = (j + 1) * kw + MAX_DISTANCE <= past + 1

    @pl.when(far)
    def _():
        paged(far_bias(kw))

    @pl.when(jnp.logical_not(far))
    def _():
        paged(near_bias(kw, j * kw))

    @pl.when(j == pl.num_programs(1) - 1)
    def _():
        update(_nt(qbd, kbnew_ref[...]), near_bias(LANES, past), maddn_ref[...],
               lambda p: _dot(p, vnew_ref[...].astype(BF16)))
        o = acc_ref[...] / l_ref[:, :1]
        own = (lax.broadcasted_iota(I32, (ROWS_Q, ATT_WIDTH), 0) // SUBLANES
               == lax.broadcasted_iota(I32, (ROWS_Q, ATT_WIDTH), 1) // HEAD_DIM)
        o = jnp.where(own, o, 0.0)
        out = o[0:SUBLANES]
        for h in range(1, N_HEADS_A):
            out = out + o[h * SUBLANES:(h + 1) * SUBLANES]
        o_ref[...] = out


def _sample_attention(page_table, rel_bias, proj, cache_k, cache_v, cache_kidx, nb, t_new):
    n_pages = page_table.shape[1]
    past = n_pages * PAGE_SIZE
    total = past + LANES
    n_top = min(TOPK_MAX, (past + t_new) // 4)
    pos_bits = max(1, (total - 1).bit_length())
    pt = page_table.reshape(-1)
    pool = cache_kidx.shape[0]

    def tok(a_t):
        return a_t.reshape(a_t.shape[0], nb, CHUNK)[:, :, :SUBLANES].transpose(1, 2, 0)

    by_head = lambda a, d: a.reshape(nb, SUBLANES, -1, d).transpose(0, 2, 1, 3)
    qall = by_head(tok(proj["qi_t"]), D_IDX).reshape(nb, ROWS_Q, D_IDX)
    hi, lo = _split(qall)
    qall3 = jnp.concatenate([hi, lo, hi, jnp.zeros_like(hi)], axis=-1)
    wi_t = tok(proj["small_t"])[:, :, SM_WI:SM_WI + N_IDX_HEADS].transpose(0, 2, 1).reshape(nb, ROWS_Q, 1)
    wcol = jnp.broadcast_to(wi_t, (nb, ROWS_Q, LANES))
    q_t = by_head(tok(proj["qe_t"] + proj["qo_t"]), HEAD_DIM)
    eye = jnp.eye(N_HEADS_A, dtype=q_t.dtype)
    qbd = (q_t[:, :, :, None, :] * eye[None, :, None, :, None]).reshape(nb, ROWS_Q, ATT_WIDTH)
    ckt = cache_k.transpose(0, 2, 3, 1).reshape(pool, ATT_WIDTH, PAGE_SIZE)
    cvt = cache_v.transpose(0, 2, 3, 1).reshape(pool, ATT_WIDTH, PAGE_SIZE)
    cit = cache_kidx.transpose(0, 2, 1)

    def page(r, per_step):
        return lambda b, j, pt_ref: (pt_ref[b * n_pages + j * per_step + r], 0, 0)

    seq3 = lambda b, j, pt_ref: (b, 0, 0)
    tcol = lambda b, j, pt_ref: (0, b)
    trow = lambda b, j, pt_ref: (b, 0)

    group = math.gcd(nb, SELECT_GROUP)
    madd = pl.pallas_call(
        functools.partial(_sscore_kernel, past=past, t_new=t_new, n_top=n_top, pos_bits=pos_bits, group=group),
        grid_spec=pltpu.PrefetchScalarGridSpec(
            num_scalar_prefetch=1, grid=(nb, n_pages // SCORE_PAGES),
            in_specs=[pl.BlockSpec((None, ROWS_Q, SPLIT3), seq3), pl.BlockSpec((None, ROWS_Q, LANES), seq3),
                      pl.BlockSpec((LANES, CHUNK), tcol)]
                     + [pl.BlockSpec((None, D_IDX, PAGE_SIZE), page(r, SCORE_PAGES)) for r in range(SCORE_PAGES)],
            out_specs=pl.BlockSpec((None, group * SUBLANES, total), lambda b, j, pt_ref: (b // group, 0, 0)),
            scratch_shapes=[pltpu.VMEM((group * SUBLANES, total), F32),
                            pltpu.VMEM((group * SUBLANES, LANES), I32)]),
        out_shape=jax.ShapeDtypeStruct((nb // group, group * SUBLANES, total), F32),
        compiler_params=_cparams(("arbitrary", "arbitrary")),
    )(pt, qall3, wcol, proj["small_t"], *([cit] * SCORE_PAGES)).reshape(nb, SUBLANES, total)

    kw = ATTN_PAGES * PAGE_SIZE
    kv_specs = [pl.BlockSpec((None, ATT_WIDTH, PAGE_SIZE), page(r, ATTN_PAGES)) for r in range(ATTN_PAGES)]
    return pl.pallas_call(
        functools.partial(_sattn_kernel, past=past),
        grid_spec=pltpu.PrefetchScalarGridSpec(
            num_scalar_prefetch=1, grid=(nb, n_pages // ATTN_PAGES),
            in_specs=[pl.BlockSpec(memory_space=pltpu.SMEM),
                      pl.BlockSpec((None, ROWS_Q, ATT_WIDTH), seq3),
                      pl.BlockSpec((None, SUBLANES, kw), lambda b, j, pt_ref: (b, 0, j)),
                      pl.BlockSpec((None, SUBLANES, LANES), lambda b, j, pt_ref: (b, 0, past // LANES)),
                      pl.BlockSpec((CHUNK, ATT_WIDTH), trow), pl.BlockSpec((CHUNK, ATT_WIDTH), trow)]
                     + kv_specs * 2,
            out_specs=pl.BlockSpec((None, SUBLANES, ATT_WIDTH), seq3),
            scratch_shapes=[pltpu.VMEM((ROWS_Q, LANES), F32), pltpu.VMEM((ROWS_Q, LANES), F32),
                            pltpu.VMEM((ROWS_Q, ATT_WIDTH), F32)]),
        out_shape=jax.ShapeDtypeStruct((nb, SUBLANES, ATT_WIDTH), F32),
        compiler_params=_cparams(("arbitrary", "arbitrary")),
    )(pt, rel_bias, qbd, madd, madd, proj["k_b"], proj["v"], *([ckt] * ATTN_PAGES), *([cvt] * ATTN_PAGES))


TM_PROJ = 256
TQ_PROMPT = 256
PROJ_NAMES = ("qe_t", "qo_t", "k3", "k_b", "v", "v3", "vp_t", "ga", "z", "xbc", "qi_t", "small", "small_t")


def _layer_weights(g_pre, w_in, conv_w, conv_b, dt_bias, a_log, d_skip, g_ssm, w_out, g_post):
    offs = np.cumsum([0, ATT_WIDTH, ATT_WIDTH, ATT_WIDTH, ATT_WIDTH, QI_WIDTH, D_IDX, N_IDX_HEADS,
                      SSM_WIDTH, CONV_CH, SSM_HEADS])
    q, k, v, ga, qi, ki, wi, z, xbc, dt = [w_in[:, offs[n]:offs[n + 1]] for n in range(10)]
    pad = jnp.zeros((D_MODEL, LANES - D_IDX - N_IDX_HEADS - SSM_HEADS), F32)
    wqi_hi, wqi_lo = _split(qi.T)
    ws_hi, ws_lo = _split(jnp.concatenate([ki, wi, dt, pad], axis=1))
    return dict(g_pre=g_pre[None, :], w_rows=jnp.concatenate([k, v, ga, z, xbc], axis=1).astype(BF16),
                wq_t=q.T.astype(BF16), wqi_t_hi=wqi_hi, wqi_t_lo=wqi_lo, ws_hi=ws_hi, ws_lo=ws_lo,
                conv_w=conv_w, conv_b=conv_b, dt_bias=dt_bias, a_log=a_log, d_skip=d_skip, g_ssm=g_ssm,
                w_top=w_out[:ATT_WIDTH].astype(BF16), w_bot=w_out[ATT_WIDTH:].astype(BF16), g_post=g_post[None, :])


def _mixer(x, lw, pos_off, t_valid, h0, c0, attn_fn):
    nb, s, _ = x.shape
    tm = min(TM_PROJ, s)
    x2d = x.reshape(nb * s, D_MODEL)
    proj = dict(zip(PROJ_NAMES, _inproj(x2d, lw, _rope_tables(s, pos_off), tm)))
    att = attn_fn(proj)
    y, h_final = _ssd_out(proj, att, x2d, lw, h0, c0, nb, s, t_valid)
    r = lambda a: a.reshape(nb, s, -1)[:, :t_valid]
    heads = lambda a: a.reshape(nb, s, N_HEADS_A, HEAD_DIM)[:, :t_valid]
    conv_state = r(proj["xbc"])[:, t_valid - (CONV_W - 1):]
    return (r(y), heads(proj["k3"]), heads(proj["v3"]), r(proj["small"])[..., :D_IDX],
            h_final.reshape(nb, SSM_HEADS, SSM_HEAD_DIM, D_STATE), conv_state)


def kernel(x_prompt, x_sample, cache_k, cache_v, cache_kidx, state_ssm, state_conv, page_table, g_pre, w_in, conv_w, conv_b, dt_bias, a_log, d_skip, g_ssm, w_out, g_post, rel_bias):
    depth = w_in.shape[0]
    bp, sp, _ = x_prompt.shape
    bs, ts, _ = x_sample.shape
    past = page_table.shape[1] * PAGE_SIZE
    assert ts <= SUBLANES and ts >= CONV_W - 1 and sp % max(TQ_PROMPT, KI3_BUILD_ROWS) == 0
    assert page_table.shape[1] % SCORE_PAGES == 0 and page_table.shape[1] % ATTN_PAGES == 0
    state_rows = SSM_HEADS * SSM_HEAD_DIM
    bias_t = _bias_tiles(rel_bias, TQ_PROMPT)

    yp = x_prompt
    ys = jnp.pad(x_sample, ((0, 0), (0, CHUNK - ts), (0, 0)))
    outs_p, outs_s = [], []
    for l in range(depth):
        lw = _layer_weights(g_pre[l], w_in[l], conv_w[l], conv_b[l], dt_bias[l], a_log[l], d_skip[l], g_ssm[l],
                            w_out[l], g_post[l])

        def prompt_attn(proj):
            return _prompt_attention(proj, bias_t, bp, sp, TQ_PROMPT)

        def sample_attn(proj, layer=l):
            att8 = _sample_attention(page_table, rel_bias, proj, cache_k[layer], cache_v[layer], cache_kidx[layer],
                                     bs, ts)
            return jnp.pad(att8, ((0, 0), (0, CHUNK - SUBLANES), (0, 0))).reshape(bs * CHUNK, ATT_WIDTH)

        op = _mixer(yp, lw, 0, sp, jnp.zeros((bp, state_rows, D_STATE), F32),
                    jnp.zeros((bp, SUBLANES, CONV_CH), F32), prompt_attn)
        c0 = jnp.pad(state_conv[l], ((0, 0), (SUBLANES - (CONV_W - 1), 0), (0, 0)))
        os_ = _mixer(ys, lw, past, ts, state_ssm[l].reshape(bs, state_rows, D_STATE), c0, sample_attn)
        yp = op[0]
        ys = jnp.pad(os_[0], ((0, 0), (0, CHUNK - ts), (0, 0)))
        outs_p.append(op[1:])
        outs_s.append(os_[1:])
    stack = lambda outs, n: jnp.stack([o[n] for o in outs])
    return (yp, ys[:, :ts], *[stack(outs_p, n) for n in range(5)], *[stack(outs_s, n) for n in range(5)])
```

```python
import functools
import math

import jax
import jax.numpy as jnp
import numpy as np
from jax import lax
from jax.experimental import pallas as pl
from jax.experimental.pallas import tpu as pltpu

F32 = jnp.float32
BF16 = jnp.bfloat16
I32 = jnp.int32
HIGHEST = lax.Precision.HIGHEST

D_MODEL = 1024
PAGE_SIZE = 128
HEAD_DIM = 64
ATT_WIDTH = 512
N_HEADS_A = 8
N_IDX_HEADS = 8
D_IDX = 64
IDX_ROPE = 32
ROPE_BASE = 10000.0
TOPK_MAX = 256
NUM_BUCKETS = 32
MAX_DISTANCE = 128
SSM_WIDTH = 512
SSM_HEAD_DIM = 64
SSM_HEADS = 8
SSM_GROUPS = 2
D_STATE = 128
CONV_W = 4
CONV_CH = 1024
CHUNK = 128
EPS = 1e-6

LANES = 128
SUBLANES = 8
VMEM_LIMIT = 56 * 1024 * 1024
NEG = -1e30
INT_MIN = -2 ** 31
LOG2E = 1.4426950408889634

SM_WI = D_IDX
SM_DT = D_IDX + N_IDX_HEADS
ROW_COLS = 3 * ATT_WIDTH + SSM_WIDTH + CONV_CH
QI_WIDTH = N_IDX_HEADS * D_IDX
VP_ROWS = N_HEADS_A * LANES
SPLIT3 = 4 * D_IDX


def _nt(a, b, **kw):
    return lax.dot_general(a, b, (((1,), (1,)), ((), ())), preferred_element_type=F32, **kw)


def _dot(a, b, **kw):
    return jnp.dot(a, b, preferred_element_type=F32, **kw)


def _split(x):
    hi = x.astype(BF16)
    return hi, (x - hi.astype(F32)).astype(BF16)


def _split3(x):
    hi = x.astype(BF16)
    rest = x - hi.astype(F32)
    mid = rest.astype(BF16)
    return hi, mid, (rest - mid.astype(F32)).astype(BF16)


def _sigmoid(x):
    return 1.0 / (1.0 + jnp.exp(-x))


def _cparams(sem):
    return pltpu.CompilerParams(dimension_semantics=sem, vmem_limit_bytes=VMEM_LIMIT)


def _rope_table_kernel(inv_ref, cos_ref, sin_ref, cost_ref, sint_ref, *, pos_off):
    rows = cos_ref.shape[0]
    pos = (lax.broadcasted_iota(I32, (rows, LANES), 0) + pos_off).astype(F32)
    ang = pos * inv_ref[...]
    c = jnp.cos(ang)
    s = jnp.sin(ang)
    cos_ref[...] = c
    sin_ref[...] = s
    cost_ref[...] = c.T
    sint_ref[...] = s.T


def _rope_tables(rows, pos_off):
    inv = ROPE_BASE ** (-jnp.arange(0, IDX_ROPE, 2, dtype=F32) / IDX_ROPE)
    l64 = np.arange(LANES) % D_IDX
    inv_row = jnp.where(l64 < IDX_ROPE, inv[l64 % (IDX_ROPE // 2)], 0.0).astype(F32)[None, :]
    return pl.pallas_call(
        functools.partial(_rope_table_kernel, pos_off=pos_off),
        out_shape=(jax.ShapeDtypeStruct((rows, LANES), F32),) * 2 + (jax.ShapeDtypeStruct((LANES, rows), F32),) * 2,
    )(inv_row)


def _inproj_kernel(x_ref, g_ref, wr_ref, wqt_ref, wqih_ref, wqil_ref, wsh_ref, wsl_ref,
                   cos_ref, sin_ref, cost_ref, sint_ref,
                   qet_ref, qot_ref, k3_ref, kb_ref, v_ref, v3_ref, vpt_ref, ga_ref, z_ref, xbc_ref, qit_ref, sm_ref,
                   smt_ref):
    x = x_ref[...]
    hn = x * lax.rsqrt(jnp.mean(x * x, axis=-1, keepdims=True) + EPS) * g_ref[...]
    hb, hlo = _split(hn)
    tm = x.shape[0]

    def rows(lo, width):
        return _dot(hb, wr_ref[:, lo:lo + width])

    k = rows(0, ATT_WIDTH)
    kb_ref[...] = k.astype(BF16)
    v = rows(ATT_WIDTH, ATT_WIDTH)
    v_ref[...] = v
    for h in range(N_HEADS_A):
        k3_ref[:, h, :] = k[:, h * HEAD_DIM:(h + 1) * HEAD_DIM]
        v3_ref[:, h, :] = v[:, h * HEAD_DIM:(h + 1) * HEAD_DIM]
    ga_ref[...] = rows(2 * ATT_WIDTH, ATT_WIDTH)
    z_ref[...] = rows(3 * ATT_WIDTH, SSM_WIDTH)
    xbc_ref[...] = rows(3 * ATT_WIDTH + SSM_WIDTH, CONV_CH)

    vt = v.T
    ones = jnp.ones((HEAD_DIM, tm), BF16)
    for h in range(N_HEADS_A):
        vpt_ref[h * LANES:h * LANES + HEAD_DIM, :] = vt[h * HEAD_DIM:(h + 1) * HEAD_DIM, :].astype(BF16)
        vpt_ref[h * LANES + HEAD_DIM:(h + 1) * LANES, :] = ones

    qt = _nt(wqt_ref[...], hb) * (HEAD_DIM ** -0.5 * LOG2E)
    even = (lax.broadcasted_iota(I32, qt.shape, 0) & HEAD_DIM) == 0
    qet_ref[...] = jnp.where(even, qt, 0.0).astype(BF16)
    qot_ref[...] = jnp.where(even, 0.0, qt).astype(BF16)

    qit = _nt(wqih_ref[...], hb) + (_nt(wqih_ref[...], hlo) + _nt(wqil_ref[...], hb))
    ct = cost_ref[0:D_IDX, :]
    st = sint_ref[0:D_IDX, :]
    first_t = lax.broadcasted_iota(I32, (D_IDX, tm), 0) < IDX_ROPE // 2
    s1t = jnp.where(first_t, -st, 0.0)
    s2t = jnp.where(first_t, 0.0, st)
    for h in range(N_IDX_HEADS):
        xh = qit[h * D_IDX:(h + 1) * D_IDX, :]
        qit_ref[h * D_IDX:(h + 1) * D_IDX, :] = (xh * ct + pltpu.roll(xh, D_IDX - IDX_ROPE // 2, 0) * s1t
                                                 + pltpu.roll(xh, IDX_ROPE // 2, 0) * s2t)

    sm = _dot(hb, wsh_ref[...]) + (_dot(hlo, wsh_ref[...]) + _dot(hb, wsl_ref[...]))
    lane = lax.broadcasted_iota(I32, (tm, LANES), 1)
    is_ki = lane < D_IDX
    first = (lane & (D_IDX - 1)) < IDX_ROPE // 2
    c = jnp.where(is_ki, cos_ref[...], 1.0)
    s = jnp.where(is_ki, sin_ref[...], 0.0)
    sm = (sm * c + pltpu.roll(sm, LANES - IDX_ROPE // 2, 1) * jnp.where(first, -s, 0.0)
          + pltpu.roll(sm, IDX_ROPE // 2, 1) * jnp.where(first, 0.0, s))
    is_wi = (lane >= SM_WI) & (lane < SM_DT)
    sm = jnp.where(is_wi, sm * (N_IDX_HEADS ** -0.5), sm)
    sm_ref[...] = sm
    smt_ref[...] = sm.T


def _inproj(x2d, lw, tables, tm):
    n = x2d.shape[0]
    cos_t, sin_t, cos_tt, sin_tt = tables
    tab_blocks = cos_t.shape[0] // tm
    row = lambda i: (i, 0)
    col = lambda i: (0, i)
    const = lambda i: (0, 0)
    rows = lambda w, dt: (jax.ShapeDtypeStruct((n, w), dt), pl.BlockSpec((tm, w), row))
    cols = lambda w, dt: (jax.ShapeDtypeStruct((w, n), dt), pl.BlockSpec((w, tm), col))
    full = lambda a: pl.BlockSpec(a.shape, const)
    heads = (jax.ShapeDtypeStruct((n, N_HEADS_A, HEAD_DIM), F32),
             pl.BlockSpec((tm, N_HEADS_A, HEAD_DIM), lambda i: (i, 0, 0)))
    outs = [cols(ATT_WIDTH, BF16), cols(ATT_WIDTH, BF16), heads, rows(ATT_WIDTH, BF16),
            rows(ATT_WIDTH, F32), heads, cols(VP_ROWS, BF16), rows(ATT_WIDTH, F32), rows(SSM_WIDTH, F32),
            rows(CONV_CH, F32), cols(QI_WIDTH, F32), rows(LANES, F32), cols(LANES, F32)]
    weights = [lw["g_pre"], lw["w_rows"], lw["wq_t"], lw["wqi_t_hi"], lw["wqi_t_lo"], lw["ws_hi"], lw["ws_lo"]]
    return pl.pallas_call(
        _inproj_kernel,
        grid=(n // tm,),
        in_specs=[pl.BlockSpec((tm, D_MODEL), row)] + [full(w) for w in weights]
                 + [pl.BlockSpec((tm, LANES), lambda i: (i % tab_blocks, 0))] * 2
                 + [pl.BlockSpec((LANES, tm), lambda i: (0, i % tab_blocks))] * 2,
        out_specs=[o[1] for o in outs],
        out_shape=[o[0] for o in outs],
        compiler_params=_cparams(("arbitrary",)),
    )(x2d, *weights, cos_t, sin_t, cos_tt, sin_tt)


def _bucket(dist):
    max_exact = NUM_BUCKETS // 2
    n = jnp.maximum(dist, 0)
    nf = jnp.maximum(n, max_exact).astype(F32)
    large = max_exact + jnp.floor(jnp.log(nf / max_exact) / math.log(MAX_DISTANCE / max_exact)
                                  * (NUM_BUCKETS - max_exact)).astype(I32)
    large = jnp.minimum(large, NUM_BUCKETS - 1)
    return jnp.where(n < max_exact, n, large)


def _bias_lookup(bucket, relb_ref, h):
    out = jnp.full(bucket.shape, relb_ref[0, h], F32)
    for b in range(1, NUM_BUCKETS):
        out = jnp.where(bucket == b, relb_ref[b, h], out)
    return out


KEY_NEG_INF = INT_MIN + 0x7FFFFF


def _bit(n):
    return lax.shift_left(jnp.int32(1), jnp.asarray(n, I32))


def _key_to_float(key):
    return pltpu.bitcast(jnp.where(key < 0, key ^ 0x7FFFFFFF, key), F32)


def _search_threshold(count_ge, n_top, n_keys, shape):
    def body(it, thr):
        cand = thr ^ _bit(31 - it)
        cnt = jnp.where(cand < KEY_NEG_INF, jnp.asarray(n_keys, F32), count_ge(_key_to_float(cand)))
        return jnp.where(cnt >= float(n_top), cand, thr)

    return _key_to_float(lax.fori_loop(0, 32, body, jnp.full(shape, INT_MIN, I32)))


def _search_last_tie(count_ties_before, need, pos_bits, shape):
    def body(it, q):
        cand = q | _bit(pos_bits - 1 - it)
        return jnp.where(count_ties_before(cand) < need, cand, q)

    return lax.fori_loop(0, pos_bits, body, jnp.zeros(shape, I32))


def _select_madd(score, pos, thr, last):
    return jnp.where(score > thr, 0.0, jnp.where(score == thr, jnp.where(pos <= last, 0.0, NEG), NEG))


def _bias_tiles_kernel(relb_ref, o_ref, *, tq):
    ki = lax.broadcasted_iota(I32, (tq, tq), 0)
    qi = lax.broadcasted_iota(I32, (tq, tq), 1)
    for kind in range(2):
        bucket = _bucket(qi - ki + kind * tq)
        for h in range(N_HEADS_A):
            o_ref[h, kind] = (_bias_lookup(bucket, relb_ref, h) - relb_ref[NUM_BUCKETS - 1, h]) * LOG2E


def _bias_tiles(rel_bias, tq):
    return pl.pallas_call(
        functools.partial(_bias_tiles_kernel, tq=tq),
        in_specs=[pl.BlockSpec(memory_space=pltpu.SMEM)],
        out_shape=jax.ShapeDtypeStruct((N_HEADS_A, 2, tq, tq), F32),
        compiler_params=pltpu.CompilerParams(vmem_limit_bytes=VMEM_LIMIT),
    )(rel_bias)


KI3_BUILD_ROWS = 512


def _pattn_kernel(qit_ref, smtq_ref, sm_ref, qet_ref, qot_ref, kb_ref, vpt_ref, bt_ref, o_ref,
                  ki3_ref, qh3_ref, sc_ref, last_ref, m_ref, acc_ref, *, tq, n_top, pos_bits):
    i = pl.program_id(1)
    nch = i + 1
    s_len = sc_ref.shape[0]
    groups = tq // SUBLANES
    kiota = lax.broadcasted_iota(I32, (tq, tq), 0)
    qpos = i * tq + lax.broadcasted_iota(I32, (tq, tq), 1)

    def rows(c, width=tq):
        return pl.ds(pl.multiple_of(c * tq, tq), width)

    @pl.when(i == 0)
    def _():
        low = lax.broadcasted_iota(I32, (KI3_BUILD_ROWS, LANES), 1) < D_IDX

        def body(r, carry):
            sl = pl.ds(pl.multiple_of(r * KI3_BUILD_ROWS, KI3_BUILD_ROWS), KI3_BUILD_ROWS)
            x = sm_ref[sl, :]
            hi = x.astype(BF16).astype(F32)
            ki3_ref[sl, 0:LANES] = jnp.where(low, hi, pltpu.roll(hi, D_IDX, 1)).astype(BF16)
            ki3_ref[sl, LANES:2 * LANES] = jnp.where(low, x - hi, 0.0).astype(BF16)
            return carry
        lax.fori_loop(0, s_len // KI3_BUILD_ROWS, body, 0)

    for h in range(N_IDX_HEADS):
        hi, lo = _split(qit_ref[h * D_IDX:(h + 1) * D_IDX, :])
        qh3_ref[h, 0:D_IDX, :] = hi
        qh3_ref[h, D_IDX:2 * D_IDX, :] = lo
        qh3_ref[h, 2 * D_IDX:3 * D_IDX, :] = hi
        qh3_ref[h, 3 * D_IDX:, :] = jnp.zeros((D_IDX, tq), BF16)
    w8 = smtq_ref[SM_WI:SM_WI + N_IDX_HEADS, :] * (D_IDX ** -0.5)

    def score_body(c2, carry):
        blocks = [2 * c2, jnp.minimum(2 * c2 + 1, i)]
        dots = [[_dot(ki3_ref[rows(c), :], qh3_ref[h]) for h in range(N_IDX_HEADS)] for c in blocks]
        for c, d in zip(blocks, dots):
            terms = [jnp.maximum(d[h], 0.0) * w8[h:h + 1, :] for h in range(N_IDX_HEADS)]
            while len(terms) > 1:
                terms = [terms[j] + terms[j + 1] for j in range(0, len(terms), 2)]
            sc_ref[rows(c), :] = jnp.where(c * tq + kiota <= qpos, terms[0], -jnp.inf)
        return carry

    lax.fori_loop(0, (nch + 1) // 2, score_body, 0)

    def over_keys(x, op):
        x = x.reshape(x.shape[0] // SUBLANES, SUBLANES, tq)
        while x.shape[0] > 1:
            half = x.shape[0] // 2
            x = op(x[:half], x[half:])
        return x[0]

    def count(pred):
        def body(c, acc):
            hit = jnp.where(pred(sc_ref[rows(c), :], c * tq + kiota), 1.0, 0.0)
            return acc + over_keys(hit, jnp.add)
        acc = lax.fori_loop(0, nch, body, jnp.zeros((SUBLANES, tq), F32))
        return jnp.broadcast_to(jnp.sum(acc, axis=0, keepdims=True), (SUBLANES, tq))

    thr = _search_threshold(lambda t: count(lambda sc, pos: sc >= t[0:1, :]), n_top, nch * tq, (SUBLANES, tq))
    thr_row = thr[0:1, :]
    need = float(n_top) - count(lambda sc, pos: sc > thr_row)
    n_eq = count(lambda sc, pos: sc == thr_row)
    last_ref[...] = jnp.full((SUBLANES, tq), 2 ** pos_bits - 1, I32)

    @pl.when(jnp.max(n_eq - need) > 0.0)
    def _():
        last_ref[...] = _search_last_tie(
            lambda q: count(lambda sc, pos: (sc == thr_row) & (pos < q[0:1, :])), need, pos_bits, (SUBLANES, tq))

    last_row = last_ref[0:1, :]

    def madd_body(c, carry):
        pos = c * tq + kiota
        madd = _select_madd(sc_ref[rows(c), :], pos, thr_row, last_row)
        sc_ref[rows(c), :] = jnp.where(pos <= qpos, madd, NEG)
        return carry

    lax.fori_loop(0, nch, madd_body, 0)

    m_ref[...] = jnp.full(m_ref.shape, NEG, F32)
    acc_ref[...] = jnp.zeros(acc_ref.shape, F32)

    def attend(c0, width, bias_of_head):
        sl = rows(c0, width)
        madd = sc_ref[sl, :]
        logits = []
        for h in range(N_HEADS_A):
            p2 = h // 2
            qt = (qet_ref if h % 2 == 0 else qot_ref)[p2 * LANES:(p2 + 1) * LANES, :]
            logits.append(_dot(kb_ref[sl, p2 * LANES:(p2 + 1) * LANES], qt))
        probs, alphas = [], []
        for h in range(N_HEADS_A):
            s = logits[h] + madd
            if bias_of_head is not None:
                s = s + bias_of_head(h)
            m_prev = m_ref[h]
            cmax = over_keys(s, jnp.maximum)
            m_new = jnp.maximum(m_prev, jnp.broadcast_to(jnp.max(cmax, axis=0, keepdims=True), (SUBLANES, tq)))
            probs.append(jnp.exp2(s - m_new[0:1, :]).astype(BF16))
            alphas.append(jnp.exp2(m_prev - m_new)[0:1, :])
            m_ref[h] = m_new
        for h in range(N_HEADS_A):
            pv = _dot(vpt_ref[h * LANES:(h + 1) * LANES, sl], probs[h])
            acc_ref[h] = alphas[h] * acc_ref[h] + pv

    n_far = jnp.maximum(i - 1, 0)

    def far_body(c, carry):
        attend(2 * c, 2 * tq, None)
        return carry

    lax.fori_loop(0, n_far // 2, far_body, 0)

    @pl.when(n_far % 2 == 1)
    def _():
        attend(n_far - 1, tq, None)

    @pl.when(i >= 1)
    def _():
        attend(i - 1, tq, lambda h: bt_ref[h, 1])

    attend(i, tq, lambda h: bt_ref[h, 0])

    outs = []
    for h in range(N_HEADS_A):
        acc = acc_ref[h]
        outs.append(acc[0:HEAD_DIM, :] / acc[HEAD_DIM:, :])
    o_ref[...] = jnp.concatenate(outs, axis=0).T


def _prompt_attention(proj, bias_t, nb, s, tq):
    nq = s // tq
    n_top = min(TOPK_MAX, s // 4)
    pos_bits = max(1, (s - 1).bit_length())
    qcols = lambda b, i: (0, b * nq + i)
    seq_rows = lambda b, i: (b, 0)
    seq_cols = lambda b, i: (0, b)
    return pl.pallas_call(
        functools.partial(_pattn_kernel, tq=tq, n_top=n_top, pos_bits=pos_bits),
        grid=(nb, nq),
        in_specs=[pl.BlockSpec((QI_WIDTH, tq), qcols), pl.BlockSpec((LANES, tq), qcols),
                  pl.BlockSpec((s, LANES), seq_rows),
                  pl.BlockSpec((ATT_WIDTH, tq), qcols), pl.BlockSpec((ATT_WIDTH, tq), qcols),
                  pl.BlockSpec((s, ATT_WIDTH), seq_rows), pl.BlockSpec((VP_ROWS, s), seq_cols),
                  pl.BlockSpec((N_HEADS_A, 2, tq, tq), lambda b, i: (0, 0, 0, 0),
                               pipeline_mode=pl.Buffered(1))],
        out_specs=pl.BlockSpec((tq, ATT_WIDTH), lambda b, i: (b * nq + i, 0)),
        out_shape=jax.ShapeDtypeStruct((nb * s, ATT_WIDTH), F32),
        scratch_shapes=[pltpu.VMEM((s, SPLIT3), BF16), pltpu.VMEM((N_IDX_HEADS, SPLIT3, tq), BF16),
                        pltpu.VMEM((s, tq), F32), pltpu.VMEM((SUBLANES, tq), I32),
                        pltpu.VMEM((N_HEADS_A, SUBLANES, tq), F32), pltpu.VMEM((N_HEADS_A, LANES, tq), F32)],
        compiler_params=_cparams(("arbitrary", "arbitrary")),
    )(proj["qi_t"], proj["small_t"], proj["small"], proj["qe_t"], proj["qo_t"], proj["k_b"], proj["vp_t"], bias_t)


def _ssd_kernel(xbc_ref, z_ref, sm_ref, cw_ref, cb_ref, dtb_ref, alog_ref, dsk_ref, gs_ref, ex_ref, ext_ref,
                h0_ref, c0_ref, att_ref, ga_ref, x_ref, wt_ref, wb_ref, gp_ref, y_ref, hf_ref, xp_ref, st_ref, *,
                t_valid):
    c = pl.program_id(1)
    L = CHUNK

    @pl.when(c == 0)
    def _():
        st_ref[...] = h0_ref[...]
        xp_ref[0:SUBLANES, :] = c0_ref[...]

    xp_ref[SUBLANES:SUBLANES + L, :] = xbc_ref[...]
    conv = cb_ref[...]
    for j in range(CONV_W):
        lo = SUBLANES - (CONV_W - 1) + j
        conv = conv + xp_ref[lo:lo + L, :] * cw_ref[j:j + 1, :]
    xp_ref[0:SUBLANES, :] = xp_ref[L:L + SUBLANES, :]
    act = conv * _sigmoid(conv)
    xs = act[:, :SSM_WIDTH]
    bm = act[:, SSM_WIDTH:SSM_WIDTH + SSM_GROUPS * D_STATE]
    cm = act[:, SSM_WIDTH + SSM_GROUPS * D_STATE:]

    raw = sm_ref[...] + dtb_ref[...]
    dtf = jnp.maximum(raw, 0.0) + jnp.log1p(jnp.exp(-jnp.abs(raw)))
    row = lax.broadcasted_iota(I32, (L, LANES), 0)
    if t_valid < L:
        dtf = jnp.where(row < t_valid, dtf, 0.0)
    adt = dtf * (-jnp.exp(alog_ref[...]))
    tril = row >= lax.broadcasted_iota(I32, (L, LANES), 1)
    tril01 = jnp.where(tril, 1.0, 0.0).astype(BF16)
    cs = sum(_dot(tril01, part) for part in _split3(adt))
    ex = ex_ref[...]
    dtx = sum(_dot(part, ex) for part in _split3(dtf))
    csx = sum(_dot(part, ex) for part in _split3(cs))
    cst = cs.T
    x = xs * dtx
    w = x * jnp.exp(csx[L - 1:L, :] - csx)
    ecsx = jnp.exp(csx)
    dec = jnp.exp(jnp.sum(ext_ref[...] * cs[L - 1:L, :], axis=1, keepdims=True))
    low = lax.broadcasted_iota(I32, (L, LANES), 1) < SSM_HEAD_DIM

    ys = []
    for p2 in range(SSM_HEADS // 2):
        g = (2 * p2) // (SSM_HEADS // SSM_GROUPS)
        cg = cm[:, g * D_STATE:(g + 1) * D_STATE].astype(BF16)
        bg = bm[:, g * D_STATE:(g + 1) * D_STATE].astype(BF16)
        cb_mat = _nt(cg, bg)
        lanes = slice(p2 * LANES, (p2 + 1) * LANES)
        xp = x[:, lanes].astype(BF16)
        yd = []
        for h in (2 * p2, 2 * p2 + 1):
            diff = cs[:, SM_DT + h:SM_DT + h + 1] - cst[SM_DT + h:SM_DT + h + 1, :]
            lm = jnp.exp(jnp.where(tril, diff, NEG))
            yd.append(_dot((cb_mat * lm).astype(BF16), xp))
        rows = slice(p2 * LANES, (p2 + 1) * LANES)
        st = st_ref[rows, :]
        y_off = _nt(cg, st.astype(BF16)) * ecsx[:, lanes]
        ys.append(jnp.where(low, yd[0], yd[1]) + y_off)
        upd = _dot(w[:, lanes].T.astype(BF16), bg)
        st_ref[rows, :] = st * dec[rows, :] + upd

    y = jnp.concatenate(ys, axis=1) + dsk_ref[...] * xs
    zz = z_ref[...]
    gated = y * (zz * _sigmoid(zz))
    ssm = gated * lax.rsqrt(jnp.mean(gated * gated, axis=-1, keepdims=True) + EPS) * gs_ref[...]

    ga = ga_ref[...]
    att = att_ref[...] * (ga * _sigmoid(ga))
    out = _dot(att.astype(BF16), wt_ref[...]) + _dot(ssm.astype(BF16), wb_ref[...])
    y_ref[...] = x_ref[...] + out * lax.rsqrt(jnp.mean(out * out, axis=-1, keepdims=True) + EPS) * gp_ref[...]

    @pl.when(c == pl.num_programs(1) - 1)
    def _():
        hf_ref[...] = st_ref[...]


def _ssd_out(proj, att, x2d, lw, h0, c0, nb, s, t_valid):
    xbc, z, small, ga = proj["xbc"], proj["z"], proj["small"], proj["ga"]
    nc = s // CHUNK
    blk = lambda b, c: (b * nc + c, 0)
    const = lambda b, c: (0, 0)
    per_b = lambda b, c: (b, 0, 0)
    lanes = np.arange(LANES)
    dt_row = lambda v: jnp.zeros((1, LANES), F32).at[0, SM_DT:SM_DT + SSM_HEADS].set(v)
    expand = (lanes[:, None] == SM_DT + np.arange(SSM_WIDTH)[None, :] // SSM_HEAD_DIM).astype(np.float32)
    state_rows = SSM_HEADS * SSM_HEAD_DIM
    return pl.pallas_call(
        functools.partial(_ssd_kernel, t_valid=t_valid),
        grid=(nb, nc),
        in_specs=[pl.BlockSpec((CHUNK, CONV_CH), blk), pl.BlockSpec((CHUNK, SSM_WIDTH), blk),
                  pl.BlockSpec((CHUNK, LANES), blk),
                  pl.BlockSpec((CONV_W, CONV_CH), const), pl.BlockSpec((1, CONV_CH), const),
                  pl.BlockSpec((1, LANES), const), pl.BlockSpec((1, LANES), const),
                  pl.BlockSpec((1, SSM_WIDTH), const), pl.BlockSpec((1, SSM_WIDTH), const),
                  pl.BlockSpec((LANES, SSM_WIDTH), const), pl.BlockSpec((SSM_WIDTH, LANES), const),
                  pl.BlockSpec((None, state_rows, D_STATE), per_b),
                  pl.BlockSpec((None, SUBLANES, CONV_CH), per_b),
                  pl.BlockSpec((CHUNK, ATT_WIDTH), blk), pl.BlockSpec((CHUNK, ATT_WIDTH), blk),
                  pl.BlockSpec((CHUNK, D_MODEL), blk),
                  pl.BlockSpec((ATT_WIDTH, D_MODEL), const), pl.BlockSpec((SSM_WIDTH, D_MODEL), const),
                  pl.BlockSpec((1, D_MODEL), const)],
        out_specs=[pl.BlockSpec((CHUNK, D_MODEL), blk), pl.BlockSpec((None, state_rows, D_STATE), per_b)],
        out_shape=[jax.ShapeDtypeStruct((nb * s, D_MODEL), F32),
                   jax.ShapeDtypeStruct((nb, state_rows, D_STATE), F32)],
        scratch_shapes=[pltpu.VMEM((CHUNK + SUBLANES, CONV_CH), F32), pltpu.VMEM((state_rows, D_STATE), F32)],
        compiler_params=_cparams(("arbitrary", "arbitrary")),
    )(xbc, z, small, lw["conv_w"], lw["conv_b"][None, :], dt_row(lw["dt_bias"]), dt_row(lw["a_log"]),
      jnp.repeat(lw["d_skip"], SSM_HEAD_DIM)[None, :], lw["g_ssm"][None, :], jnp.asarray(expand, BF16),
      jnp.asarray(expand.T), h0, c0, att, ga, x2d, lw["w_top"], lw["w_bot"], lw["g_post"])


SCORE_PAGES = 32
ATTN_PAGES = 16
ROWS_Q = N_HEADS_A * SUBLANES
COUNT_CHAINS = 4
SELECT_GROUP = 4


def _sscore_kernel(pt_ref, qall3_ref, wcol_ref, smt_ref, *rest, past, t_new, n_top, pos_bits, group):
    pages = rest[:SCORE_PAGES]
    madd_ref = rest[SCORE_PAGES]
    sc_ref, last_ref = rest[SCORE_PAGES + 1:]
    j = pl.program_id(1)
    member = pl.program_id(0) % group
    mine = pl.ds(pl.multiple_of(member * SUBLANES, SUBLANES), SUBLANES)
    rows_g = group * SUBLANES
    kw = SCORE_PAGES * PAGE_SIZE
    total = past + LANES
    qall3 = qall3_ref[...]
    wcol = wcol_ref[...]
    zeros = jnp.zeros((D_IDX, PAGE_SIZE), BF16)

    def dots(kt):
        hi, lo = _split(kt)
        return _dot(qall3, jnp.concatenate([hi, hi, lo, zeros], axis=0))

    def weigh(d):
        r = jnp.maximum(d * (D_IDX ** -0.5), 0.0) * wcol
        sc = r[0:SUBLANES]
        for h in range(1, N_IDX_HEADS):
            sc = sc + r[h * SUBLANES:(h + 1) * SUBLANES]
        return sc

    def scores(kt):
        return weigh(dots(kt))

    page_dots = [dots(page[...]) for page in pages]
    for r, d in enumerate(page_dots):
        sl = pl.ds(pl.multiple_of(j * kw + r * PAGE_SIZE, PAGE_SIZE), PAGE_SIZE)
        sc_ref[mine, sl] = weigh(d)

    @pl.when(j == pl.num_programs(1) - 1)
    def _():
        lane8 = lax.broadcasted_iota(I32, (SUBLANES, LANES), 1)
        row8 = lax.broadcasted_iota(I32, (SUBLANES, LANES), 0)
        vis8 = (lane8 <= row8) & (lane8 < t_new)
        sc_ref[mine, past:total] = jnp.where(vis8, scores(smt_ref[0:D_IDX, :]), -jnp.inf)

    @pl.when(jnp.logical_and(j == pl.num_programs(1) - 1, member == group - 1))
    def _():
        shape = (rows_g, LANES)
        lane = lax.broadcasted_iota(I32, shape, 1)
        tok = lax.broadcasted_iota(I32, shape, 0) % SUBLANES
        vis = (lane <= tok) & (lane < t_new)

        def count(pred):
            accs = [jnp.zeros(shape, F32)] * COUNT_CHAINS
            for t in range(total // LANES):
                hit = pred(sc_ref[:, t * LANES:(t + 1) * LANES], t * LANES + lane)
                accs[t % COUNT_CHAINS] = accs[t % COUNT_CHAINS] + jnp.where(hit, 1.0, 0.0)
            while len(accs) > 1:
                accs = [accs[n] + accs[n + 1] for n in range(0, len(accs), 2)]
            return jnp.broadcast_to(jnp.sum(accs[0], axis=1, keepdims=True), shape)

        thr = _search_threshold(lambda t: count(lambda sc, pos: sc >= t), n_top, total, shape)
        need = float(n_top) - count(lambda sc, pos: sc > thr)
        n_eq = count(lambda sc, pos: sc == thr)
        last_ref[...] = jnp.full(shape, 2 ** pos_bits - 1, I32)

        @pl.when(jnp.max(n_eq - need) > 0.0)
        def _():
            last_ref[...] = _search_last_tie(
                lambda q: count(lambda sc, pos: (sc == thr) & (pos < q)), need, pos_bits, shape)

        last = last_ref[...]
        for t in range(total // LANES):
            sl = slice(t * LANES, (t + 1) * LANES)
            pos = t * LANES + lane
            madd = _select_madd(sc_ref[:, sl], pos, thr, last)
            if t * LANES >= past:
                madd = jnp.where(vis, madd, NEG)
            madd_ref[:, sl] = madd


def _sattn_kernel(pt_ref, relb_ref, qbd_ref, madd_ref, maddn_ref, kbnew_ref, vnew_ref, *rest, past):
    kpages = rest[:ATTN_PAGES]
    vpages = rest[ATTN_PAGES:2 * ATTN_PAGES]
    o_ref, m_ref, l_ref, acc_ref = rest[2 * ATTN_PAGES:]
    j = pl.program_id(1)
    kw = ATTN_PAGES * PAGE_SIZE

    @pl.when(j == 0)
    def _():
        m_ref[...] = jnp.full(m_ref.shape, NEG, F32)
        l_ref[...] = jnp.zeros(l_ref.shape, F32)
        acc_ref[...] = jnp.zeros(acc_ref.shape, F32)

    qbd = qbd_ref[...]

    def far_bias(width):
        row_head = lax.broadcasted_iota(I32, (ROWS_Q, width), 0) // SUBLANES
        out = jnp.full((ROWS_Q, width), relb_ref[NUM_BUCKETS - 1, 0], F32)
        for h in range(1, N_HEADS_A):
            out = jnp.where(row_head == h, relb_ref[NUM_BUCKETS - 1, h], out)
        return out

    def near_bias(width, pos0):
        tok = lax.broadcasted_iota(I32, (SUBLANES, width), 0)
        pos = pos0 + lax.broadcasted_iota(I32, (SUBLANES, width), 1)
        bucket = _bucket(past + tok - pos)
        return jnp.concatenate([_bias_lookup(bucket, relb_ref, h) for h in range(N_HEADS_A)], axis=0)

    def update(logits, bias, madd8, pv_fn):
        s = logits + (bias * LOG2E + jnp.concatenate([madd8] * N_HEADS_A, axis=0))
        m_prev = m_ref[...]
        m_new = jnp.maximum(m_prev, jnp.broadcast_to(jnp.max(s, axis=1, keepdims=True), (ROWS_Q, LANES)))
        alpha = jnp.exp2(m_prev - m_new)
        p = jnp.exp2(s - m_new[:, :1])
        l_ref[...] = alpha * l_ref[...] + jnp.broadcast_to(jnp.sum(p, axis=1, keepdims=True), (ROWS_Q, LANES))
        acc_ref[...] = alpha[:, :1] * acc_ref[...] + pv_fn(p.astype(BF16))
        m_ref[...] = m_new

    def paged(bias):
        logits = jnp.concatenate([_dot(qbd, kp[...].astype(BF16)) for kp in kpages], axis=1)

        def pv_fn(p):
            pv = _nt(p[:, 0:PAGE_SIZE], vpages[0][...].astype(BF16))
            for r in range(1, ATTN_PAGES):
                pv = pv + _nt(p[:, r * PAGE_SIZE:(r + 1) * PAGE_SIZE], vpages[r][...].astype(BF16))
            return pv

        update(logits, bias, madd_ref[...], pv_fn)

    far = (j + 1) * kw + MAX_DISTANCE <= past + 1

    @pl.when(far)
    def _():
        paged(far_bias(kw))

    @pl.when(jnp.logical_not(far))
    def _():
        paged(near_bias(kw, j * kw))

    @pl.when(j == pl.num_programs(1) - 1)
    def _():
        update(_nt(qbd, kbnew_ref[...]), near_bias(LANES, past), maddn_ref[...],
               lambda p: _dot(p, vnew_ref[...].astype(BF16)))
        o = acc_ref[...] / l_ref[:, :1]
        own = (lax.broadcasted_iota(I32, (ROWS_Q, ATT_WIDTH), 0) // SUBLANES
               == lax.broadcasted_iota(I32, (ROWS_Q, ATT_WIDTH), 1) // HEAD_DIM)
        o = jnp.where(own, o, 0.0)
        out = o[0:SUBLANES]
        for h in range(1, N_HEADS_A):
            out = out + o[h * SUBLANES:(h + 1) * SUBLANES]
        o_ref[...] = out


def _sample_attention(page_table, rel_bias, proj, cache_k, cache_v, cache_kidx, nb, t_new):
    n_pages = page_table.shape[1]
    past = n_pages * PAGE_SIZE
    total = past + LANES
    n_top = min(TOPK_MAX, (past + t_new) // 4)
    pos_bits = max(1, (total - 1).bit_length())
    pt = page_table.reshape(-1)
    pool = cache_kidx.shape[0]

    def tok(a_t):
        return a_t.reshape(a_t.shape[0], nb, CHUNK)[:, :, :SUBLANES].transpose(1, 2, 0)

    by_head = lambda a, d: a.reshape(nb, SUBLANES, -1, d).transpose(0, 2, 1, 3)
    qall = by_head(tok(proj["qi_t"]), D_IDX).reshape(nb, ROWS_Q, D_IDX)
    hi, lo = _split(qall)
    qall3 = jnp.concatenate([hi, lo, hi, jnp.zeros_like(hi)], axis=-1)
    wi_t = tok(proj["small_t"])[:, :, SM_WI:SM_WI + N_IDX_HEADS].transpose(0, 2, 1).reshape(nb, ROWS_Q, 1)
    wcol = jnp.broadcast_to(wi_t, (nb, ROWS_Q, LANES))
    q_t = by_head(tok(proj["qe_t"] + proj["qo_t"]), HEAD_DIM)
    eye = jnp.eye(N_HEADS_A, dtype=q_t.dtype)
    qbd = (q_t[:, :, :, None, :] * eye[None, :, None, :, None]).reshape(nb, ROWS_Q, ATT_WIDTH)
    ckt = cache_k.transpose(0, 2, 3, 1).reshape(pool, ATT_WIDTH, PAGE_SIZE)
    cvt = cache_v.transpose(0, 2, 3, 1).reshape(pool, ATT_WIDTH, PAGE_SIZE)
    cit = cache_kidx.transpose(0, 2, 1)

    def page(r, per_step):
        return lambda b, j, pt_ref: (pt_ref[b * n_pages + j * per_step + r], 0, 0)

    seq3 = lambda b, j, pt_ref: (b, 0, 0)
    tcol = lambda b, j, pt_ref: (0, b)
    trow = lambda b, j, pt_ref: (b, 0)

    group = math.gcd(nb, SELECT_GROUP)
    madd = pl.pallas_call(
        functools.partial(_sscore_kernel, past=past, t_new=t_new, n_top=n_top, pos_bits=pos_bits, group=group),
        grid_spec=pltpu.PrefetchScalarGridSpec(
            num_scalar_prefetch=1, grid=(nb, n_pages // SCORE_PAGES),
            in_specs=[pl.BlockSpec((None, ROWS_Q, SPLIT3), seq3), pl.BlockSpec((None, ROWS_Q, LANES), seq3),
                      pl.BlockSpec((LANES, CHUNK), tcol)]
                     + [pl.BlockSpec((None, D_IDX, PAGE_SIZE), page(r, SCORE_PAGES)) for r in range(SCORE_PAGES)],
            out_specs=pl.BlockSpec((None, group * SUBLANES, total), lambda b, j, pt_ref: (b // group, 0, 0)),
            scratch_shapes=[pltpu.VMEM((group * SUBLANES, total), F32),
                            pltpu.VMEM((group * SUBLANES, LANES), I32)]),
        out_shape=jax.ShapeDtypeStruct((nb // group, group * SUBLANES, total), F32),
        compiler_params=_cparams(("arbitrary", "arbitrary")),
    )(pt, qall3, wcol, proj["small_t"], *([cit] * SCORE_PAGES)).reshape(nb, SUBLANES, total)

    kw = ATTN_PAGES * PAGE_SIZE
    kv_specs = [pl.BlockSpec((None, ATT_WIDTH, PAGE_SIZE), page(r, ATTN_PAGES)) for r in range(ATTN_PAGES)]
    return pl.pallas_call(
        functools.partial(_sattn_kernel, past=past),
        grid_spec=pltpu.PrefetchScalarGridSpec(
            num_scalar_prefetch=1, grid=(nb, n_pages // ATTN_PAGES),
            in_specs=[pl.BlockSpec(memory_space=pltpu.SMEM),
                      pl.BlockSpec((None, ROWS_Q, ATT_WIDTH), seq3),
                      pl.BlockSpec((None, SUBLANES, kw), lambda b, j, pt_ref: (b, 0, j)),
                      pl.BlockSpec((None, SUBLANES, LANES), lambda b, j, pt_ref: (b, 0, past // LANES)),
                      pl.BlockSpec((CHUNK, ATT_WIDTH), trow), pl.BlockSpec((CHUNK, ATT_WIDTH), trow)]
                     + kv_specs * 2,
            out_specs=pl.BlockSpec((None, SUBLANES, ATT_WIDTH), seq3),
            scratch_shapes=[pltpu.VMEM((ROWS_Q, LANES), F32), pltpu.VMEM((ROWS_Q, LANES), F32),
                            pltpu.VMEM((ROWS_Q, ATT_WIDTH), F32)]),
        out_shape=jax.ShapeDtypeStruct((nb, SUBLANES, ATT_WIDTH), F32),
        compiler_params=_cparams(("arbitrary", "arbitrary")),
    )(pt, rel_bias, qbd, madd, madd, proj["k_b"], proj["v"], *([ckt] * ATTN_PAGES), *([cvt] * ATTN_PAGES))


TM_PROJ = 256
TQ_PROMPT = 256
PROJ_NAMES = ("qe_t", "qo_t", "k3", "k_b", "v", "v3", "vp_t", "ga", "z", "xbc", "qi_t", "small", "small_t")


def _layer_weights(g_pre, w_in, conv_w, conv_b, dt_bias, a_log, d_skip, g_ssm, w_out, g_post):
    offs = np.cumsum([0, ATT_WIDTH, ATT_WIDTH, ATT_WIDTH, ATT_WIDTH, QI_WIDTH, D_IDX, N_IDX_HEADS,
                      SSM_WIDTH, CONV_CH, SSM_HEADS])
    q, k, v, ga, qi, ki, wi, z, xbc, dt = [w_in[:, offs[n]:offs[n + 1]] for n in range(10)]
    pad = jnp.zeros((D_MODEL, LANES - D_IDX - N_IDX_HEADS - SSM_HEADS), F32)
    wqi_hi, wqi_lo = _split(qi.T)
    ws_hi, ws_lo = _split(jnp.concatenate([ki, wi, dt, pad], axis=1))
    return dict(g_pre=g_pre[None, :], w_rows=jnp.concatenate([k, v, ga, z, xbc], axis=1).astype(BF16),
                wq_t=q.T.astype(BF16), wqi_t_hi=wqi_hi, wqi_t_lo=wqi_lo, ws_hi=ws_hi, ws_lo=ws_lo,
                conv_w=conv_w, conv_b=conv_b, dt_bias=dt_bias, a_log=a_log, d_skip=d_skip, g_ssm=g_ssm,
                w_top=w_out[:ATT_WIDTH].astype(BF16), w_bot=w_out[ATT_WIDTH:].astype(BF16), g_post=g_post[None, :])


def _mixer(x, lw, pos_off, t_valid, h0, c0, attn_fn):
    nb, s, _ = x.shape
    tm = min(TM_PROJ, s)
    x2d = x.reshape(nb * s, D_MODEL)
    proj = dict(zip(PROJ_NAMES, _inproj(x2d, lw, _rope_tables(s, pos_off), tm)))
    att = attn_fn(proj)
    y, h_final = _ssd_out(proj, att, x2d, lw, h0, c0, nb, s, t_valid)
    r = lambda a: a.reshape(nb, s, -1)[:, :t_valid]
    heads = lambda a: a.reshape(nb, s, N_HEADS_A, HEAD_DIM)[:, :t_valid]
    conv_state = r(proj["xbc"])[:, t_valid - (CONV_W - 1):]
    return (r(y), heads(proj["k3"]), heads(proj["v3"]), r(proj["small"])[..., :D_IDX],
            h_final.reshape(nb, SSM_HEADS, SSM_HEAD_DIM, D_STATE), conv_state)


def kernel(x_prompt, x_sample, cache_k, cache_v, cache_kidx, state_ssm, state_conv, page_table, g_pre, w_in, conv_w, conv_b, dt_bias, a_log, d_skip, g_ssm, w_out, g_post, rel_bias):
    depth = w_in.shape[0]
    bp, sp, _ = x_prompt.shape
    bs, ts, _ = x_sample.shape
    past = page_table.shape[1] * PAGE_SIZE
    assert ts <= SUBLANES and ts >= CONV_W - 1 and sp % max(TQ_PROMPT, KI3_BUILD_ROWS) == 0
    assert page_table.shape[1] % SCORE_PAGES == 0 and page_table.shape[1] % ATTN_PAGES == 0
    state_rows = SSM_HEADS * SSM_HEAD_DIM
    bias_t = _bias_tiles(rel_bias, TQ_PROMPT)

    yp = x_prompt
    ys = jnp.pad(x_sample, ((0, 0), (0, CHUNK - ts), (0, 0)))
    outs_p, outs_s = [], []
    for l in range(depth):
        lw = _layer_weights(g_pre[l], w_in[l], conv_w[l], conv_b[l], dt_bias[l], a_log[l], d_skip[l], g_ssm[l],
                            w_out[l], g_post[l])

        def prompt_attn(proj):
            return _prompt_attention(proj, bias_t, bp, sp, TQ_PROMPT)

        def sample_attn(proj, layer=l):
            att8 = _sample_attention(page_table, rel_bias, proj, cache_k[layer], cache_v[layer], cache_kidx[layer],
                                     bs, ts)
            return jnp.pad(att8, ((0, 0), (0, CHUNK - SUBLANES), (0, 0))).reshape(bs * CHUNK, ATT_WIDTH)

        op = _mixer(yp, lw, 0, sp, jnp.zeros((bp, state_rows, D_STATE), F32),
                    jnp.zeros((bp, SUBLANES, CONV_CH), F32), prompt_attn)
        c0 = jnp.pad(state_conv[l], ((0, 0), (SUBLANES - (CONV_W - 1), 0), (0, 0)))
        os_ = _mixer(ys, lw, past, ts, state_ssm[l].reshape(bs, state_rows, D_STATE), c0, sample_attn)
        yp = op[0]
        ys = jnp.pad(os_[0], ((0, 0), (0, CHUNK - ts), (0, 0)))
        outs_p.append(op[1:])
        outs_s.append(os_[1:])
    stack = lambda outs, n: jnp.stack([o[n] for o in outs])
    return (yp, ys[:, :ts], *[stack(outs_p, n) for n in range(5)], *[stack(outs_s, n) for n in range(5)])
```

```python
import functools
import math

import jax
import jax.numpy as jnp
import numpy as np
from jax import lax
from jax.experimental import pallas as pl
from jax.experimental.pallas import tpu as pltpu

F32 = jnp.float32
BF16 = jnp.bfloat16
I32 = jnp.int32
HIGHEST = lax.Precision.HIGHEST

D_MODEL = 1024
PAGE_SIZE = 128
HEAD_DIM = 64
ATT_WIDTH = 512
N_HEADS_A = 8
N_IDX_HEADS = 8
D_IDX = 64
IDX_ROPE = 32
ROPE_BASE = 10000.0
TOPK_MAX = 256
NUM_BUCKETS = 32
MAX_DISTANCE = 128
SSM_WIDTH = 512
SSM_HEAD_DIM = 64
SSM_HEADS = 8
SSM_GROUPS = 2
D_STATE = 128
CONV_W = 4
CONV_CH = 1024
CHUNK = 128
EPS = 1e-6

LANES = 128
SUBLANES = 8
BF16_ROWS = 16
VMEM_LIMIT = 56 * 1024 * 1024
NEG = -1e30
INT_MIN = -2 ** 31
LOG2E = 1.4426950408889634

SM_WI = D_IDX
SM_DT = D_IDX + N_IDX_HEADS
ROW_COLS = 3 * ATT_WIDTH + SSM_WIDTH + CONV_CH
QI_WIDTH = N_IDX_HEADS * D_IDX
VP_ROWS = N_HEADS_A * LANES
SPLIT3 = 4 * D_IDX


def _nt(a, b, **kw):
    return lax.dot_general(a, b, (((1,), (1,)), ((), ())), preferred_element_type=F32, **kw)


def _dot(a, b, **kw):
    return jnp.dot(a, b, preferred_element_type=F32, **kw)


def _split(x):
    hi = x.astype(BF16)
    return hi, (x - hi.astype(F32)).astype(BF16)


def _split3(x):
    hi = x.astype(BF16)
    rest = x - hi.astype(F32)
    mid = rest.astype(BF16)
    return hi, mid, (rest - mid.astype(F32)).astype(BF16)


def _sigmoid(x):
    return 1.0 / (1.0 + jnp.exp(-x))


def _cparams(sem):
    return pltpu.CompilerParams(dimension_semantics=sem, vmem_limit_bytes=VMEM_LIMIT)


def _rope_table_kernel(inv_ref, cos_ref, sin_ref, cost_ref, sint_ref, *, pos_off):
    rows = cos_ref.shape[0]
    pos = (lax.broadcasted_iota(I32, (rows, LANES), 0) + pos_off).astype(F32)
    ang = pos * inv_ref[...]
    c = jnp.cos(ang)
    s = jnp.sin(ang)
    cos_ref[...] = c
    sin_ref[...] = s
    cost_ref[...] = c.T
    sint_ref[...] = s.T


def _rope_tables(rows, pos_off):
    inv = ROPE_BASE ** (-jnp.arange(0, IDX_ROPE, 2, dtype=F32) / IDX_ROPE)
    l64 = np.arange(LANES) % D_IDX
    inv_row = jnp.where(l64 < IDX_ROPE, inv[l64 % (IDX_ROPE // 2)], 0.0).astype(F32)[None, :]
    return pl.pallas_call(
        functools.partial(_rope_table_kernel, pos_off=pos_off),
        out_shape=(jax.ShapeDtypeStruct((rows, LANES), F32),) * 2 + (jax.ShapeDtypeStruct((LANES, rows), F32),) * 2,
    )(inv_row)


def _inproj_kernel(x_ref, g_ref, wr_ref, wqt_ref, wqih_ref, wqil_ref, wsh_ref, wsl_ref,
                   cos_ref, sin_ref, cost_ref, sint_ref,
                   qet_ref, qot_ref, k3_ref, kb_ref, v_ref, v3_ref, vpt_ref, ga_ref, z_ref, xbc_ref, qit_ref, sm_ref,
                   smt_ref):
    x = x_ref[...]
    hn = x * lax.rsqrt(jnp.mean(x * x, axis=-1, keepdims=True) + EPS) * g_ref[...]
    hb, hlo = _split(hn)
    tm = x.shape[0]

    def rows(lo, width):
        return _dot(hb, wr_ref[:, lo:lo + width])

    k = rows(0, ATT_WIDTH)
    kb_ref[...] = k.astype(BF16)
    v = rows(ATT_WIDTH, ATT_WIDTH)
    v_ref[...] = v
    for h in range(N_HEADS_A):
        k3_ref[:, h, :] = k[:, h * HEAD_DIM:(h + 1) * HEAD_DIM]
        v3_ref[:, h, :] = v[:, h * HEAD_DIM:(h + 1) * HEAD_DIM]
    ga_ref[...] = rows(2 * ATT_WIDTH, ATT_WIDTH)
    z_ref[...] = rows(3 * ATT_WIDTH, SSM_WIDTH)
    xbc_ref[...] = rows(3 * ATT_WIDTH + SSM_WIDTH, CONV_CH)

    vt = v.T
    ones = jnp.ones((HEAD_DIM, tm), BF16)
    for h in range(N_HEADS_A):
        vpt_ref[h * LANES:h * LANES + HEAD_DIM, :] = vt[h * HEAD_DIM:(h + 1) * HEAD_DIM, :].astype(BF16)
        vpt_ref[h * LANES + HEAD_DIM:(h + 1) * LANES, :] = ones

    qt = _nt(wqt_ref[...], hb) * (HEAD_DIM ** -0.5 * LOG2E)
    even = (lax.broadcasted_iota(I32, qt.shape, 0) & HEAD_DIM) == 0
    qet_ref[...] = jnp.where(even, qt, 0.0).astype(BF16)
    qot_ref[...] = jnp.where(even, 0.0, qt).astype(BF16)

    qit = _nt(wqih_ref[...], hb) + (_nt(wqih_ref[...], hlo) + _nt(wqil_ref[...], hb))
    ct = cost_ref[0:D_IDX, :]
    st = sint_ref[0:D_IDX, :]
    first_t = lax.broadcasted_iota(I32, (D_IDX, tm), 0) < IDX_ROPE // 2
    s1t = jnp.where(first_t, -st, 0.0)
    s2t = jnp.where(first_t, 0.0, st)
    for h in range(N_IDX_HEADS):
        xh = qit[h * D_IDX:(h + 1) * D_IDX, :]
        qit_ref[h * D_IDX:(h + 1) * D_IDX, :] = (xh * ct + pltpu.roll(xh, D_IDX - IDX_ROPE // 2, 0) * s1t
                                                 + pltpu.roll(xh, IDX_ROPE // 2, 0) * s2t)

    sm = _dot(hb, wsh_ref[...]) + (_dot(hlo, wsh_ref[...]) + _dot(hb, wsl_ref[...]))
    lane = lax.broadcasted_iota(I32, (tm, LANES), 1)
    is_ki = lane < D_IDX
    first = (lane & (D_IDX - 1)) < IDX_ROPE // 2
    c = jnp.where(is_ki, cos_ref[...], 1.0)
    s = jnp.where(is_ki, sin_ref[...], 0.0)
    sm = (sm * c + pltpu.roll(sm, LANES - IDX_ROPE // 2, 1) * jnp.where(first, -s, 0.0)
          + pltpu.roll(sm, IDX_ROPE // 2, 1) * jnp.where(first, 0.0, s))
    is_wi = (lane >= SM_WI) & (lane < SM_DT)
    sm = jnp.where(is_wi, sm * (N_IDX_HEADS ** -0.5), sm)
    sm_ref[...] = sm
    smt_ref[...] = sm.T


def _inproj(x2d, lw, tables, tm):
    n = x2d.shape[0]
    cos_t, sin_t, cos_tt, sin_tt = tables
    tab_blocks = cos_t.shape[0] // tm
    row = lambda i: (i, 0)
    col = lambda i: (0, i)
    const = lambda i: (0, 0)
    rows = lambda w, dt: (jax.ShapeDtypeStruct((n, w), dt), pl.BlockSpec((tm, w), row))
    cols = lambda w, dt: (jax.ShapeDtypeStruct((w, n), dt), pl.BlockSpec((w, tm), col))
    full = lambda a: pl.BlockSpec(a.shape, const)
    heads = (jax.ShapeDtypeStruct((n, N_HEADS_A, HEAD_DIM), F32),
             pl.BlockSpec((tm, N_HEADS_A, HEAD_DIM), lambda i: (i, 0, 0)))
    outs = [cols(ATT_WIDTH, BF16), cols(ATT_WIDTH, BF16), heads, rows(ATT_WIDTH, BF16),
            rows(ATT_WIDTH, F32), heads, cols(VP_ROWS, BF16), rows(ATT_WIDTH, F32), rows(SSM_WIDTH, F32),
            rows(CONV_CH, F32), cols(QI_WIDTH, F32), rows(LANES, F32), cols(LANES, F32)]
    weights = [lw["g_pre"], lw["w_rows"], lw["wq_t"], lw["wqi_t_hi"], lw["wqi_t_lo"], lw["ws_hi"], lw["ws_lo"]]
    return pl.pallas_call(
        _inproj_kernel,
        grid=(n // tm,),
        in_specs=[pl.BlockSpec((tm, D_MODEL), row)] + [full(w) for w in weights]
                 + [pl.BlockSpec((tm, LANES), lambda i: (i % tab_blocks, 0))] * 2
                 + [pl.BlockSpec((LANES, tm), lambda i: (0, i % tab_blocks))] * 2,
        out_specs=[o[1] for o in outs],
        out_shape=[o[0] for o in outs],
        compiler_params=_cparams(("arbitrary",)),
    )(x2d, *weights, cos_t, sin_t, cos_tt, sin_tt)


def _bucket(dist):
    max_exact = NUM_BUCKETS // 2
    n = jnp.maximum(dist, 0)
    nf = jnp.maximum(n, max_exact).astype(F32)
    large = max_exact + jnp.floor(jnp.log(nf / max_exact) / math.log(MAX_DISTANCE / max_exact)
                                  * (NUM_BUCKETS - max_exact)).astype(I32)
    large = jnp.minimum(large, NUM_BUCKETS - 1)
    return jnp.where(n < max_exact, n, large)


def _bias_lookup(bucket, relb_ref, h):
    out = jnp.full(bucket.shape, relb_ref[0, h], F32)
    for b in range(1, NUM_BUCKETS):
        out = jnp.where(bucket == b, relb_ref[b, h], out)
    return out


KEY_NEG_INF = INT_MIN + 0x7FFFFF


def _bit(n):
    return lax.shift_left(jnp.int32(1), jnp.asarray(n, I32))


def _key_to_float(key):
    return pltpu.bitcast(jnp.where(key < 0, key ^ 0x7FFFFFFF, key), F32)


def _search_threshold(count_ge, n_top, n_keys, shape):
    def body(it, thr):
        cand = thr ^ _bit(31 - it)
        cnt = jnp.where(cand < KEY_NEG_INF, jnp.asarray(n_keys, F32), count_ge(_key_to_float(cand)))
        return jnp.where(cnt >= float(n_top), cand, thr)

    return _key_to_float(lax.fori_loop(0, 32, body, jnp.full(shape, INT_MIN, I32)))


def _bf16_step_bits(u):
    b = u - 32768
    return lax.shift_left(jnp.where(b < 0, b ^ 0x7FFF, b), jnp.int32(16))


def _float_image(bits):
    return jnp.where(bits < 0, bits ^ 0x7FFFFFFF, bits)


def _search_threshold_2level(count_ge_rounded, count_ge, n_top, n_keys, shape):
    k = float(n_top)
    n_all = jnp.asarray(n_keys, F32)

    def admits(u, counter):
        bits = _bf16_step_bits(u)
        return jnp.where(_float_image(bits) < KEY_NEG_INF, n_all, counter(pltpu.bitcast(bits, F32))) >= k

    def coarse(it, u):
        cand = u | _bit(15 - it)
        return jnp.where(admits(cand, count_ge_rounded), cand, u)

    u = lax.fori_loop(0, 16, coarse, jnp.zeros(shape, I32))
    u = jnp.where(admits(u, count_ge), u, u - 1)
    base = _float_image(_bf16_step_bits(u))

    def fine(it, d):
        cand = d | _bit(15 - it)
        return jnp.where(count_ge(_key_to_float(base + cand)) >= k, cand, d)

    return _key_to_float(base + lax.fori_loop(0, 16, fine, jnp.zeros(shape, I32)))


def _search_last_tie(count_ties_before, need, pos_bits, shape):
    def body(it, q):
        cand = q | _bit(pos_bits - 1 - it)
        return jnp.where(count_ties_before(cand) < need, cand, q)

    return lax.fori_loop(0, pos_bits, body, jnp.zeros(shape, I32))


def _select_madd(score, pos, thr, last):
    return jnp.where(score > thr, 0.0, jnp.where(score == thr, jnp.where(pos <= last, 0.0, NEG), NEG))


def _bias_tiles_kernel(relb_ref, o_ref, *, tq):
    ki = lax.broadcasted_iota(I32, (tq, tq), 0)
    qi = lax.broadcasted_iota(I32, (tq, tq), 1)
    for kind in range(2):
        bucket = _bucket(qi - ki + kind * tq)
        for h in range(N_HEADS_A):
            o_ref[h, kind] = (_bias_lookup(bucket, relb_ref, h) - relb_ref[NUM_BUCKETS - 1, h]) * LOG2E


def _bias_tiles(rel_bias, tq):
    return pl.pallas_call(
        functools.partial(_bias_tiles_kernel, tq=tq),
        in_specs=[pl.BlockSpec(memory_space=pltpu.SMEM)],
        out_shape=jax.ShapeDtypeStruct((N_HEADS_A, 2, tq, tq), F32),
        compiler_params=pltpu.CompilerParams(vmem_limit_bytes=VMEM_LIMIT),
    )(rel_bias)


KI3_BUILD_ROWS = 512


def _pattn_kernel(qit_ref, smtq_ref, sm_ref, qet_ref, qot_ref, kb_ref, vpt_ref, bt_ref, o_ref,
                  ki3_ref, qh3_ref, sc_ref, hi_ref, last_ref, m_ref, acc_ref, *, tq, n_top, pos_bits):
    i = pl.program_id(1)
    nch = i + 1
    s_len = sc_ref.shape[0]
    groups = tq // SUBLANES
    kiota = lax.broadcasted_iota(I32, (tq, tq), 0)
    qpos = i * tq + lax.broadcasted_iota(I32, (tq, tq), 1)

    def rows(c, width=tq):
        return pl.ds(pl.multiple_of(c * tq, tq), width)

    @pl.when(i == 0)
    def _():
        low = lax.broadcasted_iota(I32, (KI3_BUILD_ROWS, LANES), 1) < D_IDX

        def body(r, carry):
            sl = pl.ds(pl.multiple_of(r * KI3_BUILD_ROWS, KI3_BUILD_ROWS), KI3_BUILD_ROWS)
            x = sm_ref[sl, :]
            hi = x.astype(BF16).astype(F32)
            ki3_ref[sl, 0:LANES] = jnp.where(low, hi, pltpu.roll(hi, D_IDX, 1)).astype(BF16)
            ki3_ref[sl, LANES:2 * LANES] = jnp.where(low, x - hi, 0.0).astype(BF16)
            return carry
        lax.fori_loop(0, s_len // KI3_BUILD_ROWS, body, 0)

    for h in range(N_IDX_HEADS):
        hi, lo = _split(qit_ref[h * D_IDX:(h + 1) * D_IDX, :])
        qh3_ref[h, 0:D_IDX, :] = hi
        qh3_ref[h, D_IDX:2 * D_IDX, :] = lo
        qh3_ref[h, 2 * D_IDX:3 * D_IDX, :] = hi
        qh3_ref[h, 3 * D_IDX:, :] = jnp.zeros((D_IDX, tq), BF16)
    w8 = smtq_ref[SM_WI:SM_WI + N_IDX_HEADS, :] * (D_IDX ** -0.5)

    def score_body(c2, carry):
        blocks = [2 * c2, jnp.minimum(2 * c2 + 1, i)]
        dots = [[_dot(ki3_ref[rows(c), :], qh3_ref[h]) for h in range(N_IDX_HEADS)] for c in blocks]
        for c, d in zip(blocks, dots):
            terms = [jnp.maximum(d[h], 0.0) * w8[h:h + 1, :] for h in range(N_IDX_HEADS)]
            while len(terms) > 1:
                terms = [terms[j] + terms[j + 1] for j in range(0, len(terms), 2)]
            sc = jnp.where(c * tq + kiota <= qpos, terms[0], -jnp.inf)
            sc_ref[rows(c), :] = sc
            hi_ref[rows(c), :] = sc.astype(BF16)
        return carry

    lax.fori_loop(0, (nch + 1) // 2, score_body, 0)

    def over_keys(x, op):
        x = x.reshape(x.shape[0] // SUBLANES, SUBLANES, tq)
        while x.shape[0] > 1:
            half = x.shape[0] // 2
            x = op(x[:half], x[half:])
        return x[0]

    def count(pred):
        def body(c, acc):
            hit = jnp.where(pred(sc_ref[rows(c), :], c * tq + kiota), 1.0, 0.0)
            return acc + over_keys(hit, jnp.add)
        acc = lax.fori_loop(0, nch, body, jnp.zeros((SUBLANES, tq), F32))
        return jnp.broadcast_to(jnp.sum(acc, axis=0, keepdims=True), (SUBLANES, tq))

    def count_rounded(t):
        t16 = jnp.concatenate([t, t], axis=0).astype(BF16)

        def body(c, acc):
            h = hi_ref[rows(c), :].reshape(tq // BF16_ROWS, BF16_ROWS, tq)
            hit = jnp.where(h >= t16[None], jnp.ones_like(h), jnp.zeros_like(h))
            while hit.shape[0] > 1:
                half = hit.shape[0] // 2
                hit = hit[:half] + hit[half:]
            part = hit[0].astype(F32)
            return acc + (part[0:SUBLANES] + part[SUBLANES:])
        acc = lax.fori_loop(0, nch, body, jnp.zeros((SUBLANES, tq), F32))
        return jnp.broadcast_to(jnp.sum(acc, axis=0, keepdims=True), (SUBLANES, tq))

    thr = _search_threshold_2level(count_rounded, lambda t: count(lambda sc, pos: sc >= t[0:1, :]),
                                   n_top, nch * tq, (SUBLANES, tq))
    thr_row = thr[0:1, :]
    need = float(n_top) - count(lambda sc, pos: sc > thr_row)
    n_eq = count(lambda sc, pos: sc == thr_row)
    last_ref[...] = jnp.full((SUBLANES, tq), 2 ** pos_bits - 1, I32)

    @pl.when(jnp.max(n_eq - need) > 0.0)
    def _():
        last_ref[...] = _search_last_tie(
            lambda q: count(lambda sc, pos: (sc == thr_row) & (pos < q[0:1, :])), need, pos_bits, (SUBLANES, tq))

    last_row = last_ref[0:1, :]

    def madd_body(c, carry):
        pos = c * tq + kiota
        madd = _select_madd(sc_ref[rows(c), :], pos, thr_row, last_row)
        sc_ref[rows(c), :] = jnp.where(pos <= qpos, madd, NEG)
        return carry

    lax.fori_loop(0, nch, madd_body, 0)

    m_ref[...] = jnp.full(m_ref.shape, NEG, F32)
    acc_ref[...] = jnp.zeros(acc_ref.shape, F32)

    def attend(c0, width, bias_of_head):
        sl = rows(c0, width)
        madd = sc_ref[sl, :]
        logits = []
        for h in range(N_HEADS_A):
            p2 = h // 2
            qt = (qet_ref if h % 2 == 0 else qot_ref)[p2 * LANES:(p2 + 1) * LANES, :]
            logits.append(_dot(kb_ref[sl, p2 * LANES:(p2 + 1) * LANES], qt))
        probs, alphas = [], []
        for h in range(N_HEADS_A):
            s = logits[h] + madd
            if bias_of_head is not None:
                s = s + bias_of_head(h)
            m_prev = m_ref[h]
            cmax = over_keys(s, jnp.maximum)
            m_new = jnp.maximum(m_prev, jnp.broadcast_to(jnp.max(cmax, axis=0, keepdims=True), (SUBLANES, tq)))
            probs.append(jnp.exp2(s - m_new[0:1, :]).astype(BF16))
            alphas.append(jnp.exp2(m_prev - m_new)[0:1, :])
            m_ref[h] = m_new
        for h in range(N_HEADS_A):
            pv = _dot(vpt_ref[h * LANES:(h + 1) * LANES, sl], probs[h])
            acc_ref[h] = alphas[h] * acc_ref[h] + pv

    n_far = jnp.maximum(i - 1, 0)

    def far_body(c, carry):
        attend(2 * c, 2 * tq, None)
        return carry

    lax.fori_loop(0, n_far // 2, far_body, 0)

    @pl.when(n_far % 2 == 1)
    def _():
        attend(n_far - 1, tq, None)

    @pl.when(i >= 1)
    def _():
        attend(i - 1, tq, lambda h: bt_ref[h, 1])

    attend(i, tq, lambda h: bt_ref[h, 0])

    outs = []
    for h in range(N_HEADS_A):
        acc = acc_ref[h]
        outs.append(acc[0:HEAD_DIM, :] / acc[HEAD_DIM:, :])
    o_ref[...] = jnp.concatenate(outs, axis=0).T


def _prompt_attention(proj, bias_t, nb, s, tq):
    nq = s // tq
    n_top = min(TOPK_MAX, s // 4)
    pos_bits = max(1, (s - 1).bit_length())
    qcols = lambda b, i: (0, b * nq + i)
    seq_rows = lambda b, i: (b, 0)
    seq_cols = lambda b, i: (0, b)
    return pl.pallas_call(
        functools.partial(_pattn_kernel, tq=tq, n_top=n_top, pos_bits=pos_bits),
        grid=(nb, nq),
        in_specs=[pl.BlockSpec((QI_WIDTH, tq), qcols), pl.BlockSpec((LANES, tq), qcols),
                  pl.BlockSpec((s, LANES), seq_rows),
                  pl.BlockSpec((ATT_WIDTH, tq), qcols), pl.BlockSpec((ATT_WIDTH, tq), qcols),
                  pl.BlockSpec((s, ATT_WIDTH), seq_rows), pl.BlockSpec((VP_ROWS, s), seq_cols),
                  pl.BlockSpec((N_HEADS_A, 2, tq, tq), lambda b, i: (0, 0, 0, 0),
                               pipeline_mode=pl.Buffered(1))],
        out_specs=pl.BlockSpec((tq, ATT_WIDTH), lambda b, i: (b * nq + i, 0)),
        out_shape=jax.ShapeDtypeStruct((nb * s, ATT_WIDTH), F32),
        scratch_shapes=[pltpu.VMEM((s, SPLIT3), BF16), pltpu.VMEM((N_IDX_HEADS, SPLIT3, tq), BF16),
                        pltpu.VMEM((s, tq), F32), pltpu.VMEM((s, tq), BF16), pltpu.VMEM((SUBLANES, tq), I32),
                        pltpu.VMEM((N_HEADS_A, SUBLANES, tq), F32), pltpu.VMEM((N_HEADS_A, LANES, tq), F32)],
        compiler_params=_cparams(("arbitrary", "arbitrary")),
    )(proj["qi_t"], proj["small_t"], proj["small"], proj["qe_t"], proj["qo_t"], proj["k_b"], proj["vp_t"], bias_t)


def _ssd_kernel(xbc_ref, z_ref, sm_ref, cw_ref, cb_ref, dtb_ref, alog_ref, dsk_ref, gs_ref, ex_ref, ext_ref,
                h0_ref, c0_ref, att_ref, ga_ref, x_ref, wt_ref, wb_ref, gp_ref, y_ref, hf_ref, xp_ref, st_ref, *,
                t_valid):
    c = pl.program_id(1)
    L = CHUNK

    @pl.when(c == 0)
    def _():
        st_ref[...] = h0_ref[...]
        xp_ref[0:SUBLANES, :] = c0_ref[...]

    xp_ref[SUBLANES:SUBLANES + L, :] = xbc_ref[...]
    conv = cb_ref[...]
    for j in range(CONV_W):
        lo = SUBLANES - (CONV_W - 1) + j
        conv = conv + xp_ref[lo:lo + L, :] * cw_ref[j:j + 1, :]
    xp_ref[0:SUBLANES, :] = xp_ref[L:L + SUBLANES, :]
    act = conv * _sigmoid(conv)
    xs = act[:, :SSM_WIDTH]
    bm = act[:, SSM_WIDTH:SSM_WIDTH + SSM_GROUPS * D_STATE]
    cm = act[:, SSM_WIDTH + SSM_GROUPS * D_STATE:]

    raw = sm_ref[...] + dtb_ref[...]
    dtf = jnp.maximum(raw, 0.0) + jnp.log1p(jnp.exp(-jnp.abs(raw)))
    row = lax.broadcasted_iota(I32, (L, LANES), 0)
    if t_valid < L:
        dtf = jnp.where(row < t_valid, dtf, 0.0)
    adt = dtf * (-jnp.exp(alog_ref[...]))
    tril = row >= lax.broadcasted_iota(I32, (L, LANES), 1)
    tril01 = jnp.where(tril, 1.0, 0.0).astype(BF16)
    cs = sum(_dot(tril01, part) for part in _split3(adt))
    ex = ex_ref[...]
    dtx = sum(_dot(part, ex) for part in _split3(dtf))
    csx = sum(_dot(part, ex) for part in _split3(cs))
    cst = cs.T
    x = xs * dtx
    w = x * jnp.exp(csx[L - 1:L, :] - csx)
    ecsx = jnp.exp(csx)
    dec = jnp.exp(jnp.sum(ext_ref[...] * cs[L - 1:L, :], axis=1, keepdims=True))
    low = lax.broadcasted_iota(I32, (L, LANES), 1) < SSM_HEAD_DIM

    ys = []
    for p2 in range(SSM_HEADS // 2):
        g = (2 * p2) // (SSM_HEADS // SSM_GROUPS)
        cg = cm[:, g * D_STATE:(g + 1) * D_STATE].astype(BF16)
        bg = bm[:, g * D_STATE:(g + 1) * D_STATE].astype(BF16)
        cb_mat = _nt(cg, bg)
        lanes = slice(p2 * LANES, (p2 + 1) * LANES)
        xp = x[:, lanes].astype(BF16)
        yd = []
        for h in (2 * p2, 2 * p2 + 1):
            diff = cs[:, SM_DT + h:SM_DT + h + 1] - cst[SM_DT + h:SM_DT + h + 1, :]
            lm = jnp.exp(jnp.where(tril, diff, NEG))
            yd.append(_dot((cb_mat * lm).astype(BF16), xp))
        rows = slice(p2 * LANES, (p2 + 1) * LANES)
        st = st_ref[rows, :]
        y_off = _nt(cg, st.astype(BF16)) * ecsx[:, lanes]
        ys.append(jnp.where(low, yd[0], yd[1]) + y_off)
        upd = _dot(w[:, lanes].T.astype(BF16), bg)
        st_ref[rows, :] = st * dec[rows, :] + upd

    y = jnp.concatenate(ys, axis=1) + dsk_ref[...] * xs
    zz = z_ref[...]
    gated = y * (zz * _sigmoid(zz))
    ssm = gated * lax.rsqrt(jnp.mean(gated * gated, axis=-1, keepdims=True) + EPS) * gs_ref[...]

    ga = ga_ref[...]
    att = att_ref[...] * (ga * _sigmoid(ga))
    out = _dot(att.astype(BF16), wt_ref[...]) + _dot(ssm.astype(BF16), wb_ref[...])
    y_ref[...] = x_ref[...] + out * lax.rsqrt(jnp.mean(out * out, axis=-1, keepdims=True) + EPS) * gp_ref[...]

    @pl.when(c == pl.num_programs(1) - 1)
    def _():
        hf_ref[...] = st_ref[...]


def _ssd_out(proj, att, x2d, lw, h0, c0, nb, s, t_valid):
    xbc, z, small, ga = proj["xbc"], proj["z"], proj["small"], proj["ga"]
    nc = s // CHUNK
    blk = lambda b, c: (b * nc + c, 0)
    const = lambda b, c: (0, 0)
    per_b = lambda b, c: (b, 0, 0)
    lanes = np.arange(LANES)
    dt_row = lambda v: jnp.zeros((1, LANES), F32).at[0, SM_DT:SM_DT + SSM_HEADS].set(v)
    expand = (lanes[:, None] == SM_DT + np.arange(SSM_WIDTH)[None, :] // SSM_HEAD_DIM).astype(np.float32)
    state_rows = SSM_HEADS * SSM_HEAD_DIM
    return pl.pallas_call(
        functools.partial(_ssd_kernel, t_valid=t_valid),
        grid=(nb, nc),
        in_specs=[pl.BlockSpec((CHUNK, CONV_CH), blk), pl.BlockSpec((CHUNK, SSM_WIDTH), blk),
                  pl.BlockSpec((CHUNK, LANES), blk),
                  pl.BlockSpec((CONV_W, CONV_CH), const), pl.BlockSpec((1, CONV_CH), const),
                  pl.BlockSpec((1, LANES), const), pl.BlockSpec((1, LANES), const),
                  pl.BlockSpec((1, SSM_WIDTH), const), pl.BlockSpec((1, SSM_WIDTH), const),
                  pl.BlockSpec((LANES, SSM_WIDTH), const), pl.BlockSpec((SSM_WIDTH, LANES), const),
                  pl.BlockSpec((None, state_rows, D_STATE), per_b),
                  pl.BlockSpec((None, SUBLANES, CONV_CH), per_b),
                  pl.BlockSpec((CHUNK, ATT_WIDTH), blk), pl.BlockSpec((CHUNK, ATT_WIDTH), blk),
                  pl.BlockSpec((CHUNK, D_MODEL), blk),
                  pl.BlockSpec((ATT_WIDTH, D_MODEL), const), pl.BlockSpec((SSM_WIDTH, D_MODEL), const),
                  pl.BlockSpec((1, D_MODEL), const)],
        out_specs=[pl.BlockSpec((CHUNK, D_MODEL), blk), pl.BlockSpec((None, state_rows, D_STATE), per_b)],
        out_shape=[jax.ShapeDtypeStruct((nb * s, D_MODEL), F32),
                   jax.ShapeDtypeStruct((nb, state_rows, D_STATE), F32)],
        scratch_shapes=[pltpu.VMEM((CHUNK + SUBLANES, CONV_CH), F32), pltpu.VMEM((state_rows, D_STATE), F32)],
        compiler_params=_cparams(("arbitrary", "arbitrary")),
    )(xbc, z, small, lw["conv_w"], lw["conv_b"][None, :], dt_row(lw["dt_bias"]), dt_row(lw["a_log"]),
      jnp.repeat(lw["d_skip"], SSM_HEAD_DIM)[None, :], lw["g_ssm"][None, :], jnp.asarray(expand, BF16),
      jnp.asarray(expand.T), h0, c0, att, ga, x2d, lw["w_top"], lw["w_bot"], lw["g_post"])


SCORE_PAGES = 32
ATTN_PAGES = 16
ROWS_Q = N_HEADS_A * SUBLANES
COUNT_CHAINS = 4
SELECT_GROUP = 4


def _sscore_kernel(pt_ref, qall3_ref, wcol_ref, smt_ref, *rest, past, t_new, n_top, pos_bits, group):
    pages = rest[:SCORE_PAGES]
    madd_ref = rest[SCORE_PAGES]
    sc_ref, last_ref = rest[SCORE_PAGES + 1:]
    j = pl.program_id(1)
    member = pl.program_id(0) % group
    mine = pl.ds(pl.multiple_of(member * SUBLANES, SUBLANES), SUBLANES)
    rows_g = group * SUBLANES
    kw = SCORE_PAGES * PAGE_SIZE
    total = past + LANES
    qall3 = qall3_ref[...]
    wcol = wcol_ref[...]
    zeros = jnp.zeros((D_IDX, PAGE_SIZE), BF16)

    def dots(kt):
        hi, lo = _split(kt)
        return _dot(qall3, jnp.concatenate([hi, hi, lo, zeros], axis=0))

    def weigh(d):
        r = jnp.maximum(d * (D_IDX ** -0.5), 0.0) * wcol
        sc = r[0:SUBLANES]
        for h in range(1, N_IDX_HEADS):
            sc = sc + r[h * SUBLANES:(h + 1) * SUBLANES]
        return sc

    def scores(kt):
        return weigh(dots(kt))

    page_dots = [dots(page[...]) for page in pages]
    for r, d in enumerate(page_dots):
        sl = pl.ds(pl.multiple_of(j * kw + r * PAGE_SIZE, PAGE_SIZE), PAGE_SIZE)
        sc_ref[mine, sl] = weigh(d)

    @pl.when(j == pl.num_programs(1) - 1)
    def _():
        lane8 = lax.broadcasted_iota(I32, (SUBLANES, LANES), 1)
        row8 = lax.broadcasted_iota(I32, (SUBLANES, LANES), 0)
        vis8 = (lane8 <= row8) & (lane8 < t_new)
        sc_ref[mine, past:total] = jnp.where(vis8, scores(smt_ref[0:D_IDX, :]), -jnp.inf)

    @pl.when(jnp.logical_and(j == pl.num_programs(1) - 1, member == group - 1))
    def _():
        shape = (rows_g, LANES)
        lane = lax.broadcasted_iota(I32, shape, 1)
        tok = lax.broadcasted_iota(I32, shape, 0) % SUBLANES
        vis = (lane <= tok) & (lane < t_new)

        def count(pred):
            accs = [jnp.zeros(shape, F32)] * COUNT_CHAINS
            for t in range(total // LANES):
                hit = pred(sc_ref[:, t * LANES:(t + 1) * LANES], t * LANES + lane)
                accs[t % COUNT_CHAINS] = accs[t % COUNT_CHAINS] + jnp.where(hit, 1.0, 0.0)
            while len(accs) > 1:
                accs = [accs[n] + accs[n + 1] for n in range(0, len(accs), 2)]
            return jnp.broadcast_to(jnp.sum(accs[0], axis=1, keepdims=True), shape)

        thr = _search_threshold(lambda t: count(lambda sc, pos: sc >= t), n_top, total, shape)
        need = float(n_top) - count(lambda sc, pos: sc > thr)
        n_eq = count(lambda sc, pos: sc == thr)
        last_ref[...] = jnp.full(shape, 2 ** pos_bits - 1, I32)

        @pl.when(jnp.max(n_eq - need) > 0.0)
        def _():
            last_ref[...] = _search_last_tie(
                lambda q: count(lambda sc, pos: (sc == thr) & (pos < q)), need, pos_bits, shape)

        last = last_ref[...]
        for t in range(total // LANES):
            sl = slice(t * LANES, (t + 1) * LANES)
            pos = t * LANES + lane
            madd = _select_madd(sc_ref[:, sl], pos, thr, last)
            if t * LANES >= past:
                madd = jnp.where(vis, madd, NEG)
            madd_ref[:, sl] = madd


def _sattn_kernel(pt_ref, relb_ref, qbd_ref, madd_ref, maddn_ref, kbnew_ref, vnew_ref, *rest, past):
    kpages = rest[:ATTN_PAGES]
    vpages = rest[ATTN_PAGES:2 * ATTN_PAGES]
    o_ref, m_ref, l_ref, acc_ref = rest[2 * ATTN_PAGES:]
    j = pl.program_id(1)
    kw = ATTN_PAGES * PAGE_SIZE

    @pl.when(j == 0)
    def _():
        m_ref[...] = jnp.full(m_ref.shape, NEG, F32)
        l_ref[...] = jnp.zeros(l_ref.shape, F32)
        acc_ref[...] = jnp.zeros(acc_ref.shape, F32)

    qbd = qbd_ref[...]

    def far_bias(width):
        row_head = lax.broadcasted_iota(I32, (ROWS_Q, width), 0) // SUBLANES
        out = jnp.full((ROWS_Q, width), relb_ref[NUM_BUCKETS - 1, 0], F32)
        for h in range(1, N_HEADS_A):
            out = jnp.where(row_head == h, relb_ref[NUM_BUCKETS - 1, h], out)
        return out

    def near_bias(width, pos0):
        tok = lax.broadcasted_iota(I32, (SUBLANES, width), 0)
        pos = pos0 + lax.broadcasted_iota(I32, (SUBLANES, width), 1)
        bucket = _bucket(past + tok - pos)
        return jnp.concatenate([_bias_lookup(bucket, relb_ref, h) for h in range(N_HEADS_A)], axis=0)

    def update(logits, bias, madd8, pv_fn):
        s = logits + (bias * LOG2E + jnp.concatenate([madd8] * N_HEADS_A, axis=0))
        m_prev = m_ref[...]
        m_new = jnp.maximum(m_prev, jnp.broadcast_to(jnp.max(s, axis=1, keepdims=True), (ROWS_Q, LANES)))
        alpha = jnp.exp2(m_prev - m_new)
        p = jnp.exp2(s - m_new[:, :1])
        l_ref[...] = alpha * l_ref[...] + jnp.broadcast_to(jnp.sum(p, axis=1, keepdims=True), (ROWS_Q, LANES))
        acc_ref[...] = alpha[:, :1] * acc_ref[...] + pv_fn(p.astype(BF16))
        m_ref[...] = m_new

    def paged(bias):
        logits = jnp.concatenate([_dot(qbd, kp[...].astype(BF16)) for kp in kpages], axis=1)

        def pv_fn(p):
            pv = _nt(p[:, 0:PAGE_SIZE], vpages[0][...].astype(BF16))
            for r in range(1, ATTN_PAGES):
                pv = pv + _nt(p[:, r * PAGE_SIZE:(r + 1) * PAGE_SIZE], vpages[r][...].astype(BF16))
            return pv

        update(logits, bias, madd_ref[...], pv_fn)

    far = (j + 1) * kw + MAX_DISTANCE <= past + 1

    @pl.when(far)
    def _():
        paged(far_bias(kw))

    @pl.when(jnp.logical_not(far))
    def _():
        paged(near_bias(kw, j * kw))

    @pl.when(j == pl.num_programs(1) - 1)
    def _():
        update(_nt(qbd, kbnew_ref[...]), near_bias(LANES, past), maddn_ref[...],
               lambda p: _dot(p, vnew_ref[...].astype(BF16)))
        o = acc_ref[...] / l_ref[:, :1]
        own = (lax.broadcasted_iota(I32, (ROWS_Q, ATT_WIDTH), 0) // SUBLANES
               == lax.broadcasted_iota(I32, (ROWS_Q, ATT_WIDTH), 1) // HEAD_DIM)
        o = jnp.where(own, o, 0.0)
        out = o[0:SUBLANES]
        for h in range(1, N_HEADS_A):
            out = out + o[h * SUBLANES:(h + 1) * SUBLANES]
        o_ref[...] = out


def _sample_attention(page_table, rel_bias, proj, cache_k, cache_v, cache_kidx, nb, t_new):
    n_pages = page_table.shape[1]
    past = n_pages * PAGE_SIZE
    total = past + LANES
    n_top = min(TOPK_MAX, (past + t_new) // 4)
    pos_bits = max(1, (total - 1).bit_length())
    pt = page_table.reshape(-1)
    pool = cache_kidx.shape[0]

    def tok(a_t):
        return a_t.reshape(a_t.shape[0], nb, CHUNK)[:, :, :SUBLANES].transpose(1, 2, 0)

    by_head = lambda a, d: a.reshape(nb, SUBLANES, -1, d).transpose(0, 2, 1, 3)
    qall = by_head(tok(proj["qi_t"]), D_IDX).reshape(nb, ROWS_Q, D_IDX)
    hi, lo = _split(qall)
    qall3 = jnp.concatenate([hi, lo, hi, jnp.zeros_like(hi)], axis=-1)
    wi_t = tok(proj["small_t"])[:, :, SM_WI:SM_WI + N_IDX_HEADS].transpose(0, 2, 1).reshape(nb, ROWS_Q, 1)
    wcol = jnp.broadcast_to(wi_t, (nb, ROWS_Q, LANES))
    q_t = by_head(tok(proj["qe_t"] + proj["qo_t"]), HEAD_DIM)
    eye = jnp.eye(N_HEADS_A, dtype=q_t.dtype)
    qbd = (q_t[:, :, :, None, :] * eye[None, :, None, :, None]).reshape(nb, ROWS_Q, ATT_WIDTH)
    ckt = cache_k.transpose(0, 2, 3, 1).reshape(pool, ATT_WIDTH, PAGE_SIZE)
    cvt = cache_v.transpose(0, 2, 3, 1).reshape(pool, ATT_WIDTH, PAGE_SIZE)
    cit = cache_kidx.transpose(0, 2, 1)

    def page(r, per_step):
        return lambda b, j, pt_ref: (pt_ref[b * n_pages + j * per_step + r], 0, 0)

    seq3 = lambda b, j, pt_ref: (b, 0, 0)
    tcol = lambda b, j, pt_ref: (0, b)
    trow = lambda b, j, pt_ref: (b, 0)

    group = math.gcd(nb, SELECT_GROUP)
    madd = pl.pallas_call(
        functools.partial(_sscore_kernel, past=past, t_new=t_new, n_top=n_top, pos_bits=pos_bits, group=group),
        grid_spec=pltpu.PrefetchScalarGridSpec(
            num_scalar_prefetch=1, grid=(nb, n_pages // SCORE_PAGES),
            in_specs=[pl.BlockSpec((None, ROWS_Q, SPLIT3), seq3), pl.BlockSpec((None, ROWS_Q, LANES), seq3),
                      pl.BlockSpec((LANES, CHUNK), tcol)]
                     + [pl.BlockSpec((None, D_IDX, PAGE_SIZE), page(r, SCORE_PAGES)) for r in range(SCORE_PAGES)],
            out_specs=pl.BlockSpec((None, group * SUBLANES, total), lambda b, j, pt_ref: (b // group, 0, 0)),
            scratch_shapes=[pltpu.VMEM((group * SUBLANES, total), F32),
                            pltpu.VMEM((group * SUBLANES, LANES), I32)]),
        out_shape=jax.ShapeDtypeStruct((nb // group, group * SUBLANES, total), F32),
        compiler_params=_cparams(("arbitrary", "arbitrary")),
    )(pt, qall3, wcol, proj["small_t"], *([cit] * SCORE_PAGES)).reshape(nb, SUBLANES, total)

    kw = ATTN_PAGES * PAGE_SIZE
    kv_specs = [pl.BlockSpec((None, ATT_WIDTH, PAGE_SIZE), page(r, ATTN_PAGES)) for r in range(ATTN_PAGES)]
    return pl.pallas_call(
        functools.partial(_sattn_kernel, past=past),
        grid_spec=pltpu.PrefetchScalarGridSpec(
            num_scalar_prefetch=1, grid=(nb, n_pages // ATTN_PAGES),
            in_specs=[pl.BlockSpec(memory_space=pltpu.SMEM),
                      pl.BlockSpec((None, ROWS_Q, ATT_WIDTH), seq3),
                      pl.BlockSpec((None, SUBLANES, kw), lambda b, j, pt_ref: (b, 0, j)),
                      pl.BlockSpec((None, SUBLANES, LANES), lambda b, j, pt_ref: (b, 0, past // LANES)),
                      pl.BlockSpec((CHUNK, ATT_WIDTH), trow), pl.BlockSpec((CHUNK, ATT_WIDTH), trow)]
                     + kv_specs * 2,
            out_specs=pl.BlockSpec((None, SUBLANES, ATT_WIDTH), seq3),
            scratch_shapes=[pltpu.VMEM((ROWS_Q, LANES), F32), pltpu.VMEM((ROWS_Q, LANES), F32),
                            pltpu.VMEM((ROWS_Q, ATT_WIDTH), F32)]),
        out_shape=jax.ShapeDtypeStruct((nb, SUBLANES, ATT_WIDTH), F32),
        compiler_params=_cparams(("arbitrary", "arbitrary")),
    )(pt, rel_bias, qbd, madd, madd, proj["k_b"], proj["v"], *([ckt] * ATTN_PAGES), *([cvt] * ATTN_PAGES))


TM_PROJ = 256
TQ_PROMPT = 256
PROJ_NAMES = ("qe_t", "qo_t", "k3", "k_b", "v", "v3", "vp_t", "ga", "z", "xbc", "qi_t", "small", "small_t")


def _layer_weights(g_pre, w_in, conv_w, conv_b, dt_bias, a_log, d_skip, g_ssm, w_out, g_post):
    offs = np.cumsum([0, ATT_WIDTH, ATT_WIDTH, ATT_WIDTH, ATT_WIDTH, QI_WIDTH, D_IDX, N_IDX_HEADS,
                      SSM_WIDTH, CONV_CH, SSM_HEADS])
    q, k, v, ga, qi, ki, wi, z, xbc, dt = [w_in[:, offs[n]:offs[n + 1]] for n in range(10)]
    pad = jnp.zeros((D_MODEL, LANES - D_IDX - N_IDX_HEADS - SSM_HEADS), F32)
    wqi_hi, wqi_lo = _split(qi.T)
    ws_hi, ws_lo = _split(jnp.concatenate([ki, wi, dt, pad], axis=1))
    return dict(g_pre=g_pre[None, :], w_rows=jnp.concatenate([k, v, ga, z, xbc], axis=1).astype(BF16),
                wq_t=q.T.astype(BF16), wqi_t_hi=wqi_hi, wqi_t_lo=wqi_lo, ws_hi=ws_hi, ws_lo=ws_lo,
                conv_w=conv_w, conv_b=conv_b, dt_bias=dt_bias, a_log=a_log, d_skip=d_skip, g_ssm=g_ssm,
                w_top=w_out[:ATT_WIDTH].astype(BF16), w_bot=w_out[ATT_WIDTH:].astype(BF16), g_post=g_post[None, :])


def _mixer(x, lw, pos_off, t_valid, h0, c0, attn_fn):
    nb, s, _ = x.shape
    tm = min(TM_PROJ, s)
    x2d = x.reshape(nb * s, D_MODEL)
    proj = dict(zip(PROJ_NAMES, _inproj(x2d, lw, _rope_tables(s, pos_off), tm)))
    att = attn_fn(proj)
    y, h_final = _ssd_out(proj, att, x2d, lw, h0, c0, nb, s, t_valid)
    r = lambda a: a.reshape(nb, s, -1)[:, :t_valid]
    heads = lambda a: a.reshape(nb, s, N_HEADS_A, HEAD_DIM)[:, :t_valid]
    conv_state = r(proj["xbc"])[:, t_valid - (CONV_W - 1):]
    return (r(y), heads(proj["k3"]), heads(proj["v3"]), r(proj["small"])[..., :D_IDX],
            h_final.reshape(nb, SSM_HEADS, SSM_HEAD_DIM, D_STATE), conv_state)


def kernel(x_prompt, x_sample, cache_k, cache_v, cache_kidx, state_ssm, state_conv, page_table, g_pre, w_in, conv_w, conv_b, dt_bias, a_log, d_skip, g_ssm, w_out, g_post, rel_bias):
    depth = w_in.shape[0]
    bp, sp, _ = x_prompt.shape
    bs, ts, _ = x_sample.shape
    past = page_table.shape[1] * PAGE_SIZE
    assert ts <= SUBLANES and ts >= CONV_W - 1 and sp % max(TQ_PROMPT, KI3_BUILD_ROWS) == 0
    assert page_table.shape[1] % SCORE_PAGES == 0 and page_table.shape[1] % ATTN_PAGES == 0
    state_rows = SSM_HEADS * SSM_HEAD_DIM
    bias_t = _bias_tiles(rel_bias, TQ_PROMPT)

    yp = x_prompt
    ys = jnp.pad(x_sample, ((0, 0), (0, CHUNK - ts), (0, 0)))
    outs_p, outs_s = [], []
    for l in range(depth):
        lw = _layer_weights(g_pre[l], w_in[l], conv_w[l], conv_b[l], dt_bias[l], a_log[l], d_skip[l], g_ssm[l],
                            w_out[l], g_post[l])

        def prompt_attn(proj):
            return _prompt_attention(proj, bias_t, bp, sp, TQ_PROMPT)

        def sample_attn(proj, layer=l):
            att8 = _sample_attention(page_table, rel_bias, proj, cache_k[layer], cache_v[layer], cache_kidx[layer],
                                     bs, ts)
            return jnp.pad(att8, ((0, 0), (0, CHUNK - SUBLANES), (0, 0))).reshape(bs * CHUNK, ATT_WIDTH)

        op = _mixer(yp, lw, 0, sp, jnp.zeros((bp, state_rows, D_STATE), F32),
                    jnp.zeros((bp, SUBLANES, CONV_CH), F32), prompt_attn)
        c0 = jnp.pad(state_conv[l], ((0, 0), (SUBLANES - (CONV_W - 1), 0), (0, 0)))
        os_ = _mixer(ys, lw, past, ts, state_ssm[l].reshape(bs, state_rows, D_STATE), c0, sample_attn)
        yp = op[0]
        ys = jnp.pad(os_[0], ((0, 0), (0, CHUNK - ts), (0, 0)))
        outs_p.append(op[1:])
        outs_s.append(os_[1:])
    stack = lambda outs, n: jnp.stack([o[n] for o in outs])
    return (yp, ys[:, :ts], *[stack(outs_p, n) for n in range(5)], *[stack(outs_s, n) for n in range(5)])
```

```python
import functools
import math

import jax
import jax.numpy as jnp
import numpy as np
from jax import lax
from jax.experimental import pallas as pl
from jax.experimental.pallas import tpu as pltpu

F32 = jnp.float32
BF16 = jnp.bfloat16
I32 = jnp.int32

D_MODEL = 1024
PAGE_SIZE = 128
HEAD_DIM = 64
ATT_WIDTH = 512
N_HEADS_A = 8
N_IDX_HEADS = 8
D_IDX = 64
IDX_ROPE = 32
ROPE_BASE = 10000.0
TOPK_MAX = 256
NUM_BUCKETS = 32
MAX_DISTANCE = 128
SSM_WIDTH = 512
SSM_HEAD_DIM = 64
SSM_HEADS = 8
SSM_GROUPS = 2
D_STATE = 128
CONV_W = 4
CONV_CH = 1024
CHUNK = 128
EPS = 1e-6

LANES = 128
SUBLANES = 8
BF16_ROWS = 16
VMEM_LIMIT = 56 * 1024 * 1024
NEG = -1e30
INT_MIN = -2 ** 31
LOG2E = 1.4426950408889634

SM_WI = D_IDX
SM_DT = D_IDX + N_IDX_HEADS
ROW_COLS = 3 * ATT_WIDTH + SSM_WIDTH + CONV_CH
QI_WIDTH = N_IDX_HEADS * D_IDX
VP_ROWS = N_HEADS_A * LANES
SPLIT3 = 4 * D_IDX


def _nt(a, b, **kw):
    return lax.dot_general(a, b, (((1,), (1,)), ((), ())), preferred_element_type=F32, **kw)


def _dot(a, b, **kw):
    return jnp.dot(a, b, preferred_element_type=F32, **kw)


def _split(x):
    hi = x.astype(BF16)
    return hi, (x - hi.astype(F32)).astype(BF16)


def _split3(x):
    hi = x.astype(BF16)
    rest = x - hi.astype(F32)
    mid = rest.astype(BF16)
    return hi, mid, (rest - mid.astype(F32)).astype(BF16)


def _sigmoid(x):
    return 1.0 / (1.0 + jnp.exp(-x))


def _cparams(sem):
    return pltpu.CompilerParams(dimension_semantics=sem, vmem_limit_bytes=VMEM_LIMIT)


def _rope_table_kernel(inv_ref, cos_ref, sin_ref, cost_ref, sint_ref, *, pos_off):
    rows = cos_ref.shape[0]
    pos = (lax.broadcasted_iota(I32, (rows, LANES), 0) + pos_off).astype(F32)
    ang = pos * inv_ref[...]
    c = jnp.cos(ang)
    s = jnp.sin(ang)
    cos_ref[...] = c
    sin_ref[...] = s
    cost_ref[...] = c.T
    sint_ref[...] = s.T


def _rope_tables(rows, pos_off):
    inv = ROPE_BASE ** (-jnp.arange(0, IDX_ROPE, 2, dtype=F32) / IDX_ROPE)
    l64 = np.arange(LANES) % D_IDX
    inv_row = jnp.where(l64 < IDX_ROPE, inv[l64 % (IDX_ROPE // 2)], 0.0).astype(F32)[None, :]
    return pl.pallas_call(
        functools.partial(_rope_table_kernel, pos_off=pos_off),
        out_shape=(jax.ShapeDtypeStruct((rows, LANES), F32),) * 2 + (jax.ShapeDtypeStruct((LANES, rows), F32),) * 2,
    )(inv_row)


def _inproj_kernel(x_ref, g_ref, wr_ref, wqt_ref, wqih_ref, wqil_ref, wsh_ref, wsb_ref,
                   cos_ref, sin_ref, cost_ref, sint_ref,
                   qet_ref, qot_ref, k3_ref, kb_ref, v_ref, v3_ref, vpt_ref, ga_ref, z_ref, xbc_ref, qit_ref, sm_ref,
                   smt_ref):
    x = x_ref[...]
    hn = x * lax.rsqrt(jnp.mean(x * x, axis=-1, keepdims=True) + EPS) * g_ref[...]
    hb, hlo = _split(hn)
    tm = x.shape[0]

    def rows(lo, width):
        return _dot(hb, wr_ref[:, lo:lo + width])

    k = rows(0, ATT_WIDTH)
    kb_ref[...] = k.astype(BF16)
    v = rows(ATT_WIDTH, ATT_WIDTH)
    v_ref[...] = v
    for h in range(N_HEADS_A):
        k3_ref[:, h, :] = k[:, h * HEAD_DIM:(h + 1) * HEAD_DIM]
        v3_ref[:, h, :] = v[:, h * HEAD_DIM:(h + 1) * HEAD_DIM]
    ga_ref[...] = rows(2 * ATT_WIDTH, ATT_WIDTH)
    z_ref[...] = rows(3 * ATT_WIDTH, SSM_WIDTH)
    xbc_ref[...] = rows(3 * ATT_WIDTH + SSM_WIDTH, CONV_CH)

    vt = v.T
    ones = jnp.ones((HEAD_DIM, tm), BF16)
    for h in range(N_HEADS_A):
        vpt_ref[h * LANES:h * LANES + HEAD_DIM, :] = vt[h * HEAD_DIM:(h + 1) * HEAD_DIM, :].astype(BF16)
        vpt_ref[h * LANES + HEAD_DIM:(h + 1) * LANES, :] = ones

    qt = _nt(wqt_ref[...], hb) * (HEAD_DIM ** -0.5 * LOG2E)
    even = (lax.broadcasted_iota(I32, qt.shape, 0) & HEAD_DIM) == 0
    qet_ref[...] = jnp.where(even, qt, 0.0).astype(BF16)
    qot_ref[...] = jnp.where(even, 0.0, qt).astype(BF16)

    qit = _nt(wqih_ref[...], hb) + (_nt(wqih_ref[...], hlo) + _nt(wqil_ref[...], hb))
    ct = cost_ref[0:D_IDX, :]
    st = sint_ref[0:D_IDX, :]
    first_t = lax.broadcasted_iota(I32, (D_IDX, tm), 0) < IDX_ROPE // 2
    s1t = jnp.where(first_t, -st, 0.0)
    s2t = jnp.where(first_t, 0.0, st)
    for h in range(N_IDX_HEADS):
        xh = qit[h * D_IDX:(h + 1) * D_IDX, :]
        qit_ref[h * D_IDX:(h + 1) * D_IDX, :] = (xh * ct + pltpu.roll(xh, D_IDX - IDX_ROPE // 2, 0) * s1t
                                                 + pltpu.roll(xh, IDX_ROPE // 2, 0) * s2t)

    both = _dot(hb, wsb_ref[...])
    sm = both[:, :LANES] + (both[:, LANES:] + _dot(hlo, wsh_ref[...]))
    lane = lax.broadcasted_iota(I32, (tm, LANES), 1)
    is_ki = lane < D_IDX
    first = (lane & (D_IDX - 1)) < IDX_ROPE // 2
    c = jnp.where(is_ki, cos_ref[...], 1.0)
    s = jnp.where(is_ki, sin_ref[...], 0.0)
    sm = (sm * c + pltpu.roll(sm, LANES - IDX_ROPE // 2, 1) * jnp.where(first, -s, 0.0)
          + pltpu.roll(sm, IDX_ROPE // 2, 1) * jnp.where(first, 0.0, s))
    is_wi = (lane >= SM_WI) & (lane < SM_DT)
    sm = jnp.where(is_wi, sm * (N_IDX_HEADS ** -0.5), sm)
    sm_ref[...] = sm
    smt_ref[...] = sm.T


def _inproj(x2d, lw, tables, tm):
    n = x2d.shape[0]
    cos_t, sin_t, cos_tt, sin_tt = tables
    tab_blocks = cos_t.shape[0] // tm
    row = lambda i: (i, 0)
    col = lambda i: (0, i)
    const = lambda i: (0, 0)
    rows = lambda w, dt: (jax.ShapeDtypeStruct((n, w), dt), pl.BlockSpec((tm, w), row))
    cols = lambda w, dt: (jax.ShapeDtypeStruct((w, n), dt), pl.BlockSpec((w, tm), col))
    full = lambda a: pl.BlockSpec(a.shape, const)
    heads = (jax.ShapeDtypeStruct((n, N_HEADS_A, HEAD_DIM), F32),
             pl.BlockSpec((tm, N_HEADS_A, HEAD_DIM), lambda i: (i, 0, 0)))
    outs = [cols(ATT_WIDTH, BF16), cols(ATT_WIDTH, BF16), heads, rows(ATT_WIDTH, BF16),
            rows(ATT_WIDTH, F32), heads, cols(VP_ROWS, BF16), rows(ATT_WIDTH, F32), rows(SSM_WIDTH, F32),
            rows(CONV_CH, F32), cols(QI_WIDTH, F32), rows(LANES, F32), cols(LANES, F32)]
    weights = [lw["g_pre"], lw["w_rows"], lw["wq_t"], lw["wqi_t_hi"], lw["wqi_t_lo"], lw["ws_hi"], lw["ws_both"]]
    return pl.pallas_call(
        _inproj_kernel,
        grid=(n // tm,),
        in_specs=[pl.BlockSpec((tm, D_MODEL), row)] + [full(w) for w in weights]
                 + [pl.BlockSpec((tm, LANES), lambda i: (i % tab_blocks, 0))] * 2
                 + [pl.BlockSpec((LANES, tm), lambda i: (0, i % tab_blocks))] * 2,
        out_specs=[o[1] for o in outs],
        out_shape=[o[0] for o in outs],
        compiler_params=_cparams(("arbitrary",)),
    )(x2d, *weights, cos_t, sin_t, cos_tt, sin_tt)


def _bucket(dist):
    max_exact = NUM_BUCKETS // 2
    n = jnp.maximum(dist, 0)
    nf = jnp.maximum(n, max_exact).astype(F32)
    large = max_exact + jnp.floor(jnp.log(nf / max_exact) / math.log(MAX_DISTANCE / max_exact)
                                  * (NUM_BUCKETS - max_exact)).astype(I32)
    large = jnp.minimum(large, NUM_BUCKETS - 1)
    return jnp.where(n < max_exact, n, large)


def _bias_lookup(bucket, relb_ref, h):
    out = jnp.full(bucket.shape, relb_ref[0, h], F32)
    for b in range(1, NUM_BUCKETS):
        out = jnp.where(bucket == b, relb_ref[b, h], out)
    return out


KEY_NEG_INF = INT_MIN + 0x7FFFFF


def _bit(n):
    return lax.shift_left(jnp.int32(1), jnp.asarray(n, I32))


def _key_to_float(key):
    return pltpu.bitcast(jnp.where(key < 0, key ^ 0x7FFFFFFF, key), F32)


def _search_threshold(count_ge, n_top, n_keys, shape):
    def body(it, thr):
        cand = thr ^ _bit(31 - it)
        cnt = jnp.where(cand < KEY_NEG_INF, jnp.asarray(n_keys, F32), count_ge(_key_to_float(cand)))
        return jnp.where(cnt >= float(n_top), cand, thr)

    return _key_to_float(lax.fori_loop(0, 32, body, jnp.full(shape, INT_MIN, I32)))


def _bf16_step_bits(u):
    b = u - 32768
    return lax.shift_left(jnp.where(b < 0, b ^ 0x7FFF, b), jnp.int32(16))


def _float_image(bits):
    return jnp.where(bits < 0, bits ^ 0x7FFFFFFF, bits)


def _search_threshold_2level(count_ge_rounded, count_ge, n_top, n_keys, shape):
    k = float(n_top)
    n_all = jnp.asarray(n_keys, F32)

    def admits(u, counter):
        bits = _bf16_step_bits(u)
        return jnp.where(_float_image(bits) < KEY_NEG_INF, n_all, counter(pltpu.bitcast(bits, F32))) >= k

    def coarse(it, u):
        cand = u | _bit(15 - it)
        return jnp.where(admits(cand, count_ge_rounded), cand, u)

    u = lax.fori_loop(0, 16, coarse, jnp.zeros(shape, I32))
    u = jnp.where(admits(u, count_ge), u, u - 1)
    base = _float_image(_bf16_step_bits(u))

    def fine(it, d):
        cand = d | _bit(15 - it)
        return jnp.where(count_ge(_key_to_float(base + cand)) >= k, cand, d)

    return _key_to_float(base + lax.fori_loop(0, 16, fine, jnp.zeros(shape, I32)))


def _search_last_tie(count_ties_before, need, pos_bits, shape):
    def body(it, q):
        cand = q | _bit(pos_bits - 1 - it)
        return jnp.where(count_ties_before(cand) < need, cand, q)

    return lax.fori_loop(0, pos_bits, body, jnp.zeros(shape, I32))


def _select_madd(score, pos, thr, last):
    return jnp.where(score > thr, 0.0, jnp.where(score == thr, jnp.where(pos <= last, 0.0, NEG), NEG))


def _bias_tiles_kernel(relb_ref, o_ref, *, tq):
    ki = lax.broadcasted_iota(I32, (tq, tq), 0)
    qi = lax.broadcasted_iota(I32, (tq, tq), 1)
    for kind in range(2):
        bucket = _bucket(qi - ki + kind * tq)
        for h in range(N_HEADS_A):
            o_ref[h, kind] = (_bias_lookup(bucket, relb_ref, h) - relb_ref[NUM_BUCKETS - 1, h]) * LOG2E


def _bias_tiles(rel_bias, tq):
    return pl.pallas_call(
        functools.partial(_bias_tiles_kernel, tq=tq),
        in_specs=[pl.BlockSpec(memory_space=pltpu.SMEM)],
        out_shape=jax.ShapeDtypeStruct((N_HEADS_A, 2, tq, tq), F32),
        compiler_params=pltpu.CompilerParams(vmem_limit_bytes=VMEM_LIMIT),
    )(rel_bias)


KI3_BUILD_ROWS = 512


def _pattn_kernel(qit_ref, smtq_ref, sm_ref, qet_ref, qot_ref, kb_ref, vpt_ref, bt_ref, o_ref,
                  ki3_ref, qh3_ref, sc_ref, hi_ref, last_ref, m_ref, acc_ref, *, tq, n_top, pos_bits):
    i = pl.program_id(1)
    nch = i + 1
    s_len = sc_ref.shape[0]
    kiota = lax.broadcasted_iota(I32, (tq, tq), 0)
    qpos = i * tq + lax.broadcasted_iota(I32, (tq, tq), 1)

    def rows(c, width=tq):
        return pl.ds(pl.multiple_of(c * tq, tq), width)

    @pl.when(i == 0)
    def _():
        low = lax.broadcasted_iota(I32, (KI3_BUILD_ROWS, LANES), 1) < D_IDX

        def body(r, carry):
            sl = pl.ds(pl.multiple_of(r * KI3_BUILD_ROWS, KI3_BUILD_ROWS), KI3_BUILD_ROWS)
            x = sm_ref[sl, :]
            hi = x.astype(BF16).astype(F32)
            ki3_ref[sl, 0:LANES] = jnp.where(low, hi, pltpu.roll(hi, D_IDX, 1)).astype(BF16)
            ki3_ref[sl, LANES:2 * LANES] = jnp.where(low, x - hi, 0.0).astype(BF16)
            return carry
        lax.fori_loop(0, s_len // KI3_BUILD_ROWS, body, 0)

    for h in range(N_IDX_HEADS):
        hi, lo = _split(qit_ref[h * D_IDX:(h + 1) * D_IDX, :])
        qh3_ref[h, 0:D_IDX, :] = hi
        qh3_ref[h, D_IDX:2 * D_IDX, :] = lo
        qh3_ref[h, 2 * D_IDX:3 * D_IDX, :] = hi
        qh3_ref[h, 3 * D_IDX:, :] = jnp.zeros((D_IDX, tq), BF16)
    w8 = smtq_ref[SM_WI:SM_WI + N_IDX_HEADS, :] * (D_IDX ** -0.5)

    def score_body(c2, carry):
        blocks = [2 * c2, jnp.minimum(2 * c2 + 1, i)]
        dots = [[_dot(ki3_ref[rows(c), :], qh3_ref[h]) for h in range(N_IDX_HEADS)] for c in blocks]
        for c, d in zip(blocks, dots):
            terms = [jnp.maximum(d[h], 0.0) * w8[h:h + 1, :] for h in range(N_IDX_HEADS)]
            while len(terms) > 1:
                terms = [terms[j] + terms[j + 1] for j in range(0, len(terms), 2)]
            sc = jnp.where(c * tq + kiota <= qpos, terms[0], -jnp.inf)
            sc_ref[rows(c), :] = sc
            hi_ref[rows(c), :] = sc.astype(BF16)
        return carry

    lax.fori_loop(0, (nch + 1) // 2, score_body, 0)

    def over_keys(x, op):
        x = x.reshape(x.shape[0] // SUBLANES, SUBLANES, tq)
        while x.shape[0] > 1:
            half = x.shape[0] // 2
            x = op(x[:half], x[half:])
        return x[0]

    def count(pred):
        def body(c, acc):
            hit = jnp.where(pred(sc_ref[rows(c), :], c * tq + kiota), 1.0, 0.0)
            return acc + over_keys(hit, jnp.add)
        acc = lax.fori_loop(0, nch, body, jnp.zeros((SUBLANES, tq), F32))
        return jnp.broadcast_to(jnp.sum(acc, axis=0, keepdims=True), (SUBLANES, tq))

    def count_rounded(t):
        t16 = jnp.concatenate([t, t], axis=0).astype(BF16)

        def body(c, acc):
            h = hi_ref[rows(c), :].reshape(tq // BF16_ROWS, BF16_ROWS, tq)
            hit = jnp.where(h >= t16[None], jnp.ones_like(h), jnp.zeros_like(h))
            while hit.shape[0] > 1:
                half = hit.shape[0] // 2
                hit = hit[:half] + hit[half:]
            part = hit[0].astype(F32)
            return acc + (part[0:SUBLANES] + part[SUBLANES:])
        acc = lax.fori_loop(0, nch, body, jnp.zeros((SUBLANES, tq), F32))
        return jnp.broadcast_to(jnp.sum(acc, axis=0, keepdims=True), (SUBLANES, tq))

    thr = _search_threshold_2level(count_rounded, lambda t: count(lambda sc, pos: sc >= t[0:1, :]),
                                   n_top, nch * tq, (SUBLANES, tq))
    thr_row = thr[0:1, :]
    kept = count(lambda sc, pos: sc >= thr_row)
    last_ref[...] = jnp.full((SUBLANES, tq), 2 ** pos_bits - 1, I32)

    @pl.when(jnp.max(kept) > float(n_top))
    def _():
        need = float(n_top) - count(lambda sc, pos: sc > thr_row)
        last_ref[...] = _search_last_tie(
            lambda q: count(lambda sc, pos: (sc == thr_row) & (pos < q[0:1, :])), need, pos_bits, (SUBLANES, tq))

    last_row = last_ref[0:1, :]

    def madd_body(c, carry):
        pos = c * tq + kiota
        madd = _select_madd(sc_ref[rows(c), :], pos, thr_row, last_row)
        sc_ref[rows(c), :] = jnp.where(pos <= qpos, madd, NEG)
        return carry

    lax.fori_loop(0, nch, madd_body, 0)

    m_ref[...] = jnp.full(m_ref.shape, NEG, F32)
    acc_ref[...] = jnp.zeros(acc_ref.shape, F32)

    def attend(c0, width, bias_of_head):
        sl = rows(c0, width)
        madd = sc_ref[sl, :]
        logits = []
        for h in range(N_HEADS_A):
            p2 = h // 2
            qt = (qet_ref if h % 2 == 0 else qot_ref)[p2 * LANES:(p2 + 1) * LANES, :]
            logits.append(_dot(kb_ref[sl, p2 * LANES:(p2 + 1) * LANES], qt))
        probs, alphas = [], []
        for h in range(N_HEADS_A):
            s = logits[h] + madd
            if bias_of_head is not None:
                s = s + bias_of_head(h)
            m_prev = m_ref[h]
            cmax = over_keys(s, jnp.maximum)
            m_new = jnp.maximum(m_prev, jnp.broadcast_to(jnp.max(cmax, axis=0, keepdims=True), (SUBLANES, tq)))
            probs.append(jnp.exp2(s - m_new[0:1, :]).astype(BF16))
            alphas.append(jnp.exp2(m_prev - m_new)[0:1, :])
            m_ref[h] = m_new
        for h in range(N_HEADS_A):
            pv = _dot(vpt_ref[h * LANES:(h + 1) * LANES, sl], probs[h])
            acc_ref[h] = alphas[h] * acc_ref[h] + pv

    n_far = jnp.maximum(i - 1, 0)

    def far_body(c, carry):
        attend(2 * c, 2 * tq, None)
        return carry

    lax.fori_loop(0, n_far // 2, far_body, 0)

    @pl.when(n_far % 2 == 1)
    def _():
        attend(n_far - 1, tq, None)

    @pl.when(i >= 1)
    def _():
        attend(i - 1, tq, lambda h: bt_ref[h, 1])

    attend(i, tq, lambda h: bt_ref[h, 0])

    outs = []
    for h in range(N_HEADS_A):
        acc = acc_ref[h]
        outs.append(acc[0:HEAD_DIM, :] / acc[HEAD_DIM:, :])
    o_ref[...] = jnp.concatenate(outs, axis=0).T


def _prompt_attention(proj, bias_t, nb, s, tq):
    nq = s // tq
    n_top = min(TOPK_MAX, s // 4)
    pos_bits = max(1, (s - 1).bit_length())
    qcols = lambda b, i: (0, b * nq + i)
    seq_rows = lambda b, i: (b, 0)
    seq_cols = lambda b, i: (0, b)
    return pl.pallas_call(
        functools.partial(_pattn_kernel, tq=tq, n_top=n_top, pos_bits=pos_bits),
        grid=(nb, nq),
        in_specs=[pl.BlockSpec((QI_WIDTH, tq), qcols), pl.BlockSpec((LANES, tq), qcols),
                  pl.BlockSpec((s, LANES), seq_rows),
                  pl.BlockSpec((ATT_WIDTH, tq), qcols), pl.BlockSpec((ATT_WIDTH, tq), qcols),
                  pl.BlockSpec((s, ATT_WIDTH), seq_rows), pl.BlockSpec((VP_ROWS, s), seq_cols),
                  pl.BlockSpec((N_HEADS_A, 2, tq, tq), lambda b, i: (0, 0, 0, 0),
                               pipeline_mode=pl.Buffered(1))],
        out_specs=pl.BlockSpec((tq, ATT_WIDTH), lambda b, i: (b * nq + i, 0)),
        out_shape=jax.ShapeDtypeStruct((nb * s, ATT_WIDTH), F32),
        scratch_shapes=[pltpu.VMEM((s, SPLIT3), BF16), pltpu.VMEM((N_IDX_HEADS, SPLIT3, tq), BF16),
                        pltpu.VMEM((s, tq), F32), pltpu.VMEM((s, tq), BF16), pltpu.VMEM((SUBLANES, tq), I32),
                        pltpu.VMEM((N_HEADS_A, SUBLANES, tq), F32), pltpu.VMEM((N_HEADS_A, LANES, tq), F32)],
        compiler_params=_cparams(("arbitrary", "arbitrary")),
    )(proj["qi_t"], proj["small_t"], proj["small"], proj["qe_t"], proj["qo_t"], proj["k_b"], proj["vp_t"], bias_t)


def _ssd_kernel(xbc_ref, z_ref, sm_ref, cw_ref, cb_ref, dtb_ref, alog_ref, dsk_ref, gs_ref, ex_ref, ext_ref,
                h0_ref, c0_ref, att_ref, ga_ref, x_ref, wt_ref, wb_ref, gp_ref, y_ref, hf_ref, xp_ref, st_ref, *,
                t_valid):
    c = pl.program_id(1)
    L = CHUNK

    @pl.when(c == 0)
    def _():
        st_ref[...] = h0_ref[...]
        xp_ref[0:SUBLANES, :] = c0_ref[...]

    xp_ref[SUBLANES:SUBLANES + L, :] = xbc_ref[...]
    conv = cb_ref[...]
    for j in range(CONV_W):
        lo = SUBLANES - (CONV_W - 1) + j
        conv = conv + xp_ref[lo:lo + L, :] * cw_ref[j:j + 1, :]
    xp_ref[0:SUBLANES, :] = xp_ref[L:L + SUBLANES, :]
    act = conv * _sigmoid(conv)
    xs = act[:, :SSM_WIDTH]
    bm = act[:, SSM_WIDTH:SSM_WIDTH + SSM_GROUPS * D_STATE]
    cm = act[:, SSM_WIDTH + SSM_GROUPS * D_STATE:]

    raw = sm_ref[...] + dtb_ref[...]
    dtf = jnp.maximum(raw, 0.0) + jnp.log1p(jnp.exp(-jnp.abs(raw)))
    row = lax.broadcasted_iota(I32, (L, LANES), 0)
    if t_valid < L:
        dtf = jnp.where(row < t_valid, dtf, 0.0)
    adt = dtf * (-jnp.exp(alog_ref[...]))
    tril = row >= lax.broadcasted_iota(I32, (L, LANES), 1)
    tril01 = jnp.where(tril, 1.0, 0.0).astype(BF16)
    cs = sum(_dot(tril01, part) for part in _split3(adt))
    ex = ex_ref[...]
    dtx = sum(_dot(part, ex) for part in _split3(dtf))
    csx = sum(_dot(part, ex) for part in _split3(cs))
    cst = cs.T
    x = xs * dtx
    w = x * jnp.exp(csx[L - 1:L, :] - csx)
    ecsx = jnp.exp(csx)
    dec = jnp.exp(jnp.sum(ext_ref[...] * cs[L - 1:L, :], axis=1, keepdims=True))
    low = lax.broadcasted_iota(I32, (L, LANES), 1) < SSM_HEAD_DIM

    ys = []
    for p2 in range(SSM_HEADS // 2):
        g = (2 * p2) // (SSM_HEADS // SSM_GROUPS)
        cg = cm[:, g * D_STATE:(g + 1) * D_STATE].astype(BF16)
        bg = bm[:, g * D_STATE:(g + 1) * D_STATE].astype(BF16)
        cb_mat = _nt(cg, bg)
        lanes = slice(p2 * LANES, (p2 + 1) * LANES)
        xp = x[:, lanes].astype(BF16)
        yd = []
        for h in (2 * p2, 2 * p2 + 1):
            diff = cs[:, SM_DT + h:SM_DT + h + 1] - cst[SM_DT + h:SM_DT + h + 1, :]
            lm = jnp.exp(jnp.where(tril, diff, NEG))
            yd.append(_dot((cb_mat * lm).astype(BF16), xp))
        rows = slice(p2 * LANES, (p2 + 1) * LANES)
        st = st_ref[rows, :]
        y_off = _nt(cg, st.astype(BF16)) * ecsx[:, lanes]
        ys.append(jnp.where(low, yd[0], yd[1]) + y_off)
        upd = _dot(w[:, lanes].T.astype(BF16), bg)
        st_ref[rows, :] = st * dec[rows, :] + upd

    y = jnp.concatenate(ys, axis=1) + dsk_ref[...] * xs
    zz = z_ref[...]
    gated = y * (zz * _sigmoid(zz))
    ssm = gated * lax.rsqrt(jnp.mean(gated * gated, axis=-1, keepdims=True) + EPS) * gs_ref[...]

    ga = ga_ref[...]
    att = att_ref[...] * (ga * _sigmoid(ga))
    out = _dot(att.astype(BF16), wt_ref[...]) + _dot(ssm.astype(BF16), wb_ref[...])
    y_ref[...] = x_ref[...] + out * lax.rsqrt(jnp.mean(out * out, axis=-1, keepdims=True) + EPS) * gp_ref[...]

    @pl.when(c == pl.num_programs(1) - 1)
    def _():
        hf_ref[...] = st_ref[...]


def _ssd_out(proj, att, x2d, lw, h0, c0, nb, s, t_valid):
    xbc, z, small, ga = proj["xbc"], proj["z"], proj["small"], proj["ga"]
    nc = s // CHUNK
    blk = lambda b, c: (b * nc + c, 0)
    const = lambda b, c: (0, 0)
    per_b = lambda b, c: (b, 0, 0)
    lanes = np.arange(LANES)
    dt_row = lambda v: jnp.zeros((1, LANES), F32).at[0, SM_DT:SM_DT + SSM_HEADS].set(v)
    expand = (lanes[:, None] == SM_DT + np.arange(SSM_WIDTH)[None, :] // SSM_HEAD_DIM).astype(np.float32)
    state_rows = SSM_HEADS * SSM_HEAD_DIM
    return pl.pallas_call(
        functools.partial(_ssd_kernel, t_valid=t_valid),
        grid=(nb, nc),
        in_specs=[pl.BlockSpec((CHUNK, CONV_CH), blk), pl.BlockSpec((CHUNK, SSM_WIDTH), blk),
                  pl.BlockSpec((CHUNK, LANES), blk),
                  pl.BlockSpec((CONV_W, CONV_CH), const), pl.BlockSpec((1, CONV_CH), const),
                  pl.BlockSpec((1, LANES), const), pl.BlockSpec((1, LANES), const),
                  pl.BlockSpec((1, SSM_WIDTH), const), pl.BlockSpec((1, SSM_WIDTH), const),
                  pl.BlockSpec((LANES, SSM_WIDTH), const), pl.BlockSpec((SSM_WIDTH, LANES), const),
                  pl.BlockSpec((None, state_rows, D_STATE), per_b),
                  pl.BlockSpec((None, SUBLANES, CONV_CH), per_b),
                  pl.BlockSpec((CHUNK, ATT_WIDTH), blk), pl.BlockSpec((CHUNK, ATT_WIDTH), blk),
                  pl.BlockSpec((CHUNK, D_MODEL), blk),
                  pl.BlockSpec((ATT_WIDTH, D_MODEL), const), pl.BlockSpec((SSM_WIDTH, D_MODEL), const),
                  pl.BlockSpec((1, D_MODEL), const)],
        out_specs=[pl.BlockSpec((CHUNK, D_MODEL), blk), pl.BlockSpec((None, state_rows, D_STATE), per_b)],
        out_shape=[jax.ShapeDtypeStruct((nb * s, D_MODEL), F32),
                   jax.ShapeDtypeStruct((nb, state_rows, D_STATE), F32)],
        scratch_shapes=[pltpu.VMEM((CHUNK + SUBLANES, CONV_CH), F32), pltpu.VMEM((state_rows, D_STATE), F32)],
        compiler_params=_cparams(("arbitrary", "arbitrary")),
    )(xbc, z, small, lw["conv_w"], lw["conv_b"][None, :], dt_row(lw["dt_bias"]), dt_row(lw["a_log"]),
      jnp.repeat(lw["d_skip"], SSM_HEAD_DIM)[None, :], lw["g_ssm"][None, :], jnp.asarray(expand, BF16),
      jnp.asarray(expand.T), h0, c0, att, ga, x2d, lw["w_top"], lw["w_bot"], lw["g_post"])


SCORE_PAGES = 32
ATTN_PAGES = 16
ROWS_Q = N_HEADS_A * SUBLANES
COUNT_CHAINS = 4
SELECT_GROUP = 4


def _sscore_kernel(pt_ref, qall3_ref, wcol_ref, smt_ref, *rest, past, t_new, n_top, pos_bits, group):
    pages = rest[:SCORE_PAGES]
    madd_ref = rest[SCORE_PAGES]
    sc_ref, last_ref = rest[SCORE_PAGES + 1:]
    j = pl.program_id(1)
    member = pl.program_id(0) % group
    mine = pl.ds(pl.multiple_of(member * SUBLANES, SUBLANES), SUBLANES)
    rows_g = group * SUBLANES
    kw = SCORE_PAGES * PAGE_SIZE
    total = past + LANES
    qall3 = qall3_ref[...]
    wcol = wcol_ref[...]
    zeros = jnp.zeros((D_IDX, PAGE_SIZE), BF16)

    def dots(kt):
        hi, lo = _split(kt)
        return _dot(qall3, jnp.concatenate([hi, hi, lo, zeros], axis=0))

    def weigh(d):
        r = jnp.maximum(d * (D_IDX ** -0.5), 0.0) * wcol
        sc = r[0:SUBLANES]
        for h in range(1, N_IDX_HEADS):
            sc = sc + r[h * SUBLANES:(h + 1) * SUBLANES]
        return sc

    def scores(kt):
        return weigh(dots(kt))

    page_dots = [dots(page[...]) for page in pages]
    for r, d in enumerate(page_dots):
        sl = pl.ds(pl.multiple_of(j * kw + r * PAGE_SIZE, PAGE_SIZE), PAGE_SIZE)
        sc_ref[mine, sl] = weigh(d)

    @pl.when(j == pl.num_programs(1) - 1)
    def _():
        lane8 = lax.broadcasted_iota(I32, (SUBLANES, LANES), 1)
        row8 = lax.broadcasted_iota(I32, (SUBLANES, LANES), 0)
        vis8 = (lane8 <= row8) & (lane8 < t_new)
        sc_ref[mine, past:total] = jnp.where(vis8, scores(smt_ref[0:D_IDX, :]), -jnp.inf)

    @pl.when(jnp.logical_and(j == pl.num_programs(1) - 1, member == group - 1))
    def _():
        shape = (rows_g, LANES)
        lane = lax.broadcasted_iota(I32, shape, 1)
        tok = lax.broadcasted_iota(I32, shape, 0) % SUBLANES
        vis = (lane <= tok) & (lane < t_new)

        def count(pred):
            accs = [jnp.zeros(shape, F32)] * COUNT_CHAINS
            for t in range(total // LANES):
                hit = pred(sc_ref[:, t * LANES:(t + 1) * LANES], t * LANES + lane)
                accs[t % COUNT_CHAINS] = accs[t % COUNT_CHAINS] + jnp.where(hit, 1.0, 0.0)
            while len(accs) > 1:
                accs = [accs[n] + accs[n + 1] for n in range(0, len(accs), 2)]
            return jnp.broadcast_to(jnp.sum(accs[0], axis=1, keepdims=True), shape)

        thr = _search_threshold(lambda t: count(lambda sc, pos: sc >= t), n_top, total, shape)
        kept = count(lambda sc, pos: sc >= thr)
        last_ref[...] = jnp.full(shape, 2 ** pos_bits - 1, I32)

        @pl.when(jnp.max(kept) > float(n_top))
        def _():
            need = float(n_top) - count(lambda sc, pos: sc > thr)
            last_ref[...] = _search_last_tie(
                lambda q: count(lambda sc, pos: (sc == thr) & (pos < q)), need, pos_bits, shape)

        last = last_ref[...]
        for t in range(total // LANES):
            sl = slice(t * LANES, (t + 1) * LANES)
            pos = t * LANES + lane
            madd = _select_madd(sc_ref[:, sl], pos, thr, last)
            if t * LANES >= past:
                madd = jnp.where(vis, madd, NEG)
            madd_ref[:, sl] = madd


def _sattn_kernel(pt_ref, relb_ref, qbd_ref, madd_ref, maddn_ref, kbnew_ref, vnew_ref, *rest, past):
    kpages = rest[:ATTN_PAGES]
    vpages = rest[ATTN_PAGES:2 * ATTN_PAGES]
    o_ref, m_ref, l_ref, acc_ref = rest[2 * ATTN_PAGES:]
    j = pl.program_id(1)
    kw = ATTN_PAGES * PAGE_SIZE

    @pl.when(j == 0)
    def _():
        m_ref[...] = jnp.full(m_ref.shape, NEG, F32)
        l_ref[...] = jnp.zeros(l_ref.shape, F32)
        acc_ref[...] = jnp.zeros(acc_ref.shape, F32)

    qbd = qbd_ref[...]

    def far_bias(width):
        row_head = lax.broadcasted_iota(I32, (ROWS_Q, width), 0) // SUBLANES
        out = jnp.full((ROWS_Q, width), relb_ref[NUM_BUCKETS - 1, 0], F32)
        for h in range(1, N_HEADS_A):
            out = jnp.where(row_head == h, relb_ref[NUM_BUCKETS - 1, h], out)
        return out

    def near_bias(width, pos0):
        tok = lax.broadcasted_iota(I32, (SUBLANES, width), 0)
        pos = pos0 + lax.broadcasted_iota(I32, (SUBLANES, width), 1)
        bucket = _bucket(past + tok - pos)
        return jnp.concatenate([_bias_lookup(bucket, relb_ref, h) for h in range(N_HEADS_A)], axis=0)

    def update(logits, bias, madd8, pv_fn):
        s = logits + (bias * LOG2E + jnp.concatenate([madd8] * N_HEADS_A, axis=0))
        m_prev = m_ref[...]
        m_new = jnp.maximum(m_prev, jnp.broadcast_to(jnp.max(s, axis=1, keepdims=True), (ROWS_Q, LANES)))
        alpha = jnp.exp2(m_prev - m_new)
        p = jnp.exp2(s - m_new[:, :1])
        l_ref[...] = alpha * l_ref[...] + jnp.broadcast_to(jnp.sum(p, axis=1, keepdims=True), (ROWS_Q, LANES))
        acc_ref[...] = alpha[:, :1] * acc_ref[...] + pv_fn(p.astype(BF16))
        m_ref[...] = m_new

    def paged(bias):
        logits = jnp.concatenate([_dot(qbd, kp[...].astype(BF16)) for kp in kpages], axis=1)

        def pv_fn(p):
            pv = _nt(p[:, 0:PAGE_SIZE], vpages[0][...].astype(BF16))
            for r in range(1, ATTN_PAGES):
                pv = pv + _nt(p[:, r * PAGE_SIZE:(r + 1) * PAGE_SIZE], vpages[r][...].astype(BF16))
            return pv

        update(logits, bias, madd_ref[...], pv_fn)

    far = (j + 1) * kw + MAX_DISTANCE <= past + 1

    @pl.when(far)
    def _():
        paged(far_bias(kw))

    @pl.when(jnp.logical_not(far))
    def _():
        paged(near_bias(kw, j * kw))

    @pl.when(j == pl.num_programs(1) - 1)
    def _():
        update(_nt(qbd, kbnew_ref[...]), near_bias(LANES, past), maddn_ref[...],
               lambda p: _dot(p, vnew_ref[...].astype(BF16)))
        o = acc_ref[...] / l_ref[:, :1]
        own = (lax.broadcasted_iota(I32, (ROWS_Q, ATT_WIDTH), 0) // SUBLANES
               == lax.broadcasted_iota(I32, (ROWS_Q, ATT_WIDTH), 1) // HEAD_DIM)
        o = jnp.where(own, o, 0.0)
        out = o[0:SUBLANES]
        for h in range(1, N_HEADS_A):
            out = out + o[h * SUBLANES:(h + 1) * SUBLANES]
        o_ref[...] = out


def _sample_attention(page_table, rel_bias, proj, cache_k, cache_v, cache_kidx, nb, t_new):
    n_pages = page_table.shape[1]
    past = n_pages * PAGE_SIZE
    total = past + LANES
    n_top = min(TOPK_MAX, (past + t_new) // 4)
    pos_bits = max(1, (total - 1).bit_length())
    pt = page_table.reshape(-1)
    pool = cache_kidx.shape[0]

    def tok(a_t):
        return a_t.reshape(a_t.shape[0], nb, CHUNK)[:, :, :SUBLANES].transpose(1, 2, 0)

    by_head = lambda a, d: a.reshape(nb, SUBLANES, -1, d).transpose(0, 2, 1, 3)
    qall = by_head(tok(proj["qi_t"]), D_IDX).reshape(nb, ROWS_Q, D_IDX)
    hi, lo = _split(qall)
    qall3 = jnp.concatenate([hi, lo, hi, jnp.zeros_like(hi)], axis=-1)
    wi_t = tok(proj["small_t"])[:, :, SM_WI:SM_WI + N_IDX_HEADS].transpose(0, 2, 1).reshape(nb, ROWS_Q, 1)
    wcol = jnp.broadcast_to(wi_t, (nb, ROWS_Q, LANES))
    q_t = by_head(tok(proj["qe_t"] + proj["qo_t"]), HEAD_DIM)
    eye = jnp.eye(N_HEADS_A, dtype=q_t.dtype)
    qbd = (q_t[:, :, :, None, :] * eye[None, :, None, :, None]).reshape(nb, ROWS_Q, ATT_WIDTH)
    ckt = cache_k.transpose(0, 2, 3, 1).reshape(pool, ATT_WIDTH, PAGE_SIZE)
    cvt = cache_v.transpose(0, 2, 3, 1).reshape(pool, ATT_WIDTH, PAGE_SIZE)
    cit = cache_kidx.transpose(0, 2, 1)

    def page(r, per_step):
        return lambda b, j, pt_ref: (pt_ref[b * n_pages + j * per_step + r], 0, 0)

    seq3 = lambda b, j, pt_ref: (b, 0, 0)
    tcol = lambda b, j, pt_ref: (0, b)
    trow = lambda b, j, pt_ref: (b, 0)

    group = math.gcd(nb, SELECT_GROUP)
    madd = pl.pallas_call(
        functools.partial(_sscore_kernel, past=past, t_new=t_new, n_top=n_top, pos_bits=pos_bits, group=group),
        grid_spec=pltpu.PrefetchScalarGridSpec(
            num_scalar_prefetch=1, grid=(nb, n_pages // SCORE_PAGES),
            in_specs=[pl.BlockSpec((None, ROWS_Q, SPLIT3), seq3), pl.BlockSpec((None, ROWS_Q, LANES), seq3),
                      pl.BlockSpec((LANES, CHUNK), tcol)]
                     + [pl.BlockSpec((None, D_IDX, PAGE_SIZE), page(r, SCORE_PAGES)) for r in range(SCORE_PAGES)],
            out_specs=pl.BlockSpec((None, group * SUBLANES, total), lambda b, j, pt_ref: (b // group, 0, 0)),
            scratch_shapes=[pltpu.VMEM((group * SUBLANES, total), F32),
                            pltpu.VMEM((group * SUBLANES, LANES), I32)]),
        out_shape=jax.ShapeDtypeStruct((nb // group, group * SUBLANES, total), F32),
        compiler_params=_cparams(("arbitrary", "arbitrary")),
    )(pt, qall3, wcol, proj["small_t"], *([cit] * SCORE_PAGES)).reshape(nb, SUBLANES, total)

    kw = ATTN_PAGES * PAGE_SIZE
    kv_specs = [pl.BlockSpec((None, ATT_WIDTH, PAGE_SIZE), page(r, ATTN_PAGES)) for r in range(ATTN_PAGES)]
    return pl.pallas_call(
        functools.partial(_sattn_kernel, past=past),
        grid_spec=pltpu.PrefetchScalarGridSpec(
            num_scalar_prefetch=1, grid=(nb, n_pages // ATTN_PAGES),
            in_specs=[pl.BlockSpec(memory_space=pltpu.SMEM),
                      pl.BlockSpec((None, ROWS_Q, ATT_WIDTH), seq3),
                      pl.BlockSpec((None, SUBLANES, kw), lambda b, j, pt_ref: (b, 0, j)),
                      pl.BlockSpec((None, SUBLANES, LANES), lambda b, j, pt_ref: (b, 0, past // LANES)),
                      pl.BlockSpec((CHUNK, ATT_WIDTH), trow), pl.BlockSpec((CHUNK, ATT_WIDTH), trow)]
                     + kv_specs * 2,
            out_specs=pl.BlockSpec((None, SUBLANES, ATT_WIDTH), seq3),
            scratch_shapes=[pltpu.VMEM((ROWS_Q, LANES), F32), pltpu.VMEM((ROWS_Q, LANES), F32),
                            pltpu.VMEM((ROWS_Q, ATT_WIDTH), F32)]),
        out_shape=jax.ShapeDtypeStruct((nb, SUBLANES, ATT_WIDTH), F32),
        compiler_params=_cparams(("arbitrary", "arbitrary")),
    )(pt, rel_bias, qbd, madd, madd, proj["k_b"], proj["v"], *([ckt] * ATTN_PAGES), *([cvt] * ATTN_PAGES))


TM_PROJ = 256
TQ_PROMPT = 256
PROJ_NAMES = ("qe_t", "qo_t", "k3", "k_b", "v", "v3", "vp_t", "ga", "z", "xbc", "qi_t", "small", "small_t")


def _layer_weights(g_pre, w_in, conv_w, conv_b, dt_bias, a_log, d_skip, g_ssm, w_out, g_post):
    offs = np.cumsum([0, ATT_WIDTH, ATT_WIDTH, ATT_WIDTH, ATT_WIDTH, QI_WIDTH, D_IDX, N_IDX_HEADS,
                      SSM_WIDTH, CONV_CH, SSM_HEADS])
    q, k, v, ga, qi, ki, wi, z, xbc, dt = [w_in[:, offs[n]:offs[n + 1]] for n in range(10)]
    pad = jnp.zeros((D_MODEL, LANES - D_IDX - N_IDX_HEADS - SSM_HEADS), F32)
    wqi_hi, wqi_lo = _split(qi.T)
    ws_hi, ws_lo = _split(jnp.concatenate([ki, wi, dt, pad], axis=1))
    return dict(g_pre=g_pre[None, :], w_rows=jnp.concatenate([k, v, ga, z, xbc], axis=1).astype(BF16),
                wq_t=q.T.astype(BF16), wqi_t_hi=wqi_hi, wqi_t_lo=wqi_lo, ws_hi=ws_hi,
                ws_both=jnp.concatenate([ws_hi, ws_lo], axis=1),
                conv_w=conv_w, conv_b=conv_b, dt_bias=dt_bias, a_log=a_log, d_skip=d_skip, g_ssm=g_ssm,
                w_top=w_out[:ATT_WIDTH].astype(BF16), w_bot=w_out[ATT_WIDTH:].astype(BF16), g_post=g_post[None, :])


def _mixer(x, lw, pos_off, t_valid, h0, c0, attn_fn):
    nb, s, _ = x.shape
    tm = min(TM_PROJ, s)
    x2d = x.reshape(nb * s, D_MODEL)
    proj = dict(zip(PROJ_NAMES, _inproj(x2d, lw, _rope_tables(s, pos_off), tm)))
    att = attn_fn(proj)
    y, h_final = _ssd_out(proj, att, x2d, lw, h0, c0, nb, s, t_valid)
    r = lambda a: a.reshape(nb, s, -1)[:, :t_valid]
    heads = lambda a: a.reshape(nb, s, N_HEADS_A, HEAD_DIM)[:, :t_valid]
    conv_state = r(proj["xbc"])[:, t_valid - (CONV_W - 1):]
    return (r(y), heads(proj["k3"]), heads(proj["v3"]), r(proj["small"])[..., :D_IDX],
            h_final.reshape(nb, SSM_HEADS, SSM_HEAD_DIM, D_STATE), conv_state)


def kernel(x_prompt, x_sample, cache_k, cache_v, cache_kidx, state_ssm, state_conv, page_table, g_pre, w_in, conv_w, conv_b, dt_bias, a_log, d_skip, g_ssm, w_out, g_post, rel_bias):
    depth = w_in.shape[0]
    bp, sp, _ = x_prompt.shape
    bs, ts, _ = x_sample.shape
    past = page_table.shape[1] * PAGE_SIZE
    assert ts <= SUBLANES and ts >= CONV_W - 1 and sp % max(TQ_PROMPT, KI3_BUILD_ROWS) == 0
    assert page_table.shape[1] % SCORE_PAGES == 0 and page_table.shape[1] % ATTN_PAGES == 0
    state_rows = SSM_HEADS * SSM_HEAD_DIM
    bias_t = _bias_tiles(rel_bias, TQ_PROMPT)

    yp = x_prompt
    ys = jnp.pad(x_sample, ((0, 0), (0, CHUNK - ts), (0, 0)))
    outs_p, outs_s = [], []
    for l in range(depth):
        lw = _layer_weights(g_pre[l], w_in[l], conv_w[l], conv_b[l], dt_bias[l], a_log[l], d_skip[l], g_ssm[l],
                            w_out[l], g_post[l])

        def prompt_attn(proj):
            return _prompt_attention(proj, bias_t, bp, sp, TQ_PROMPT)

        def sample_attn(proj, layer=l):
            att8 = _sample_attention(page_table, rel_bias, proj, cache_k[layer], cache_v[layer], cache_kidx[layer],
                                     bs, ts)
            return jnp.pad(att8, ((0, 0), (0, CHUNK - SUBLANES), (0, 0))).reshape(bs * CHUNK, ATT_WIDTH)

        op = _mixer(yp, lw, 0, sp, jnp.zeros((bp, state_rows, D_STATE), F32),
                    jnp.zeros((bp, SUBLANES, CONV_CH), F32), prompt_attn)
        c0 = jnp.pad(state_conv[l], ((0, 0), (SUBLANES - (CONV_W - 1), 0), (0, 0)))
        os_ = _mixer(ys, lw, past, ts, state_ssm[l].reshape(bs, state_rows, D_STATE), c0, sample_attn)
        yp = op[0]
        ys = jnp.pad(os_[0], ((0, 0), (0, CHUNK - ts), (0, 0)))
        outs_p.append(op[1:])
        outs_s.append(os_[1:])
    stack = lambda outs, n: jnp.stack([o[n] for o in outs])
    return (yp, ys[:, :ts], *[stack(outs_p, n) for n in range(5)], *[stack(outs_s, n) for n in range(5)])
```

```python
import functools
import math

import jax
import jax.numpy as jnp
import numpy as np
from jax import lax
from jax.experimental import pallas as pl
from jax.experimental.pallas import tpu as pltpu

F32 = jnp.float32
BF16 = jnp.bfloat16
I32 = jnp.int32

D_MODEL = 1024
PAGE_SIZE = 128
HEAD_DIM = 64
ATT_WIDTH = 512
N_HEADS_A = 8
N_IDX_HEADS = 8
D_IDX = 64
IDX_ROPE = 32
ROPE_BASE = 10000.0
TOPK_MAX = 256
NUM_BUCKETS = 32
MAX_DISTANCE = 128
SSM_WIDTH = 512
SSM_HEAD_DIM = 64
SSM_HEADS = 8
SSM_GROUPS = 2
D_STATE = 128
CONV_W = 4
CONV_CH = 1024
CHUNK = 128
EPS = 1e-6

LANES = 128
SUBLANES = 8
BF16_ROWS = 16
VMEM_LIMIT = 56 * 1024 * 1024
NEG = -1e30
INT_MIN = -2 ** 31
LOG2E = 1.4426950408889634

SM_WI = D_IDX
SM_DT = D_IDX + N_IDX_HEADS
ROW_COLS = 3 * ATT_WIDTH + SSM_WIDTH + CONV_CH
QI_WIDTH = N_IDX_HEADS * D_IDX
VP_ROWS = N_HEADS_A * LANES
SPLIT3 = 4 * D_IDX


def _nt(a, b, **kw):
    return lax.dot_general(a, b, (((1,), (1,)), ((), ())), preferred_element_type=F32, **kw)


def _dot(a, b, **kw):
    return jnp.dot(a, b, preferred_element_type=F32, **kw)


def _split(x):
    hi = x.astype(BF16)
    return hi, (x - hi.astype(F32)).astype(BF16)


def _split3(x):
    hi = x.astype(BF16)
    rest = x - hi.astype(F32)
    mid = rest.astype(BF16)
    return hi, mid, (rest - mid.astype(F32)).astype(BF16)


def _sigmoid(x):
    return 1.0 / (1.0 + jnp.exp(-x))


def _cparams(sem):
    return pltpu.CompilerParams(dimension_semantics=sem, vmem_limit_bytes=VMEM_LIMIT)


def _rope_table_kernel(inv_ref, cos_ref, sin_ref, cost_ref, sint_ref, *, pos_off):
    rows = cos_ref.shape[0]
    pos = (lax.broadcasted_iota(I32, (rows, LANES), 0) + pos_off).astype(F32)
    ang = pos * inv_ref[...]
    c = jnp.cos(ang)
    s = jnp.sin(ang)
    cos_ref[...] = c
    sin_ref[...] = s
    cost_ref[...] = c.T
    sint_ref[...] = s.T


def _rope_tables(rows, pos_off):
    inv = ROPE_BASE ** (-jnp.arange(0, IDX_ROPE, 2, dtype=F32) / IDX_ROPE)
    l64 = np.arange(LANES) % D_IDX
    inv_row = jnp.where(l64 < IDX_ROPE, inv[l64 % (IDX_ROPE // 2)], 0.0).astype(F32)[None, :]
    return pl.pallas_call(
        functools.partial(_rope_table_kernel, pos_off=pos_off),
        out_shape=(jax.ShapeDtypeStruct((rows, LANES), F32),) * 2 + (jax.ShapeDtypeStruct((LANES, rows), F32),) * 2,
    )(inv_row)


def _inproj_kernel(x_ref, g_ref, wr_ref, wqt_ref, wqih_ref, wqil_ref, wsh_ref, wsb_ref,
                   cos_ref, sin_ref, cost_ref, sint_ref,
                   qet_ref, qot_ref, k3_ref, kb_ref, v_ref, v3_ref, vpt_ref, ga_ref, z_ref, xbc_ref, qit_ref, sm_ref,
                   smt_ref):
    x = x_ref[...]
    hn = x * lax.rsqrt(jnp.mean(x * x, axis=-1, keepdims=True) + EPS) * g_ref[...]
    hb, hlo = _split(hn)
    tm = x.shape[0]

    def rows(lo, width):
        return _dot(hb, wr_ref[:, lo:lo + width])

    k = rows(0, ATT_WIDTH)
    kb_ref[...] = k.astype(BF16)
    v = rows(ATT_WIDTH, ATT_WIDTH)
    v_ref[...] = v
    for h in range(N_HEADS_A):
        k3_ref[:, h, :] = k[:, h * HEAD_DIM:(h + 1) * HEAD_DIM]
        v3_ref[:, h, :] = v[:, h * HEAD_DIM:(h + 1) * HEAD_DIM]
    ga_ref[...] = rows(2 * ATT_WIDTH, ATT_WIDTH)
    z_ref[...] = rows(3 * ATT_WIDTH, SSM_WIDTH)
    xbc_ref[...] = rows(3 * ATT_WIDTH + SSM_WIDTH, CONV_CH)

    vt = v.T
    ones = jnp.ones((HEAD_DIM, tm), BF16)
    for h in range(N_HEADS_A):
        vpt_ref[h * LANES:h * LANES + HEAD_DIM, :] = vt[h * HEAD_DIM:(h + 1) * HEAD_DIM, :].astype(BF16)
        vpt_ref[h * LANES + HEAD_DIM:(h + 1) * LANES, :] = ones

    qt = _nt(wqt_ref[...], hb) * (HEAD_DIM ** -0.5 * LOG2E)
    even = (lax.broadcasted_iota(I32, qt.shape, 0) & HEAD_DIM) == 0
    qet_ref[...] = jnp.where(even, qt, 0.0).astype(BF16)
    qot_ref[...] = jnp.where(even, 0.0, qt).astype(BF16)

    qit = _nt(wqih_ref[...], hb) + (_nt(wqih_ref[...], hlo) + _nt(wqil_ref[...], hb))
    ct = cost_ref[0:D_IDX, :]
    st = sint_ref[0:D_IDX, :]
    first_t = lax.broadcasted_iota(I32, (D_IDX, tm), 0) < IDX_ROPE // 2
    s1t = jnp.where(first_t, -st, 0.0)
    s2t = jnp.where(first_t, 0.0, st)
    for h in range(N_IDX_HEADS):
        xh = qit[h * D_IDX:(h + 1) * D_IDX, :]
        qit_ref[h * D_IDX:(h + 1) * D_IDX, :] = (xh * ct + pltpu.roll(xh, D_IDX - IDX_ROPE // 2, 0) * s1t
                                                 + pltpu.roll(xh, IDX_ROPE // 2, 0) * s2t)

    both = _dot(hb, wsb_ref[...])
    sm = both[:, :LANES] + (both[:, LANES:] + _dot(hlo, wsh_ref[...]))
    lane = lax.broadcasted_iota(I32, (tm, LANES), 1)
    is_ki = lane < D_IDX
    first = (lane & (D_IDX - 1)) < IDX_ROPE // 2
    c = jnp.where(is_ki, cos_ref[...], 1.0)
    s = jnp.where(is_ki, sin_ref[...], 0.0)
    sm = (sm * c + pltpu.roll(sm, LANES - IDX_ROPE // 2, 1) * jnp.where(first, -s, 0.0)
          + pltpu.roll(sm, IDX_ROPE // 2, 1) * jnp.where(first, 0.0, s))
    is_wi = (lane >= SM_WI) & (lane < SM_DT)
    sm = jnp.where(is_wi, sm * (N_IDX_HEADS ** -0.5), sm)
    sm_ref[...] = sm
    smt_ref[...] = sm.T


def _inproj(x2d, lw, tables, tm):
    n = x2d.shape[0]
    cos_t, sin_t, cos_tt, sin_tt = tables
    tab_blocks = cos_t.shape[0] // tm
    row = lambda i: (i, 0)
    col = lambda i: (0, i)
    const = lambda i: (0, 0)
    rows = lambda w, dt: (jax.ShapeDtypeStruct((n, w), dt), pl.BlockSpec((tm, w), row))
    cols = lambda w, dt: (jax.ShapeDtypeStruct((w, n), dt), pl.BlockSpec((w, tm), col))
    full = lambda a: pl.BlockSpec(a.shape, const)
    heads = (jax.ShapeDtypeStruct((n, N_HEADS_A, HEAD_DIM), F32),
             pl.BlockSpec((tm, N_HEADS_A, HEAD_DIM), lambda i: (i, 0, 0)))
    outs = [cols(ATT_WIDTH, BF16), cols(ATT_WIDTH, BF16), heads, rows(ATT_WIDTH, BF16),
            rows(ATT_WIDTH, F32), heads, cols(VP_ROWS, BF16), rows(ATT_WIDTH, F32), rows(SSM_WIDTH, F32),
            rows(CONV_CH, F32), cols(QI_WIDTH, F32), rows(LANES, F32), cols(LANES, F32)]
    weights = [lw["g_pre"], lw["w_rows"], lw["wq_t"], lw["wqi_t_hi"], lw["wqi_t_lo"], lw["ws_hi"], lw["ws_both"]]
    return pl.pallas_call(
        _inproj_kernel,
        grid=(n // tm,),
        in_specs=[pl.BlockSpec((tm, D_MODEL), row)] + [full(w) for w in weights]
                 + [pl.BlockSpec((tm, LANES), lambda i: (i % tab_blocks, 0))] * 2
                 + [pl.BlockSpec((LANES, tm), lambda i: (0, i % tab_blocks))] * 2,
        out_specs=[o[1] for o in outs],
        out_shape=[o[0] for o in outs],
        compiler_params=_cparams(("arbitrary",)),
    )(x2d, *weights, cos_t, sin_t, cos_tt, sin_tt)


def _bucket(dist):
    max_exact = NUM_BUCKETS // 2
    n = jnp.maximum(dist, 0)
    nf = jnp.maximum(n, max_exact).astype(F32)
    large = max_exact + jnp.floor(jnp.log(nf / max_exact) / math.log(MAX_DISTANCE / max_exact)
                                  * (NUM_BUCKETS - max_exact)).astype(I32)
    large = jnp.minimum(large, NUM_BUCKETS - 1)
    return jnp.where(n < max_exact, n, large)


def _bias_lookup(bucket, relb_ref, h):
    out = jnp.full(bucket.shape, relb_ref[0, h], F32)
    for b in range(1, NUM_BUCKETS):
        out = jnp.where(bucket == b, relb_ref[b, h], out)
    return out


KEY_NEG_INF = INT_MIN + 0x7FFFFF


def _bit(n):
    return lax.shift_left(jnp.int32(1), jnp.asarray(n, I32))


def _key_to_float(key):
    return pltpu.bitcast(jnp.where(key < 0, key ^ 0x7FFFFFFF, key), F32)


def _search_threshold(count_ge, n_top, n_keys, shape):
    def body(it, thr):
        cand = thr ^ _bit(31 - it)
        cnt = jnp.where(cand < KEY_NEG_INF, jnp.asarray(n_keys, F32), count_ge(_key_to_float(cand)))
        return jnp.where(cnt >= float(n_top), cand, thr)

    return _key_to_float(lax.fori_loop(0, 32, body, jnp.full(shape, INT_MIN, I32)))


def _bf16_step_bits(u):
    b = u - 32768
    return lax.shift_left(jnp.where(b < 0, b ^ 0x7FFF, b), jnp.int32(16))


def _float_image(bits):
    return jnp.where(bits < 0, bits ^ 0x7FFFFFFF, bits)


def _search_threshold_2level(count_ge_rounded, count_ge, n_top, n_keys, shape):
    k = float(n_top)
    n_all = jnp.asarray(n_keys, F32)

    def admits(u, counter):
        bits = _bf16_step_bits(u)
        return jnp.where(_float_image(bits) < KEY_NEG_INF, n_all, counter(pltpu.bitcast(bits, F32))) >= k

    def coarse(it, u):
        cand = u | _bit(15 - it)
        return jnp.where(admits(cand, count_ge_rounded), cand, u)

    u = lax.fori_loop(0, 16, coarse, jnp.zeros(shape, I32))
    u = jnp.where(admits(u, count_ge), u, u - 1)
    base = _float_image(_bf16_step_bits(u))

    def fine(it, d):
        cand = d | _bit(15 - it)
        return jnp.where(count_ge(_key_to_float(base + cand)) >= k, cand, d)

    return _key_to_float(base + lax.fori_loop(0, 16, fine, jnp.zeros(shape, I32)))


def _search_last_tie(count_ties_before, need, pos_bits, shape):
    def body(it, q):
        cand = q | _bit(pos_bits - 1 - it)
        return jnp.where(count_ties_before(cand) < need, cand, q)

    return lax.fori_loop(0, pos_bits, body, jnp.zeros(shape, I32))


def _select_madd(score, pos, thr, last):
    return jnp.where(score > thr, 0.0, jnp.where(score == thr, jnp.where(pos <= last, 0.0, NEG), NEG))


def _bias_tiles_kernel(relb_ref, o_ref, *, tq):
    ki = lax.broadcasted_iota(I32, (tq, tq), 0)
    qi = lax.broadcasted_iota(I32, (tq, tq), 1)
    for kind in range(2):
        bucket = _bucket(qi - ki + kind * tq)
        for h in range(N_HEADS_A):
            o_ref[h, kind] = (_bias_lookup(bucket, relb_ref, h) - relb_ref[NUM_BUCKETS - 1, h]) * LOG2E


def _bias_tiles(rel_bias, tq):
    return pl.pallas_call(
        functools.partial(_bias_tiles_kernel, tq=tq),
        in_specs=[pl.BlockSpec(memory_space=pltpu.SMEM)],
        out_shape=jax.ShapeDtypeStruct((N_HEADS_A, 2, tq, tq), F32),
        compiler_params=pltpu.CompilerParams(vmem_limit_bytes=VMEM_LIMIT),
    )(rel_bias)


KI3_BUILD_ROWS = 512


def _pattn_kernel(qit_ref, smtq_ref, sm_ref, qet_ref, qot_ref, kb_ref, vpt_ref, bt_ref, o_ref,
                  ki3_ref, qh3_ref, sc_ref, hi_ref, last_ref, m_ref, acc_ref, *, tq, n_top, pos_bits):
    i = pl.program_id(1)
    nch = i + 1
    s_len = sc_ref.shape[0]
    kiota = lax.broadcasted_iota(I32, (tq, tq), 0)
    qpos = i * tq + lax.broadcasted_iota(I32, (tq, tq), 1)

    def rows(c, width=tq):
        return pl.ds(pl.multiple_of(c * tq, tq), width)

    @pl.when(i == 0)
    def _():
        low = lax.broadcasted_iota(I32, (KI3_BUILD_ROWS, LANES), 1) < D_IDX

        def body(r, carry):
            sl = pl.ds(pl.multiple_of(r * KI3_BUILD_ROWS, KI3_BUILD_ROWS), KI3_BUILD_ROWS)
            x = sm_ref[sl, :]
            hi = x.astype(BF16).astype(F32)
            ki3_ref[sl, 0:LANES] = jnp.where(low, hi, pltpu.roll(hi, D_IDX, 1)).astype(BF16)
            ki3_ref[sl, LANES:2 * LANES] = jnp.where(low, x - hi, 0.0).astype(BF16)
            return carry
        lax.fori_loop(0, s_len // KI3_BUILD_ROWS, body, 0)

    for h in range(N_IDX_HEADS):
        hi, lo = _split(qit_ref[h * D_IDX:(h + 1) * D_IDX, :])
        qh3_ref[h, 0:D_IDX, :] = hi
        qh3_ref[h, D_IDX:2 * D_IDX, :] = lo
        qh3_ref[h, 2 * D_IDX:3 * D_IDX, :] = hi
        qh3_ref[h, 3 * D_IDX:, :] = jnp.zeros((D_IDX, tq), BF16)
    w8 = smtq_ref[SM_WI:SM_WI + N_IDX_HEADS, :] * (D_IDX ** -0.5)

    def score_body(c2, carry):
        blocks = [2 * c2, jnp.minimum(2 * c2 + 1, i)]
        dots = [[_dot(ki3_ref[rows(c), :], qh3_ref[h]) for h in range(N_IDX_HEADS)] for c in blocks]
        for c, d in zip(blocks, dots):
            terms = [jnp.maximum(d[h], 0.0) * w8[h:h + 1, :] for h in range(N_IDX_HEADS)]
            while len(terms) > 1:
                terms = [terms[j] + terms[j + 1] for j in range(0, len(terms), 2)]
            sc = jnp.where(c * tq + kiota <= qpos, terms[0], -jnp.inf)
            sc_ref[rows(c), :] = sc
            hi_ref[rows(c), :] = sc.astype(BF16)
        return carry

    lax.fori_loop(0, (nch + 1) // 2, score_body, 0)

    def over_keys(x, op):
        x = x.reshape(x.shape[0] // SUBLANES, SUBLANES, tq)
        while x.shape[0] > 1:
            half = x.shape[0] // 2
            x = op(x[:half], x[half:])
        return x[0]

    def count(pred):
        def body(c, acc):
            hit = jnp.where(pred(sc_ref[rows(c), :], c * tq + kiota), 1.0, 0.0)
            return acc + over_keys(hit, jnp.add)
        acc = lax.fori_loop(0, nch, body, jnp.zeros((SUBLANES, tq), F32))
        return jnp.broadcast_to(jnp.sum(acc, axis=0, keepdims=True), (SUBLANES, tq))

    def count_rounded(t):
        t16 = jnp.concatenate([t, t], axis=0).astype(BF16)

        def body(c, acc):
            h = hi_ref[rows(c), :].reshape(tq // BF16_ROWS, BF16_ROWS, tq)
            hit = jnp.where(h >= t16[None], jnp.ones_like(h), jnp.zeros_like(h))
            while hit.shape[0] > 1:
                half = hit.shape[0] // 2
                hit = hit[:half] + hit[half:]
            part = hit[0].astype(F32)
            return acc + (part[0:SUBLANES] + part[SUBLANES:])
        acc = lax.fori_loop(0, nch, body, jnp.zeros((SUBLANES, tq), F32))
        return jnp.broadcast_to(jnp.sum(acc, axis=0, keepdims=True), (SUBLANES, tq))

    thr = _search_threshold_2level(count_rounded, lambda t: count(lambda sc, pos: sc >= t[0:1, :]),
                                   n_top, nch * tq, (SUBLANES, tq))
    thr_row = thr[0:1, :]
    kept = count(lambda sc, pos: sc >= thr_row)
    last_ref[...] = jnp.full((SUBLANES, tq), 2 ** pos_bits - 1, I32)

    @pl.when(jnp.max(kept) > float(n_top))
    def _():
        need = float(n_top) - count(lambda sc, pos: sc > thr_row)
        last_ref[...] = _search_last_tie(
            lambda q: count(lambda sc, pos: (sc == thr_row) & (pos < q[0:1, :])), need, pos_bits, (SUBLANES, tq))

    last_row = last_ref[0:1, :]

    def madd_body(c, carry):
        pos = c * tq + kiota
        madd = _select_madd(sc_ref[rows(c), :], pos, thr_row, last_row)
        sc_ref[rows(c), :] = jnp.where(pos <= qpos, madd, NEG)
        return carry

    lax.fori_loop(0, nch, madd_body, 0)

    m_ref[...] = jnp.full(m_ref.shape, NEG, F32)
    acc_ref[...] = jnp.zeros(acc_ref.shape, F32)

    def attend(c0, width, bias_of_head):
        sl = rows(c0, width)
        madd = sc_ref[sl, :]
        logits = []
        for h in range(N_HEADS_A):
            p2 = h // 2
            qt = (qet_ref if h % 2 == 0 else qot_ref)[p2 * LANES:(p2 + 1) * LANES, :]
            logits.append(_dot(kb_ref[sl, p2 * LANES:(p2 + 1) * LANES], qt))
        probs, alphas = [], []
        for h in range(N_HEADS_A):
            s = logits[h] + madd
            if bias_of_head is not None:
                s = s + bias_of_head(h)
            m_prev = m_ref[h]
            cmax = over_keys(s, jnp.maximum)
            m_new = jnp.maximum(m_prev, jnp.broadcast_to(jnp.max(cmax, axis=0, keepdims=True), (SUBLANES, tq)))
            probs.append(jnp.exp2(s - m_new[0:1, :]).astype(BF16))
            alphas.append(jnp.exp2(m_prev - m_new)[0:1, :])
            m_ref[h] = m_new
        for h in range(N_HEADS_A):
            pv = _dot(vpt_ref[h * LANES:(h + 1) * LANES, sl], probs[h])
            acc_ref[h] = alphas[h] * acc_ref[h] + pv

    n_far = jnp.maximum(i - 1, 0)

    def far_body(c, carry):
        attend(2 * c, 2 * tq, None)
        return carry

    lax.fori_loop(0, n_far // 2, far_body, 0)

    @pl.when(n_far % 2 == 1)
    def _():
        attend(n_far - 1, tq, None)

    @pl.when(i >= 1)
    def _():
        attend(i - 1, tq, lambda h: bt_ref[h, 1])

    attend(i, tq, lambda h: bt_ref[h, 0])

    outs = []
    for h in range(N_HEADS_A):
        acc = acc_ref[h]
        outs.append(acc[0:HEAD_DIM, :] / acc[HEAD_DIM:, :])
    o_ref[...] = jnp.concatenate(outs, axis=0).T


def _prompt_attention(proj, bias_t, nb, s, tq):
    nq = s // tq
    n_top = min(TOPK_MAX, s // 4)
    pos_bits = max(1, (s - 1).bit_length())
    qcols = lambda b, i: (0, b * nq + i)
    seq_rows = lambda b, i: (b, 0)
    seq_cols = lambda b, i: (0, b)
    return pl.pallas_call(
        functools.partial(_pattn_kernel, tq=tq, n_top=n_top, pos_bits=pos_bits),
        grid=(nb, nq),
        in_specs=[pl.BlockSpec((QI_WIDTH, tq), qcols), pl.BlockSpec((LANES, tq), qcols),
                  pl.BlockSpec((s, LANES), seq_rows),
                  pl.BlockSpec((ATT_WIDTH, tq), qcols), pl.BlockSpec((ATT_WIDTH, tq), qcols),
                  pl.BlockSpec((s, ATT_WIDTH), seq_rows), pl.BlockSpec((VP_ROWS, s), seq_cols),
                  pl.BlockSpec((N_HEADS_A, 2, tq, tq), lambda b, i: (0, 0, 0, 0),
                               pipeline_mode=pl.Buffered(1))],
        out_specs=pl.BlockSpec((tq, ATT_WIDTH), lambda b, i: (b * nq + i, 0)),
        out_shape=jax.ShapeDtypeStruct((nb * s, ATT_WIDTH), F32),
        scratch_shapes=[pltpu.VMEM((s, SPLIT3), BF16), pltpu.VMEM((N_IDX_HEADS, SPLIT3, tq), BF16),
                        pltpu.VMEM((s, tq), F32), pltpu.VMEM((s, tq), BF16), pltpu.VMEM((SUBLANES, tq), I32),
                        pltpu.VMEM((N_HEADS_A, SUBLANES, tq), F32), pltpu.VMEM((N_HEADS_A, LANES, tq), F32)],
        compiler_params=_cparams(("arbitrary", "arbitrary")),
    )(proj["qi_t"], proj["small_t"], proj["small"], proj["qe_t"], proj["qo_t"], proj["k_b"], proj["vp_t"], bias_t)


def _ssd_kernel(xbc_ref, z_ref, sm_ref, cw_ref, cb_ref, dtb_ref, alog_ref, dsk_ref, gs_ref, ex_ref, ext_ref,
                h0_ref, c0_ref, att_ref, ga_ref, x_ref, wt_ref, wb_ref, gp_ref, y_ref, hf_ref, xp_ref, st_ref, *,
                t_valid):
    c = pl.program_id(1)
    L = CHUNK

    @pl.when(c == 0)
    def _():
        st_ref[...] = h0_ref[...]
        xp_ref[0:SUBLANES, :] = c0_ref[...]

    xp_ref[SUBLANES:SUBLANES + L, :] = xbc_ref[...]
    conv = cb_ref[...]
    for j in range(CONV_W):
        lo = SUBLANES - (CONV_W - 1) + j
        conv = conv + xp_ref[lo:lo + L, :] * cw_ref[j:j + 1, :]
    xp_ref[0:SUBLANES, :] = xp_ref[L:L + SUBLANES, :]
    act = conv * _sigmoid(conv)
    xs = act[:, :SSM_WIDTH]
    bm = act[:, SSM_WIDTH:SSM_WIDTH + SSM_GROUPS * D_STATE]
    cm = act[:, SSM_WIDTH + SSM_GROUPS * D_STATE:]

    raw = sm_ref[...] + dtb_ref[...]
    dtf = jnp.maximum(raw, 0.0) + jnp.log1p(jnp.exp(-jnp.abs(raw)))
    row = lax.broadcasted_iota(I32, (L, LANES), 0)
    if t_valid < L:
        dtf = jnp.where(row < t_valid, dtf, 0.0)
    adt = dtf * (-jnp.exp(alog_ref[...]))
    tril = row >= lax.broadcasted_iota(I32, (L, LANES), 1)
    tril01 = jnp.where(tril, 1.0, 0.0).astype(BF16)
    cs = sum(_dot(tril01, part) for part in _split3(adt))
    ex = ex_ref[...]
    dtx = sum(_dot(part, ex) for part in _split3(dtf))
    csx = sum(_dot(part, ex) for part in _split3(cs))
    cst = cs.T
    x = xs * dtx
    w = x * jnp.exp(csx[L - 1:L, :] - csx)
    ecsx = jnp.exp(csx)
    dec = jnp.exp(jnp.sum(ext_ref[...] * cs[L - 1:L, :], axis=1, keepdims=True))
    low = lax.broadcasted_iota(I32, (L, LANES), 1) < SSM_HEAD_DIM

    ys = []
    for p2 in range(SSM_HEADS // 2):
        g = (2 * p2) // (SSM_HEADS // SSM_GROUPS)
        cg = cm[:, g * D_STATE:(g + 1) * D_STATE].astype(BF16)
        bg = bm[:, g * D_STATE:(g + 1) * D_STATE].astype(BF16)
        cb_mat = _nt(cg, bg)
        lanes = slice(p2 * LANES, (p2 + 1) * LANES)
        xp = x[:, lanes].astype(BF16)
        yd = []
        for h in (2 * p2, 2 * p2 + 1):
            diff = cs[:, SM_DT + h:SM_DT + h + 1] - cst[SM_DT + h:SM_DT + h + 1, :]
            lm = jnp.exp(jnp.where(tril, diff, NEG))
            yd.append(_dot((cb_mat * lm).astype(BF16), xp))
        rows = slice(p2 * LANES, (p2 + 1) * LANES)
        st = st_ref[rows, :]
        y_off = _nt(cg, st.astype(BF16)) * ecsx[:, lanes]
        ys.append(jnp.where(low, yd[0], yd[1]) + y_off)
        upd = _dot(w[:, lanes].T.astype(BF16), bg)
        st_ref[rows, :] = st * dec[rows, :] + upd

    y = jnp.concatenate(ys, axis=1) + dsk_ref[...] * xs
    zz = z_ref[...]
    gated = y * (zz * _sigmoid(zz))
    ssm = gated * lax.rsqrt(jnp.mean(gated * gated, axis=-1, keepdims=True) + EPS) * gs_ref[...]

    ga = ga_ref[...]
    att = att_ref[...] * (ga * _sigmoid(ga))
    out = _dot(att.astype(BF16), wt_ref[...]) + _dot(ssm.astype(BF16), wb_ref[...])
    y_ref[...] = x_ref[...] + out * lax.rsqrt(jnp.mean(out * out, axis=-1, keepdims=True) + EPS) * gp_ref[...]

    @pl.when(c == pl.num_programs(1) - 1)
    def _():
        hf_ref[...] = st_ref[...]


def _ssd_out(proj, att, x2d, lw, h0, c0, nb, s, t_valid):
    xbc, z, small, ga = proj["xbc"], proj["z"], proj["small"], proj["ga"]
    nc = s // CHUNK
    blk = lambda b, c: (b * nc + c, 0)
    const = lambda b, c: (0, 0)
    per_b = lambda b, c: (b, 0, 0)
    lanes = np.arange(LANES)
    dt_row = lambda v: jnp.zeros((1, LANES), F32).at[0, SM_DT:SM_DT + SSM_HEADS].set(v)
    expand = (lanes[:, None] == SM_DT + np.arange(SSM_WIDTH)[None, :] // SSM_HEAD_DIM).astype(np.float32)
    state_rows = SSM_HEADS * SSM_HEAD_DIM
    return pl.pallas_call(
        functools.partial(_ssd_kernel, t_valid=t_valid),
        grid=(nb, nc),
        in_specs=[pl.BlockSpec((CHUNK, CONV_CH), blk), pl.BlockSpec((CHUNK, SSM_WIDTH), blk),
                  pl.BlockSpec((CHUNK, LANES), blk),
                  pl.BlockSpec((CONV_W, CONV_CH), const), pl.BlockSpec((1, CONV_CH), const),
                  pl.BlockSpec((1, LANES), const), pl.BlockSpec((1, LANES), const),
                  pl.BlockSpec((1, SSM_WIDTH), const), pl.BlockSpec((1, SSM_WIDTH), const),
                  pl.BlockSpec((LANES, SSM_WIDTH), const), pl.BlockSpec((SSM_WIDTH, LANES), const),
                  pl.BlockSpec((None, state_rows, D_STATE), per_b),
                  pl.BlockSpec((None, SUBLANES, CONV_CH), per_b),
                  pl.BlockSpec((CHUNK, ATT_WIDTH), blk), pl.BlockSpec((CHUNK, ATT_WIDTH), blk),
                  pl.BlockSpec((CHUNK, D_MODEL), blk),
                  pl.BlockSpec((ATT_WIDTH, D_MODEL), const), pl.BlockSpec((SSM_WIDTH, D_MODEL), const),
                  pl.BlockSpec((1, D_MODEL), const)],
        out_specs=[pl.BlockSpec((CHUNK, D_MODEL), blk), pl.BlockSpec((None, state_rows, D_STATE), per_b)],
        out_shape=[jax.ShapeDtypeStruct((nb * s, D_MODEL), F32),
                   jax.ShapeDtypeStruct((nb, state_rows, D_STATE), F32)],
        scratch_shapes=[pltpu.VMEM((CHUNK + SUBLANES, CONV_CH), F32), pltpu.VMEM((state_rows, D_STATE), F32)],
        compiler_params=_cparams(("arbitrary", "arbitrary")),
    )(xbc, z, small, lw["conv_w"], lw["conv_b"][None, :], dt_row(lw["dt_bias"]), dt_row(lw["a_log"]),
      jnp.repeat(lw["d_skip"], SSM_HEAD_DIM)[None, :], lw["g_ssm"][None, :], jnp.asarray(expand, BF16),
      jnp.asarray(expand.T), h0, c0, att, ga, x2d, lw["w_top"], lw["w_bot"], lw["g_post"])


SCORE_PAGES = 64
ATTN_PAGES = 32
ROWS_Q = N_HEADS_A * SUBLANES
COUNT_CHAINS = 4
SELECT_GROUP = 4


def _sscore_kernel(pt_ref, qall3_ref, wcol_ref, smt_ref, *rest, past, t_new, n_top, pos_bits, group):
    pages = rest[:SCORE_PAGES]
    madd_ref = rest[SCORE_PAGES]
    sc_ref, last_ref = rest[SCORE_PAGES + 1:]
    j = pl.program_id(1)
    member = pl.program_id(0) % group
    mine = pl.ds(pl.multiple_of(member * SUBLANES, SUBLANES), SUBLANES)
    rows_g = group * SUBLANES
    kw = SCORE_PAGES * PAGE_SIZE
    total = past + LANES
    qall3 = qall3_ref[...]
    wcol = wcol_ref[...]
    zeros = jnp.zeros((D_IDX, PAGE_SIZE), BF16)

    def dots(kt):
        hi, lo = _split(kt)
        return _dot(qall3, jnp.concatenate([hi, hi, lo, zeros], axis=0))

    def weigh(d):
        r = jnp.maximum(d * (D_IDX ** -0.5), 0.0) * wcol
        sc = r[0:SUBLANES]
        for h in range(1, N_IDX_HEADS):
            sc = sc + r[h * SUBLANES:(h + 1) * SUBLANES]
        return sc

    def scores(kt):
        return weigh(dots(kt))

    page_dots = [dots(page[...]) for page in pages]
    for r, d in enumerate(page_dots):
        sl = pl.ds(pl.multiple_of(j * kw + r * PAGE_SIZE, PAGE_SIZE), PAGE_SIZE)
        sc_ref[mine, sl] = weigh(d)

    @pl.when(j == pl.num_programs(1) - 1)
    def _():
        lane8 = lax.broadcasted_iota(I32, (SUBLANES, LANES), 1)
        row8 = lax.broadcasted_iota(I32, (SUBLANES, LANES), 0)
        vis8 = (lane8 <= row8) & (lane8 < t_new)
        sc_ref[mine, past:total] = jnp.where(vis8, scores(smt_ref[0:D_IDX, :]), -jnp.inf)

    @pl.when(jnp.logical_and(j == pl.num_programs(1) - 1, member == group - 1))
    def _():
        shape = (rows_g, LANES)
        lane = lax.broadcasted_iota(I32, shape, 1)
        tok = lax.broadcasted_iota(I32, shape, 0) % SUBLANES
        vis = (lane <= tok) & (lane < t_new)

        def count(pred):
            accs = [jnp.zeros(shape, F32)] * COUNT_CHAINS
            for t in range(total // LANES):
                hit = pred(sc_ref[:, t * LANES:(t + 1) * LANES], t * LANES + lane)
                accs[t % COUNT_CHAINS] = accs[t % COUNT_CHAINS] + jnp.where(hit, 1.0, 0.0)
            while len(accs) > 1:
                accs = [accs[n] + accs[n + 1] for n in range(0, len(accs), 2)]
            return jnp.broadcast_to(jnp.sum(accs[0], axis=1, keepdims=True), shape)

        thr = _search_threshold(lambda t: count(lambda sc, pos: sc >= t), n_top, total, shape)
        kept = count(lambda sc, pos: sc >= thr)
        last_ref[...] = jnp.full(shape, 2 ** pos_bits - 1, I32)

        @pl.when(jnp.max(kept) > float(n_top))
        def _():
            need = float(n_top) - count(lambda sc, pos: sc > thr)
            last_ref[...] = _search_last_tie(
                lambda q: count(lambda sc, pos: (sc == thr) & (pos < q)), need, pos_bits, shape)

        last = last_ref[...]
        for t in range(total // LANES):
            sl = slice(t * LANES, (t + 1) * LANES)
            pos = t * LANES + lane
            madd = _select_madd(sc_ref[:, sl], pos, thr, last)
            if t * LANES >= past:
                madd = jnp.where(vis, madd, NEG)
            madd_ref[:, sl] = madd


def _sattn_kernel(pt_ref, relb_ref, qbd_ref, madd_ref, maddn_ref, kbnew_ref, vnew_ref, *rest, past):
    kpages = rest[:ATTN_PAGES]
    vpages = rest[ATTN_PAGES:2 * ATTN_PAGES]
    o_ref, m_ref, l_ref, acc_ref = rest[2 * ATTN_PAGES:]
    j = pl.program_id(1)
    kw = ATTN_PAGES * PAGE_SIZE

    @pl.when(j == 0)
    def _():
        m_ref[...] = jnp.full(m_ref.shape, NEG, F32)
        l_ref[...] = jnp.zeros(l_ref.shape, F32)
        acc_ref[...] = jnp.zeros(acc_ref.shape, F32)

    qbd = qbd_ref[...]

    def far_bias(width):
        row_head = lax.broadcasted_iota(I32, (ROWS_Q, width), 0) // SUBLANES
        out = jnp.full((ROWS_Q, width), relb_ref[NUM_BUCKETS - 1, 0], F32)
        for h in range(1, N_HEADS_A):
            out = jnp.where(row_head == h, relb_ref[NUM_BUCKETS - 1, h], out)
        return out

    def near_bias(width, pos0):
        tok = lax.broadcasted_iota(I32, (SUBLANES, width), 0)
        pos = pos0 + lax.broadcasted_iota(I32, (SUBLANES, width), 1)
        bucket = _bucket(past + tok - pos)
        return jnp.concatenate([_bias_lookup(bucket, relb_ref, h) for h in range(N_HEADS_A)], axis=0)

    def update(logits, bias, madd8, pv_fn):
        s = logits + (bias * LOG2E + jnp.concatenate([madd8] * N_HEADS_A, axis=0))
        m_prev = m_ref[...]
        m_new = jnp.maximum(m_prev, jnp.broadcast_to(jnp.max(s, axis=1, keepdims=True), (ROWS_Q, LANES)))
        alpha = jnp.exp2(m_prev - m_new)
        p = jnp.exp2(s - m_new[:, :1])
        l_ref[...] = alpha * l_ref[...] + jnp.broadcast_to(jnp.sum(p, axis=1, keepdims=True), (ROWS_Q, LANES))
        acc_ref[...] = alpha[:, :1] * acc_ref[...] + pv_fn(p.astype(BF16))
        m_ref[...] = m_new

    def paged(bias):
        logits = jnp.concatenate([_dot(qbd, kp[...].astype(BF16)) for kp in kpages], axis=1)

        def pv_fn(p):
            pv = _nt(p[:, 0:PAGE_SIZE], vpages[0][...].astype(BF16))
            for r in range(1, ATTN_PAGES):
                pv = pv + _nt(p[:, r * PAGE_SIZE:(r + 1) * PAGE_SIZE], vpages[r][...].astype(BF16))
            return pv

        update(logits, bias, madd_ref[...], pv_fn)

    far = (j + 1) * kw + MAX_DISTANCE <= past + 1

    @pl.when(far)
    def _():
        paged(far_bias(kw))

    @pl.when(jnp.logical_not(far))
    def _():
        paged(near_bias(kw, j * kw))

    @pl.when(j == pl.num_programs(1) - 1)
    def _():
        update(_nt(qbd, kbnew_ref[...]), near_bias(LANES, past), maddn_ref[...],
               lambda p: _dot(p, vnew_ref[...].astype(BF16)))
        o = acc_ref[...] / l_ref[:, :1]
        own = (lax.broadcasted_iota(I32, (ROWS_Q, ATT_WIDTH), 0) // SUBLANES
               == lax.broadcasted_iota(I32, (ROWS_Q, ATT_WIDTH), 1) // HEAD_DIM)
        o = jnp.where(own, o, 0.0)
        out = o[0:SUBLANES]
        for h in range(1, N_HEADS_A):
            out = out + o[h * SUBLANES:(h + 1) * SUBLANES]
        o_ref[...] = out


def _sample_attention(page_table, rel_bias, proj, cache_k, cache_v, cache_kidx, nb, t_new):
    n_pages = page_table.shape[1]
    past = n_pages * PAGE_SIZE
    total = past + LANES
    n_top = min(TOPK_MAX, (past + t_new) // 4)
    pos_bits = max(1, (total - 1).bit_length())
    pt = page_table.reshape(-1)
    pool = cache_kidx.shape[0]

    def tok(a_t):
        return a_t.reshape(a_t.shape[0], nb, CHUNK)[:, :, :SUBLANES].transpose(1, 2, 0)

    by_head = lambda a, d: a.reshape(nb, SUBLANES, -1, d).transpose(0, 2, 1, 3)
    qall = by_head(tok(proj["qi_t"]), D_IDX).reshape(nb, ROWS_Q, D_IDX)
    hi, lo = _split(qall)
    qall3 = jnp.concatenate([hi, lo, hi, jnp.zeros_like(hi)], axis=-1)
    wi_t = tok(proj["small_t"])[:, :, SM_WI:SM_WI + N_IDX_HEADS].transpose(0, 2, 1).reshape(nb, ROWS_Q, 1)
    wcol = jnp.broadcast_to(wi_t, (nb, ROWS_Q, LANES))
    q_t = by_head(tok(proj["qe_t"] + proj["qo_t"]), HEAD_DIM)
    eye = jnp.eye(N_HEADS_A, dtype=q_t.dtype)
    qbd = (q_t[:, :, :, None, :] * eye[None, :, None, :, None]).reshape(nb, ROWS_Q, ATT_WIDTH)
    ckt = cache_k.transpose(0, 2, 3, 1).reshape(pool, ATT_WIDTH, PAGE_SIZE)
    cvt = cache_v.transpose(0, 2, 3, 1).reshape(pool, ATT_WIDTH, PAGE_SIZE)
    cit = cache_kidx.transpose(0, 2, 1)

    def page(r, per_step):
        return lambda b, j, pt_ref: (pt_ref[b * n_pages + j * per_step + r], 0, 0)

    seq3 = lambda b, j, pt_ref: (b, 0, 0)
    tcol = lambda b, j, pt_ref: (0, b)
    trow = lambda b, j, pt_ref: (b, 0)

    group = math.gcd(nb, SELECT_GROUP)
    madd = pl.pallas_call(
        functools.partial(_sscore_kernel, past=past, t_new=t_new, n_top=n_top, pos_bits=pos_bits, group=group),
        grid_spec=pltpu.PrefetchScalarGridSpec(
            num_scalar_prefetch=1, grid=(nb, n_pages // SCORE_PAGES),
            in_specs=[pl.BlockSpec((None, ROWS_Q, SPLIT3), seq3), pl.BlockSpec((None, ROWS_Q, LANES), seq3),
                      pl.BlockSpec((LANES, CHUNK), tcol)]
                     + [pl.BlockSpec((None, D_IDX, PAGE_SIZE), page(r, SCORE_PAGES)) for r in range(SCORE_PAGES)],
            out_specs=pl.BlockSpec((None, group * SUBLANES, total), lambda b, j, pt_ref: (b // group, 0, 0)),
            scratch_shapes=[pltpu.VMEM((group * SUBLANES, total), F32),
                            pltpu.VMEM((group * SUBLANES, LANES), I32)]),
        out_shape=jax.ShapeDtypeStruct((nb // group, group * SUBLANES, total), F32),
        compiler_params=_cparams(("arbitrary", "arbitrary")),
    )(pt, qall3, wcol, proj["small_t"], *([cit] * SCORE_PAGES)).reshape(nb, SUBLANES, total)

    kw = ATTN_PAGES * PAGE_SIZE
    kv_specs = [pl.BlockSpec((None, ATT_WIDTH, PAGE_SIZE), page(r, ATTN_PAGES)) for r in range(ATTN_PAGES)]
    return pl.pallas_call(
        functools.partial(_sattn_kernel, past=past),
        grid_spec=pltpu.PrefetchScalarGridSpec(
            num_scalar_prefetch=1, grid=(nb, n_pages // ATTN_PAGES),
            in_specs=[pl.BlockSpec(memory_space=pltpu.SMEM),
                      pl.BlockSpec((None, ROWS_Q, ATT_WIDTH), seq3),
                      pl.BlockSpec((None, SUBLANES, kw), lambda b, j, pt_ref: (b, 0, j)),
                      pl.BlockSpec((None, SUBLANES, LANES), lambda b, j, pt_ref: (b, 0, past // LANES)),
                      pl.BlockSpec((CHUNK, ATT_WIDTH), trow), pl.BlockSpec((CHUNK, ATT_WIDTH), trow)]
                     + kv_specs * 2,
            out_specs=pl.BlockSpec((None, SUBLANES, ATT_WIDTH), seq3),
            scratch_shapes=[pltpu.VMEM((ROWS_Q, LANES), F32), pltpu.VMEM((ROWS_Q, LANES), F32),
                            pltpu.VMEM((ROWS_Q, ATT_WIDTH), F32)]),
        out_shape=jax.ShapeDtypeStruct((nb, SUBLANES, ATT_WIDTH), F32),
        compiler_params=_cparams(("arbitrary", "arbitrary")),
    )(pt, rel_bias, qbd, madd, madd, proj["k_b"], proj["v"], *([ckt] * ATTN_PAGES), *([cvt] * ATTN_PAGES))


TM_PROJ = 256
TQ_PROMPT = 256
PROJ_NAMES = ("qe_t", "qo_t", "k3", "k_b", "v", "v3", "vp_t", "ga", "z", "xbc", "qi_t", "small", "small_t")


def _layer_weights(g_pre, w_in, conv_w, conv_b, dt_bias, a_log, d_skip, g_ssm, w_out, g_post):
    offs = np.cumsum([0, ATT_WIDTH, ATT_WIDTH, ATT_WIDTH, ATT_WIDTH, QI_WIDTH, D_IDX, N_IDX_HEADS,
                      SSM_WIDTH, CONV_CH, SSM_HEADS])
    q, k, v, ga, qi, ki, wi, z, xbc, dt = [w_in[:, offs[n]:offs[n + 1]] for n in range(10)]
    pad = jnp.zeros((D_MODEL, LANES - D_IDX - N_IDX_HEADS - SSM_HEADS), F32)
    wqi_hi, wqi_lo = _split(qi.T)
    ws_hi, ws_lo = _split(jnp.concatenate([ki, wi, dt, pad], axis=1))
    return dict(g_pre=g_pre[None, :], w_rows=jnp.concatenate([k, v, ga, z, xbc], axis=1).astype(BF16),
                wq_t=q.T.astype(BF16), wqi_t_hi=wqi_hi, wqi_t_lo=wqi_lo, ws_hi=ws_hi,
                ws_both=jnp.concatenate([ws_hi, ws_lo], axis=1),
                conv_w=conv_w, conv_b=conv_b, dt_bias=dt_bias, a_log=a_log, d_skip=d_skip, g_ssm=g_ssm,
                w_top=w_out[:ATT_WIDTH].astype(BF16), w_bot=w_out[ATT_WIDTH:].astype(BF16), g_post=g_post[None, :])


def _mixer(x, lw, pos_off, t_valid, h0, c0, attn_fn):
    nb, s, _ = x.shape
    tm = min(TM_PROJ, s)
    x2d = x.reshape(nb * s, D_MODEL)
    proj = dict(zip(PROJ_NAMES, _inproj(x2d, lw, _rope_tables(s, pos_off), tm)))
    att = attn_fn(proj)
    y, h_final = _ssd_out(proj, att, x2d, lw, h0, c0, nb, s, t_valid)
    r = lambda a: a.reshape(nb, s, -1)[:, :t_valid]
    heads = lambda a: a.reshape(nb, s, N_HEADS_A, HEAD_DIM)[:, :t_valid]
    conv_state = r(proj["xbc"])[:, t_valid - (CONV_W - 1):]
    return (r(y), heads(proj["k3"]), heads(proj["v3"]), r(proj["small"])[..., :D_IDX],
            h_final.reshape(nb, SSM_HEADS, SSM_HEAD_DIM, D_STATE), conv_state)


def kernel(x_prompt, x_sample, cache_k, cache_v, cache_kidx, state_ssm, state_conv, page_table, g_pre, w_in, conv_w, conv_b, dt_bias, a_log, d_skip, g_ssm, w_out, g_post, rel_bias):
    depth = w_in.shape[0]
    bp, sp, _ = x_prompt.shape
    bs, ts, _ = x_sample.shape
    past = page_table.shape[1] * PAGE_SIZE
    assert ts <= SUBLANES and ts >= CONV_W - 1 and sp % max(TQ_PROMPT, KI3_BUILD_ROWS) == 0
    assert page_table.shape[1] % SCORE_PAGES == 0 and page_table.shape[1] % ATTN_PAGES == 0
    state_rows = SSM_HEADS * SSM_HEAD_DIM
    bias_t = _bias_tiles(rel_bias, TQ_PROMPT)

    yp = x_prompt
    ys = jnp.pad(x_sample, ((0, 0), (0, CHUNK - ts), (0, 0)))
    outs_p, outs_s = [], []
    for l in range(depth):
        lw = _layer_weights(g_pre[l], w_in[l], conv_w[l], conv_b[l], dt_bias[l], a_log[l], d_skip[l], g_ssm[l],
                            w_out[l], g_post[l])

        def prompt_attn(proj):
            return _prompt_attention(proj, bias_t, bp, sp, TQ_PROMPT)

        def sample_attn(proj, layer=l):
            att8 = _sample_attention(page_table, rel_bias, proj, cache_k[layer], cache_v[layer], cache_kidx[layer],
                                     bs, ts)
            return jnp.pad(att8, ((0, 0), (0, CHUNK - SUBLANES), (0, 0))).reshape(bs * CHUNK, ATT_WIDTH)

        op = _mixer(yp, lw, 0, sp, jnp.zeros((bp, state_rows, D_STATE), F32),
                    jnp.zeros((bp, SUBLANES, CONV_CH), F32), prompt_attn)
        c0 = jnp.pad(state_conv[l], ((0, 0), (SUBLANES - (CONV_W - 1), 0), (0, 0)))
        os_ = _mixer(ys, lw, past, ts, state_ssm[l].reshape(bs, state_rows, D_STATE), c0, sample_attn)
        yp = op[0]
        ys = jnp.pad(os_[0], ((0, 0), (0, CHUNK - ts), (0, 0)))
        outs_p.append(op[1:])
        outs_s.append(os_[1:])
    stack = lambda outs, n: jnp.stack([o[n] for o in outs])
    return (yp, ys[:, :ts], *[stack(outs_p, n) for n in range(5)], *[stack(outs_s, n) for n in range(5)])
```

```python
import functools
import math

import jax
import jax.numpy as jnp
import numpy as np
from jax import lax
from jax.experimental import pallas as pl
from jax.experimental.pallas import tpu as pltpu

F32 = jnp.float32
BF16 = jnp.bfloat16
I32 = jnp.int32

D_MODEL = 1024
PAGE_SIZE = 128
HEAD_DIM = 64
ATT_WIDTH = 512
N_HEADS_A = 8
N_IDX_HEADS = 8
D_IDX = 64
IDX_ROPE = 32
ROPE_BASE = 10000.0
TOPK_MAX = 256
NUM_BUCKETS = 32
MAX_DISTANCE = 128
SSM_WIDTH = 512
SSM_HEAD_DIM = 64
SSM_HEADS = 8
SSM_GROUPS = 2
D_STATE = 128
CONV_W = 4
CONV_CH = 1024
CHUNK = 128
EPS = 1e-6

LANES = 128
SUBLANES = 8
BF16_ROWS = 16
VMEM_LIMIT = 56 * 1024 * 1024
NEG = -1e30
INT_MIN = -2 ** 31
LOG2E = 1.4426950408889634

SM_WI = D_IDX
SM_DT = D_IDX + N_IDX_HEADS
QI_WIDTH = N_IDX_HEADS * D_IDX
VP_ROWS = N_HEADS_A * LANES
SPLIT3 = 4 * D_IDX


def _nt(a, b, **kw):
    return lax.dot_general(a, b, (((1,), (1,)), ((), ())), preferred_element_type=F32, **kw)


def _dot(a, b, **kw):
    return jnp.dot(a, b, preferred_element_type=F32, **kw)


def _split(x):
    hi = x.astype(BF16)
    return hi, (x - hi.astype(F32)).astype(BF16)


def _split3(x):
    hi = x.astype(BF16)
    rest = x - hi.astype(F32)
    mid = rest.astype(BF16)
    return hi, mid, (rest - mid.astype(F32)).astype(BF16)


def _sigmoid(x):
    return 1.0 / (1.0 + jnp.exp(-x))


def _cparams(sem):
    return pltpu.CompilerParams(dimension_semantics=sem, vmem_limit_bytes=VMEM_LIMIT)


def _rope_table_kernel(inv_ref, cos_ref, sin_ref, cost_ref, sint_ref, *, pos_off):
    rows = cos_ref.shape[0]
    pos = (lax.broadcasted_iota(I32, (rows, LANES), 0) + pos_off).astype(F32)
    ang = pos * inv_ref[...]
    c = jnp.cos(ang)
    s = jnp.sin(ang)
    cos_ref[...] = c
    sin_ref[...] = s
    cost_ref[...] = c.T
    sint_ref[...] = s.T


def _rope_tables(rows, pos_off):
    inv = ROPE_BASE ** (-jnp.arange(0, IDX_ROPE, 2, dtype=F32) / IDX_ROPE)
    l64 = np.arange(LANES) % D_IDX
    inv_row = jnp.where(l64 < IDX_ROPE, inv[l64 % (IDX_ROPE // 2)], 0.0).astype(F32)[None, :]
    return pl.pallas_call(
        functools.partial(_rope_table_kernel, pos_off=pos_off),
        out_shape=(jax.ShapeDtypeStruct((rows, LANES), F32),) * 2 + (jax.ShapeDtypeStruct((LANES, rows), F32),) * 2,
    )(inv_row)


def _inproj_kernel(x_ref, g_ref, wr_ref, wqt_ref, wqih_ref, wqil_ref, wsh_ref, wsb_ref,
                   cos_ref, sin_ref, cost_ref, sint_ref,
                   qet_ref, qot_ref, k3_ref, kb_ref, v_ref, v3_ref, vpt_ref, ga_ref, z_ref, xbc_ref, qit_ref, sm_ref,
                   smt_ref):
    x = x_ref[...]
    hn = x * lax.rsqrt(jnp.mean(x * x, axis=-1, keepdims=True) + EPS) * g_ref[...]
    hb, hlo = _split(hn)
    tm = x.shape[0]

    def rows(lo, width):
        return _dot(hb, wr_ref[:, lo:lo + width])

    k = rows(0, ATT_WIDTH)
    kb_ref[...] = k.astype(BF16)
    v = rows(ATT_WIDTH, ATT_WIDTH)
    v_ref[...] = v
    for h in range(N_HEADS_A):
        k3_ref[:, h, :] = k[:, h * HEAD_DIM:(h + 1) * HEAD_DIM]
        v3_ref[:, h, :] = v[:, h * HEAD_DIM:(h + 1) * HEAD_DIM]
    ga_ref[...] = rows(2 * ATT_WIDTH, ATT_WIDTH)
    z_ref[...] = rows(3 * ATT_WIDTH, SSM_WIDTH)
    xbc_ref[...] = rows(3 * ATT_WIDTH + SSM_WIDTH, CONV_CH)

    vt = v.T
    ones = jnp.ones((HEAD_DIM, tm), BF16)
    for h in range(N_HEADS_A):
        vpt_ref[h * LANES:h * LANES + HEAD_DIM, :] = vt[h * HEAD_DIM:(h + 1) * HEAD_DIM, :].astype(BF16)
        vpt_ref[h * LANES + HEAD_DIM:(h + 1) * LANES, :] = ones

    qt = _nt(wqt_ref[...], hb) * (HEAD_DIM ** -0.5 * LOG2E)
    even = (lax.broadcasted_iota(I32, qt.shape, 0) & HEAD_DIM) == 0
    qet_ref[...] = jnp.where(even, qt, 0.0).astype(BF16)
    qot_ref[...] = jnp.where(even, 0.0, qt).astype(BF16)

    qit = _nt(wqih_ref[...], hb) + (_nt(wqih_ref[...], hlo) + _nt(wqil_ref[...], hb))
    ct = cost_ref[0:D_IDX, :]
    st = sint_ref[0:D_IDX, :]
    first_t = lax.broadcasted_iota(I32, (D_IDX, tm), 0) < IDX_ROPE // 2
    s1t = jnp.where(first_t, -st, 0.0)
    s2t = jnp.where(first_t, 0.0, st)
    for h in range(N_IDX_HEADS):
        xh = qit[h * D_IDX:(h + 1) * D_IDX, :]
        qit_ref[h * D_IDX:(h + 1) * D_IDX, :] = (xh * ct + pltpu.roll(xh, D_IDX - IDX_ROPE // 2, 0) * s1t
                                                 + pltpu.roll(xh, IDX_ROPE // 2, 0) * s2t)

    both = _dot(hb, wsb_ref[...])
    sm = both[:, :LANES] + (both[:, LANES:] + _dot(hlo, wsh_ref[...]))
    lane = lax.broadcasted_iota(I32, (tm, LANES), 1)
    is_ki = lane < D_IDX
    first = (lane & (D_IDX - 1)) < IDX_ROPE // 2
    c = jnp.where(is_ki, cos_ref[...], 1.0)
    s = jnp.where(is_ki, sin_ref[...], 0.0)
    sm = (sm * c + pltpu.roll(sm, LANES - IDX_ROPE // 2, 1) * jnp.where(first, -s, 0.0)
          + pltpu.roll(sm, IDX_ROPE // 2, 1) * jnp.where(first, 0.0, s))
    is_wi = (lane >= SM_WI) & (lane < SM_DT)
    sm = jnp.where(is_wi, sm * (N_IDX_HEADS ** -0.5), sm)
    sm_ref[...] = sm
    smt_ref[...] = sm.T


def _inproj(x2d, lw, tables, tm):
    n = x2d.shape[0]
    cos_t, sin_t, cos_tt, sin_tt = tables
    tab_blocks = cos_t.shape[0] // tm
    row = lambda i: (i, 0)
    col = lambda i: (0, i)
    const = lambda i: (0, 0)
    rows = lambda w, dt: (jax.ShapeDtypeStruct((n, w), dt), pl.BlockSpec((tm, w), row))
    cols = lambda w, dt: (jax.ShapeDtypeStruct((w, n), dt), pl.BlockSpec((w, tm), col))
    full = lambda a: pl.BlockSpec(a.shape, const)
    heads = (jax.ShapeDtypeStruct((n, N_HEADS_A, HEAD_DIM), F32),
             pl.BlockSpec((tm, N_HEADS_A, HEAD_DIM), lambda i: (i, 0, 0)))
    outs = [cols(ATT_WIDTH, BF16), cols(ATT_WIDTH, BF16), heads, rows(ATT_WIDTH, BF16),
            rows(ATT_WIDTH, F32), heads, cols(VP_ROWS, BF16), rows(ATT_WIDTH, F32), rows(SSM_WIDTH, F32),
            rows(CONV_CH, F32), cols(QI_WIDTH, F32), rows(LANES, F32), cols(LANES, F32)]
    weights = [lw["g_pre"], lw["w_rows"], lw["wq_t"], lw["wqi_t_hi"], lw["wqi_t_lo"], lw["ws_hi"], lw["ws_both"]]
    return pl.pallas_call(
        _inproj_kernel,
        grid=(n // tm,),
        in_specs=[pl.BlockSpec((tm, D_MODEL), row)] + [full(w) for w in weights]
                 + [pl.BlockSpec((tm, LANES), lambda i: (i % tab_blocks, 0))] * 2
                 + [pl.BlockSpec((LANES, tm), lambda i: (0, i % tab_blocks))] * 2,
        out_specs=[o[1] for o in outs],
        out_shape=[o[0] for o in outs],
        compiler_params=_cparams(("arbitrary",)),
    )(x2d, *weights, cos_t, sin_t, cos_tt, sin_tt)


def _bucket(dist):
    max_exact = NUM_BUCKETS // 2
    n = jnp.maximum(dist, 0)
    nf = jnp.maximum(n, max_exact).astype(F32)
    large = max_exact + jnp.floor(jnp.log(nf / max_exact) / math.log(MAX_DISTANCE / max_exact)
                                  * (NUM_BUCKETS - max_exact)).astype(I32)
    large = jnp.minimum(large, NUM_BUCKETS - 1)
    return jnp.where(n < max_exact, n, large)


def _bias_lookup(bucket, relb_ref, h):
    out = jnp.full(bucket.shape, relb_ref[0, h], F32)
    for b in range(1, NUM_BUCKETS):
        out = jnp.where(bucket == b, relb_ref[b, h], out)
    return out


KEY_NEG_INF = INT_MIN + 0x7FFFFF


def _bit(n):
    return lax.shift_left(jnp.int32(1), jnp.asarray(n, I32))


def _key_to_float(key):
    return pltpu.bitcast(jnp.where(key < 0, key ^ 0x7FFFFFFF, key), F32)


def _search_threshold(count_ge, n_top, n_keys, shape):
    def body(it, thr):
        cand = thr ^ _bit(31 - it)
        cnt = jnp.where(cand < KEY_NEG_INF, jnp.asarray(n_keys, F32), count_ge(_key_to_float(cand)))
        return jnp.where(cnt >= float(n_top), cand, thr)

    return _key_to_float(lax.fori_loop(0, 32, body, jnp.full(shape, INT_MIN, I32)))


def _bf16_step_bits(u):
    b = u - 32768
    return lax.shift_left(jnp.where(b < 0, b ^ 0x7FFF, b), jnp.int32(16))


def _float_image(bits):
    return jnp.where(bits < 0, bits ^ 0x7FFFFFFF, bits)


def _search_threshold_2level(count_ge_rounded, count_ge, n_top, n_keys, shape):
    k = float(n_top)
    n_all = jnp.asarray(n_keys, F32)

    def admits(u, counter):
        bits = _bf16_step_bits(u)
        return jnp.where(_float_image(bits) < KEY_NEG_INF, n_all, counter(pltpu.bitcast(bits, F32))) >= k

    def coarse(it, u):
        cand = u | _bit(15 - it)
        return jnp.where(admits(cand, count_ge_rounded), cand, u)

    u = lax.fori_loop(0, 16, coarse, jnp.zeros(shape, I32))
    u = jnp.where(admits(u, count_ge), u, u - 1)
    base = _float_image(_bf16_step_bits(u))

    def fine(it, d):
        cand = d | _bit(15 - it)
        return jnp.where(count_ge(_key_to_float(base + cand)) >= k, cand, d)

    return _key_to_float(base + lax.fori_loop(0, 16, fine, jnp.zeros(shape, I32)))


def _search_last_tie(count_ties_before, need, pos_bits, shape):
    def body(it, q):
        cand = q | _bit(pos_bits - 1 - it)
        return jnp.where(count_ties_before(cand) < need, cand, q)

    return lax.fori_loop(0, pos_bits, body, jnp.zeros(shape, I32))


def _select_madd(score, pos, thr, last):
    return jnp.where(score > thr, 0.0, jnp.where(score == thr, jnp.where(pos <= last, 0.0, NEG), NEG))


def _bias_tiles_kernel(relb_ref, o_ref, *, tq):
    ki = lax.broadcasted_iota(I32, (tq, tq), 0)
    qi = lax.broadcasted_iota(I32, (tq, tq), 1)
    for kind in range(2):
        bucket = _bucket(qi - ki + kind * tq)
        for h in range(N_HEADS_A):
            o_ref[h, kind] = (_bias_lookup(bucket, relb_ref, h) - relb_ref[NUM_BUCKETS - 1, h]) * LOG2E


def _bias_tiles(rel_bias, tq):
    return pl.pallas_call(
        functools.partial(_bias_tiles_kernel, tq=tq),
        in_specs=[pl.BlockSpec(memory_space=pltpu.SMEM)],
        out_shape=jax.ShapeDtypeStruct((N_HEADS_A, 2, tq, tq), F32),
        compiler_params=pltpu.CompilerParams(vmem_limit_bytes=VMEM_LIMIT),
    )(rel_bias)


KI3_BUILD_ROWS = 512


def _pattn_kernel(qit_ref, smtq_ref, sm_ref, qet_ref, qot_ref, kb_ref, vpt_ref, bt_ref, o_ref,
                  ki3_ref, qh3_ref, sc_ref, hi_ref, last_ref, m_ref, acc_ref, *, tq, n_top, pos_bits):
    i = pl.program_id(1)
    nch = i + 1
    s_len = sc_ref.shape[0]
    kiota = lax.broadcasted_iota(I32, (tq, tq), 0)
    qpos = i * tq + lax.broadcasted_iota(I32, (tq, tq), 1)

    def rows(c, width=tq):
        return pl.ds(pl.multiple_of(c * tq, tq), width)

    @pl.when(i == 0)
    def _():
        low = lax.broadcasted_iota(I32, (KI3_BUILD_ROWS, LANES), 1) < D_IDX

        def body(r, carry):
            sl = pl.ds(pl.multiple_of(r * KI3_BUILD_ROWS, KI3_BUILD_ROWS), KI3_BUILD_ROWS)
            x = sm_ref[sl, :]
            hi = x.astype(BF16).astype(F32)
            ki3_ref[sl, 0:LANES] = jnp.where(low, hi, pltpu.roll(hi, D_IDX, 1)).astype(BF16)
            ki3_ref[sl, LANES:2 * LANES] = jnp.where(low, x - hi, 0.0).astype(BF16)
            return carry
        lax.fori_loop(0, s_len // KI3_BUILD_ROWS, body, 0)

    for h in range(N_IDX_HEADS):
        hi, lo = _split(qit_ref[h * D_IDX:(h + 1) * D_IDX, :])
        qh3_ref[h, 0:D_IDX, :] = hi
        qh3_ref[h, D_IDX:2 * D_IDX, :] = lo
        qh3_ref[h, 2 * D_IDX:3 * D_IDX, :] = hi
        qh3_ref[h, 3 * D_IDX:, :] = jnp.zeros((D_IDX, tq), BF16)
    w8 = smtq_ref[SM_WI:SM_WI + N_IDX_HEADS, :] * (D_IDX ** -0.5)

    def score_body(c2, carry):
        blocks = [2 * c2, jnp.minimum(2 * c2 + 1, i)]
        dots = [[_dot(ki3_ref[rows(c), :], qh3_ref[h]) for h in range(N_IDX_HEADS)] for c in blocks]
        for c, d in zip(blocks, dots):
            terms = [jnp.maximum(d[h], 0.0) * w8[h:h + 1, :] for h in range(N_IDX_HEADS)]
            while len(terms) > 1:
                terms = [terms[j] + terms[j + 1] for j in range(0, len(terms), 2)]
            sc = jnp.where(c * tq + kiota <= qpos, terms[0], -jnp.inf)
            sc_ref[rows(c), :] = sc
            hi_ref[rows(c), :] = sc.astype(BF16)
        return carry

    lax.fori_loop(0, (nch + 1) // 2, score_body, 0)

    def over_keys(x, op):
        x = x.reshape(x.shape[0] // SUBLANES, SUBLANES, tq)
        while x.shape[0] > 1:
            half = x.shape[0] // 2
            x = op(x[:half], x[half:])
        return x[0]

    def count(pred):
        def body(c, acc):
            hit = jnp.where(pred(sc_ref[rows(c), :], c * tq + kiota), 1.0, 0.0)
            return acc + over_keys(hit, jnp.add)
        acc = lax.fori_loop(0, nch, body, jnp.zeros((SUBLANES, tq), F32))
        return jnp.broadcast_to(jnp.sum(acc, axis=0, keepdims=True), (SUBLANES, tq))

    def count_rounded(t):
        t16 = jnp.concatenate([t, t], axis=0).astype(BF16)

        def body(c, acc):
            h = hi_ref[rows(c), :].reshape(tq // BF16_ROWS, BF16_ROWS, tq)
            hit = jnp.where(h >= t16[None], jnp.ones_like(h), jnp.zeros_like(h))
            while hit.shape[0] > 1:
                half = hit.shape[0] // 2
                hit = hit[:half] + hit[half:]
            part = hit[0].astype(F32)
            return acc + (part[0:SUBLANES] + part[SUBLANES:])
        acc = lax.fori_loop(0, nch, body, jnp.zeros((SUBLANES, tq), F32))
        return jnp.broadcast_to(jnp.sum(acc, axis=0, keepdims=True), (SUBLANES, tq))

    thr = _search_threshold_2level(count_rounded, lambda t: count(lambda sc, pos: sc >= t[0:1, :]),
                                   n_top, nch * tq, (SUBLANES, tq))
    thr_row = thr[0:1, :]
    kept = count(lambda sc, pos: sc >= thr_row)
    last_ref[...] = jnp.full((SUBLANES, tq), 2 ** pos_bits - 1, I32)

    @pl.when(jnp.max(kept) > float(n_top))
    def _():
        need = float(n_top) - count(lambda sc, pos: sc > thr_row)
        last_ref[...] = _search_last_tie(
            lambda q: count(lambda sc, pos: (sc == thr_row) & (pos < q[0:1, :])), need, pos_bits, (SUBLANES, tq))

    last_row = last_ref[0:1, :]

    def madd_body(c, carry):
        pos = c * tq + kiota
        madd = _select_madd(sc_ref[rows(c), :], pos, thr_row, last_row)
        sc_ref[rows(c), :] = jnp.where(pos <= qpos, madd, NEG)
        return carry

    lax.fori_loop(0, nch, madd_body, 0)

    m_ref[...] = jnp.full(m_ref.shape, NEG, F32)
    acc_ref[...] = jnp.zeros(acc_ref.shape, F32)

    def attend(c0, width, bias_of_head):
        sl = rows(c0, width)
        madd = sc_ref[sl, :]
        logits = []
        for h in range(N_HEADS_A):
            p2 = h // 2
            qt = (qet_ref if h % 2 == 0 else qot_ref)[p2 * LANES:(p2 + 1) * LANES, :]
            logits.append(_dot(kb_ref[sl, p2 * LANES:(p2 + 1) * LANES], qt))
        probs, alphas = [], []
        for h in range(N_HEADS_A):
            s = logits[h] + madd
            if bias_of_head is not None:
                s = s + bias_of_head(h)
            m_prev = m_ref[h]
            cmax = over_keys(s, jnp.maximum)
            m_new = jnp.maximum(m_prev, jnp.broadcast_to(jnp.max(cmax, axis=0, keepdims=True), (SUBLANES, tq)))
            probs.append(jnp.exp2(s - m_new[0:1, :]).astype(BF16))
            alphas.append(jnp.exp2(m_prev - m_new)[0:1, :])
            m_ref[h] = m_new
        for h in range(N_HEADS_A):
            pv = _dot(vpt_ref[h * LANES:(h + 1) * LANES, sl], probs[h])
            acc_ref[h] = alphas[h] * acc_ref[h] + pv

    n_far = jnp.maximum(i - 1, 0)

    def far_body(c, carry):
        attend(2 * c, 2 * tq, None)
        return carry

    lax.fori_loop(0, n_far // 2, far_body, 0)

    @pl.when(n_far % 2 == 1)
    def _():
        attend(n_far - 1, tq, None)

    @pl.when(i >= 1)
    def _():
        attend(i - 1, tq, lambda h: bt_ref[h, 1])

    attend(i, tq, lambda h: bt_ref[h, 0])

    outs = []
    for h in range(N_HEADS_A):
        acc = acc_ref[h]
        outs.append(acc[0:HEAD_DIM, :] / acc[HEAD_DIM:, :])
    o_ref[...] = jnp.concatenate(outs, axis=0).T


def _prompt_attention(proj, bias_t, nb, s, tq):
    nq = s // tq
    n_top = min(TOPK_MAX, s // 4)
    pos_bits = max(1, (s - 1).bit_length())
    qcols = lambda b, i: (0, b * nq + i)
    seq_rows = lambda b, i: (b, 0)
    seq_cols = lambda b, i: (0, b)
    return pl.pallas_call(
        functools.partial(_pattn_kernel, tq=tq, n_top=n_top, pos_bits=pos_bits),
        grid=(nb, nq),
        in_specs=[pl.BlockSpec((QI_WIDTH, tq), qcols), pl.BlockSpec((LANES, tq), qcols),
                  pl.BlockSpec((s, LANES), seq_rows),
                  pl.BlockSpec((ATT_WIDTH, tq), qcols), pl.BlockSpec((ATT_WIDTH, tq), qcols),
                  pl.BlockSpec((s, ATT_WIDTH), seq_rows), pl.BlockSpec((VP_ROWS, s), seq_cols),
                  pl.BlockSpec((N_HEADS_A, 2, tq, tq), lambda b, i: (0, 0, 0, 0),
                               pipeline_mode=pl.Buffered(1))],
        out_specs=pl.BlockSpec((tq, ATT_WIDTH), lambda b, i: (b * nq + i, 0)),
        out_shape=jax.ShapeDtypeStruct((nb * s, ATT_WIDTH), F32),
        scratch_shapes=[pltpu.VMEM((s, SPLIT3), BF16), pltpu.VMEM((N_IDX_HEADS, SPLIT3, tq), BF16),
                        pltpu.VMEM((s, tq), F32), pltpu.VMEM((s, tq), BF16), pltpu.VMEM((SUBLANES, tq), I32),
                        pltpu.VMEM((N_HEADS_A, SUBLANES, tq), F32), pltpu.VMEM((N_HEADS_A, LANES, tq), F32)],
        compiler_params=_cparams(("arbitrary", "arbitrary")),
    )(proj["qi_t"], proj["small_t"], proj["small"], proj["qe_t"], proj["qo_t"], proj["k_b"], proj["vp_t"], bias_t)


def _ssd_kernel(xbc_ref, z_ref, sm_ref, cw_ref, cb_ref, dtb_ref, alog_ref, dsk_ref, gs_ref, ex_ref, ext_ref,
                h0_ref, c0_ref, att_ref, ga_ref, x_ref, wt_ref, wb_ref, gp_ref, y_ref, hf_ref, xp_ref, st_ref, *,
                t_valid):
    c = pl.program_id(1)
    L = CHUNK

    @pl.when(c == 0)
    def _():
        st_ref[...] = h0_ref[...]
        xp_ref[0:SUBLANES, :] = c0_ref[...]

    xp_ref[SUBLANES:SUBLANES + L, :] = xbc_ref[...]
    conv = cb_ref[...]
    for j in range(CONV_W):
        lo = SUBLANES - (CONV_W - 1) + j
        conv = conv + xp_ref[lo:lo + L, :] * cw_ref[j:j + 1, :]
    xp_ref[0:SUBLANES, :] = xp_ref[L:L + SUBLANES, :]
    act = conv * _sigmoid(conv)
    xs = act[:, :SSM_WIDTH]
    bm = act[:, SSM_WIDTH:SSM_WIDTH + SSM_GROUPS * D_STATE]
    cm = act[:, SSM_WIDTH + SSM_GROUPS * D_STATE:]

    raw = sm_ref[...] + dtb_ref[...]
    dtf = jnp.maximum(raw, 0.0) + jnp.log1p(jnp.exp(-jnp.abs(raw)))
    row = lax.broadcasted_iota(I32, (L, LANES), 0)
    if t_valid < L:
        dtf = jnp.where(row < t_valid, dtf, 0.0)
    adt = dtf * (-jnp.exp(alog_ref[...]))
    tril = row >= lax.broadcasted_iota(I32, (L, LANES), 1)
    tril01 = jnp.where(tril, 1.0, 0.0).astype(BF16)
    cs = sum(_dot(tril01, part) for part in _split3(adt))
    ex = ex_ref[...]
    dtx = sum(_dot(part, ex) for part in _split3(dtf))
    csx = sum(_dot(part, ex) for part in _split3(cs))
    cst = cs.T
    x = xs * dtx
    w = x * jnp.exp(csx[L - 1:L, :] - csx)
    ecsx = jnp.exp(csx)
    dec = jnp.exp(jnp.sum(ext_ref[...] * cs[L - 1:L, :], axis=1, keepdims=True))
    low = lax.broadcasted_iota(I32, (L, LANES), 1) < SSM_HEAD_DIM

    ys = []
    for p2 in range(SSM_HEADS // 2):
        g = (2 * p2) // (SSM_HEADS // SSM_GROUPS)
        cg = cm[:, g * D_STATE:(g + 1) * D_STATE].astype(BF16)
        bg = bm[:, g * D_STATE:(g + 1) * D_STATE].astype(BF16)
        cb_mat = _nt(cg, bg)
        lanes = slice(p2 * LANES, (p2 + 1) * LANES)
        xp = x[:, lanes].astype(BF16)
        yd = []
        for h in (2 * p2, 2 * p2 + 1):
            diff = cs[:, SM_DT + h:SM_DT + h + 1] - cst[SM_DT + h:SM_DT + h + 1, :]
            lm = jnp.exp(jnp.where(tril, diff, NEG))
            yd.append(_dot((cb_mat * lm).astype(BF16), xp))
        rows = slice(p2 * LANES, (p2 + 1) * LANES)
        st = st_ref[rows, :]
        y_off = _nt(cg, st.astype(BF16)) * ecsx[:, lanes]
        ys.append(jnp.where(low, yd[0], yd[1]) + y_off)
        upd = _dot(w[:, lanes].T.astype(BF16), bg)
        st_ref[rows, :] = st * dec[rows, :] + upd

    y = jnp.concatenate(ys, axis=1) + dsk_ref[...] * xs
    zz = z_ref[...]
    gated = y * (zz * _sigmoid(zz))
    ssm = gated * lax.rsqrt(jnp.mean(gated * gated, axis=-1, keepdims=True) + EPS) * gs_ref[...]

    ga = ga_ref[...]
    att = att_ref[...] * (ga * _sigmoid(ga))
    out = _dot(att.astype(BF16), wt_ref[...]) + _dot(ssm.astype(BF16), wb_ref[...])
    y_ref[...] = x_ref[...] + out * lax.rsqrt(jnp.mean(out * out, axis=-1, keepdims=True) + EPS) * gp_ref[...]

    @pl.when(c == pl.num_programs(1) - 1)
    def _():
        hf_ref[...] = st_ref[...]


def _ssd_out(proj, att, x2d, lw, h0, c0, nb, s, t_valid):
    xbc, z, small, ga = proj["xbc"], proj["z"], proj["small"], proj["ga"]
    nc = s // CHUNK
    blk = lambda b, c: (b * nc + c, 0)
    const = lambda b, c: (0, 0)
    per_b = lambda b, c: (b, 0, 0)
    lanes = np.arange(LANES)
    dt_row = lambda v: jnp.zeros((1, LANES), F32).at[0, SM_DT:SM_DT + SSM_HEADS].set(v)
    expand = (lanes[:, None] == SM_DT + np.arange(SSM_WIDTH)[None, :] // SSM_HEAD_DIM).astype(np.float32)
    state_rows = SSM_HEADS * SSM_HEAD_DIM
    return pl.pallas_call(
        functools.partial(_ssd_kernel, t_valid=t_valid),
        grid=(nb, nc),
        in_specs=[pl.BlockSpec((CHUNK, CONV_CH), blk), pl.BlockSpec((CHUNK, SSM_WIDTH), blk),
                  pl.BlockSpec((CHUNK, LANES), blk),
                  pl.BlockSpec((CONV_W, CONV_CH), const), pl.BlockSpec((1, CONV_CH), const),
                  pl.BlockSpec((1, LANES), const), pl.BlockSpec((1, LANES), const),
                  pl.BlockSpec((1, SSM_WIDTH), const), pl.BlockSpec((1, SSM_WIDTH), const),
                  pl.BlockSpec((LANES, SSM_WIDTH), const), pl.BlockSpec((SSM_WIDTH, LANES), const),
                  pl.BlockSpec((None, state_rows, D_STATE), per_b),
                  pl.BlockSpec((None, SUBLANES, CONV_CH), per_b),
                  pl.BlockSpec((CHUNK, ATT_WIDTH), blk), pl.BlockSpec((CHUNK, ATT_WIDTH), blk),
                  pl.BlockSpec((CHUNK, D_MODEL), blk),
                  pl.BlockSpec((ATT_WIDTH, D_MODEL), const), pl.BlockSpec((SSM_WIDTH, D_MODEL), const),
                  pl.BlockSpec((1, D_MODEL), const)],
        out_specs=[pl.BlockSpec((CHUNK, D_MODEL), blk), pl.BlockSpec((None, state_rows, D_STATE), per_b)],
        out_shape=[jax.ShapeDtypeStruct((nb * s, D_MODEL), F32),
                   jax.ShapeDtypeStruct((nb, state_rows, D_STATE), F32)],
        scratch_shapes=[pltpu.VMEM((CHUNK + SUBLANES, CONV_CH), F32), pltpu.VMEM((state_rows, D_STATE), F32)],
        compiler_params=_cparams(("arbitrary", "arbitrary")),
    )(xbc, z, small, lw["conv_w"], lw["conv_b"][None, :], dt_row(lw["dt_bias"]), dt_row(lw["a_log"]),
      jnp.repeat(lw["d_skip"], SSM_HEAD_DIM)[None, :], lw["g_ssm"][None, :], jnp.asarray(expand, BF16),
      jnp.asarray(expand.T), h0, c0, att, ga, x2d, lw["w_top"], lw["w_bot"], lw["g_post"])


SCORE_PAGES = 128
ATTN_PAGES = 32
ROWS_Q = N_HEADS_A * SUBLANES
COUNT_CHAINS = 4
SELECT_GROUP = 4


def _sscore_kernel(pt_ref, qall3_ref, wcol_ref, smt_ref, *rest, past, t_new, n_top, pos_bits, group):
    pages = rest[:SCORE_PAGES]
    madd_ref = rest[SCORE_PAGES]
    sc_ref, last_ref = rest[SCORE_PAGES + 1:]
    j = pl.program_id(1)
    member = pl.program_id(0) % group
    mine = pl.ds(pl.multiple_of(member * SUBLANES, SUBLANES), SUBLANES)
    rows_g = group * SUBLANES
    kw = SCORE_PAGES * PAGE_SIZE
    total = past + LANES
    qall3 = qall3_ref[...]
    wcol = wcol_ref[...]
    zeros = jnp.zeros((D_IDX, PAGE_SIZE), BF16)

    def dots(kt):
        hi, lo = _split(kt)
        return _dot(qall3, jnp.concatenate([hi, hi, lo, zeros], axis=0))

    def weigh(d):
        r = jnp.maximum(d * (D_IDX ** -0.5), 0.0) * wcol
        sc = r[0:SUBLANES]
        for h in range(1, N_IDX_HEADS):
            sc = sc + r[h * SUBLANES:(h + 1) * SUBLANES]
        return sc

    def scores(kt):
        return weigh(dots(kt))

    page_dots = [dots(page[...]) for page in pages]
    for r, d in enumerate(page_dots):
        sl = pl.ds(pl.multiple_of(j * kw + r * PAGE_SIZE, PAGE_SIZE), PAGE_SIZE)
        sc_ref[mine, sl] = weigh(d)

    @pl.when(j == pl.num_programs(1) - 1)
    def _():
        lane8 = lax.broadcasted_iota(I32, (SUBLANES, LANES), 1)
        row8 = lax.broadcasted_iota(I32, (SUBLANES, LANES), 0)
        vis8 = (lane8 <= row8) & (lane8 < t_new)
        sc_ref[mine, past:total] = jnp.where(vis8, scores(smt_ref[0:D_IDX, :]), -jnp.inf)

    @pl.when(jnp.logical_and(j == pl.num_programs(1) - 1, member == group - 1))
    def _():
        shape = (rows_g, LANES)
        lane = lax.broadcasted_iota(I32, shape, 1)
        tok = lax.broadcasted_iota(I32, shape, 0) % SUBLANES
        vis = (lane <= tok) & (lane < t_new)

        def count(pred):
            accs = [jnp.zeros(shape, F32)] * COUNT_CHAINS
            for t in range(total // LANES):
                hit = pred(sc_ref[:, t * LANES:(t + 1) * LANES], t * LANES + lane)
                accs[t % COUNT_CHAINS] = accs[t % COUNT_CHAINS] + jnp.where(hit, 1.0, 0.0)
            while len(accs) > 1:
                accs = [accs[n] + accs[n + 1] for n in range(0, len(accs), 2)]
            return jnp.broadcast_to(jnp.sum(accs[0], axis=1, keepdims=True), shape)

        thr = _search_threshold(lambda t: count(lambda sc, pos: sc >= t), n_top, total, shape)
        kept = count(lambda sc, pos: sc >= thr)
        last_ref[...] = jnp.full(shape, 2 ** pos_bits - 1, I32)

        @pl.when(jnp.max(kept) > float(n_top))
        def _():
            need = float(n_top) - count(lambda sc, pos: sc > thr)
            last_ref[...] = _search_last_tie(
                lambda q: count(lambda sc, pos: (sc == thr) & (pos < q)), need, pos_bits, shape)

        last = last_ref[...]
        for t in range(total // LANES):
            sl = slice(t * LANES, (t + 1) * LANES)
            pos = t * LANES + lane
            madd = _select_madd(sc_ref[:, sl], pos, thr, last)
            if t * LANES >= past:
                madd = jnp.where(vis, madd, NEG)
            madd_ref[:, sl] = madd


def _sattn_kernel(pt_ref, relb_ref, qbd_ref, madd_ref, maddn_ref, kbnew_ref, vnew_ref, *rest, past):
    kpages = rest[:ATTN_PAGES]
    vpages = rest[ATTN_PAGES:2 * ATTN_PAGES]
    o_ref, m_ref, l_ref, acc_ref = rest[2 * ATTN_PAGES:]
    j = pl.program_id(1)
    kw = ATTN_PAGES * PAGE_SIZE

    @pl.when(j == 0)
    def _():
        m_ref[...] = jnp.full(m_ref.shape, NEG, F32)
        l_ref[...] = jnp.zeros(l_ref.shape, F32)
        acc_ref[...] = jnp.zeros(acc_ref.shape, F32)

    qbd = qbd_ref[...]

    def far_bias(width):
        row_head = lax.broadcasted_iota(I32, (ROWS_Q, width), 0) // SUBLANES
        out = jnp.full((ROWS_Q, width), relb_ref[NUM_BUCKETS - 1, 0], F32)
        for h in range(1, N_HEADS_A):
            out = jnp.where(row_head == h, relb_ref[NUM_BUCKETS - 1, h], out)
        return out

    def near_bias(width, pos0):
        tok = lax.broadcasted_iota(I32, (SUBLANES, width), 0)
        pos = pos0 + lax.broadcasted_iota(I32, (SUBLANES, width), 1)
        bucket = _bucket(past + tok - pos)
        return jnp.concatenate([_bias_lookup(bucket, relb_ref, h) for h in range(N_HEADS_A)], axis=0)

    def update(logits, bias, madd8, pv_fn):
        s = logits + (bias * LOG2E + jnp.concatenate([madd8] * N_HEADS_A, axis=0))
        m_prev = m_ref[...]
        m_new = jnp.maximum(m_prev, jnp.broadcast_to(jnp.max(s, axis=1, keepdims=True), (ROWS_Q, LANES)))
        alpha = jnp.exp2(m_prev - m_new)
        p = jnp.exp2(s - m_new[:, :1])
        l_ref[...] = alpha * l_ref[...] + jnp.broadcast_to(jnp.sum(p, axis=1, keepdims=True), (ROWS_Q, LANES))
        acc_ref[...] = alpha[:, :1] * acc_ref[...] + pv_fn(p.astype(BF16))
        m_ref[...] = m_new

    def paged(bias):
        logits = jnp.concatenate([_dot(qbd, kp[...].astype(BF16)) for kp in kpages], axis=1)

        def pv_fn(p):
            pv = _nt(p[:, 0:PAGE_SIZE], vpages[0][...].astype(BF16))
            for r in range(1, ATTN_PAGES):
                pv = pv + _nt(p[:, r * PAGE_SIZE:(r + 1) * PAGE_SIZE], vpages[r][...].astype(BF16))
            return pv

        update(logits, bias, madd_ref[...], pv_fn)

    far = (j + 1) * kw + MAX_DISTANCE <= past + 1

    @pl.when(far)
    def _():
        paged(far_bias(kw))

    @pl.when(jnp.logical_not(far))
    def _():
        paged(near_bias(kw, j * kw))

    @pl.when(j == pl.num_programs(1) - 1)
    def _():
        update(_nt(qbd, kbnew_ref[...]), near_bias(LANES, past), maddn_ref[...],
               lambda p: _dot(p, vnew_ref[...].astype(BF16)))
        o = acc_ref[...] / l_ref[:, :1]
        own = (lax.broadcasted_iota(I32, (ROWS_Q, ATT_WIDTH), 0) // SUBLANES
               == lax.broadcasted_iota(I32, (ROWS_Q, ATT_WIDTH), 1) // HEAD_DIM)
        o = jnp.where(own, o, 0.0)
        out = o[0:SUBLANES]
        for h in range(1, N_HEADS_A):
            out = out + o[h * SUBLANES:(h + 1) * SUBLANES]
        o_ref[...] = out


def _sample_attention(page_table, rel_bias, proj, cache_k, cache_v, cache_kidx, nb, t_new):
    n_pages = page_table.shape[1]
    past = n_pages * PAGE_SIZE
    total = past + LANES
    n_top = min(TOPK_MAX, (past + t_new) // 4)
    pos_bits = max(1, (total - 1).bit_length())
    pt = page_table.reshape(-1)
    pool = cache_kidx.shape[0]

    def tok(a_t):
        return a_t.reshape(a_t.shape[0], nb, CHUNK)[:, :, :SUBLANES].transpose(1, 2, 0)

    by_head = lambda a, d: a.reshape(nb, SUBLANES, -1, d).transpose(0, 2, 1, 3)
    qall = by_head(tok(proj["qi_t"]), D_IDX).reshape(nb, ROWS_Q, D_IDX)
    hi, lo = _split(qall)
    qall3 = jnp.concatenate([hi, lo, hi, jnp.zeros_like(hi)], axis=-1)
    wi_t = tok(proj["small_t"])[:, :, SM_WI:SM_WI + N_IDX_HEADS].transpose(0, 2, 1).reshape(nb, ROWS_Q, 1)
    wcol = jnp.broadcast_to(wi_t, (nb, ROWS_Q, LANES))
    q_t = by_head(tok(proj["qe_t"] + proj["qo_t"]), HEAD_DIM)
    eye = jnp.eye(N_HEADS_A, dtype=q_t.dtype)
    qbd = (q_t[:, :, :, None, :] * eye[None, :, None, :, None]).reshape(nb, ROWS_Q, ATT_WIDTH)
    ckt = cache_k.transpose(0, 2, 3, 1).reshape(pool, ATT_WIDTH, PAGE_SIZE)
    cvt = cache_v.transpose(0, 2, 3, 1).reshape(pool, ATT_WIDTH, PAGE_SIZE)
    cit = cache_kidx.transpose(0, 2, 1)

    def page(r, per_step):
        return lambda b, j, pt_ref: (pt_ref[b * n_pages + j * per_step + r], 0, 0)

    seq3 = lambda b, j, pt_ref: (b, 0, 0)
    tcol = lambda b, j, pt_ref: (0, b)
    trow = lambda b, j, pt_ref: (b, 0)

    group = math.gcd(nb, SELECT_GROUP)
    madd = pl.pallas_call(
        functools.partial(_sscore_kernel, past=past, t_new=t_new, n_top=n_top, pos_bits=pos_bits, group=group),
        grid_spec=pltpu.PrefetchScalarGridSpec(
            num_scalar_prefetch=1, grid=(nb, n_pages // SCORE_PAGES),
            in_specs=[pl.BlockSpec((None, ROWS_Q, SPLIT3), seq3), pl.BlockSpec((None, ROWS_Q, LANES), seq3),
                      pl.BlockSpec((LANES, CHUNK), tcol)]
                     + [pl.BlockSpec((None, D_IDX, PAGE_SIZE), page(r, SCORE_PAGES)) for r in range(SCORE_PAGES)],
            out_specs=pl.BlockSpec((None, group * SUBLANES, total), lambda b, j, pt_ref: (b // group, 0, 0)),
            scratch_shapes=[pltpu.VMEM((group * SUBLANES, total), F32),
                            pltpu.VMEM((group * SUBLANES, LANES), I32)]),
        out_shape=jax.ShapeDtypeStruct((nb // group, group * SUBLANES, total), F32),
        compiler_params=_cparams(("arbitrary", "arbitrary")),
    )(pt, qall3, wcol, proj["small_t"], *([cit] * SCORE_PAGES)).reshape(nb, SUBLANES, total)

    kw = ATTN_PAGES * PAGE_SIZE
    kv_specs = [pl.BlockSpec((None, ATT_WIDTH, PAGE_SIZE), page(r, ATTN_PAGES)) for r in range(ATTN_PAGES)]
    return pl.pallas_call(
        functools.partial(_sattn_kernel, past=past),
        grid_spec=pltpu.PrefetchScalarGridSpec(
            num_scalar_prefetch=1, grid=(nb, n_pages // ATTN_PAGES),
            in_specs=[pl.BlockSpec(memory_space=pltpu.SMEM),
                      pl.BlockSpec((None, ROWS_Q, ATT_WIDTH), seq3),
                      pl.BlockSpec((None, SUBLANES, kw), lambda b, j, pt_ref: (b, 0, j)),
                      pl.BlockSpec((None, SUBLANES, LANES), lambda b, j, pt_ref: (b, 0, past // LANES)),
                      pl.BlockSpec((CHUNK, ATT_WIDTH), trow), pl.BlockSpec((CHUNK, ATT_WIDTH), trow)]
                     + kv_specs * 2,
            out_specs=pl.BlockSpec((None, SUBLANES, ATT_WIDTH), seq3),
            scratch_shapes=[pltpu.VMEM((ROWS_Q, LANES), F32), pltpu.VMEM((ROWS_Q, LANES), F32),
                            pltpu.VMEM((ROWS_Q, ATT_WIDTH), F32)]),
        out_shape=jax.ShapeDtypeStruct((nb, SUBLANES, ATT_WIDTH), F32),
        compiler_params=_cparams(("arbitrary", "arbitrary")),
    )(pt, rel_bias, qbd, madd, madd, proj["k_b"], proj["v"], *([ckt] * ATTN_PAGES), *([cvt] * ATTN_PAGES))


TM_PROJ = 256
TQ_PROMPT = 256
PROJ_NAMES = ("qe_t", "qo_t", "k3", "k_b", "v", "v3", "vp_t", "ga", "z", "xbc", "qi_t", "small", "small_t")


def _layer_weights(g_pre, w_in, conv_w, conv_b, dt_bias, a_log, d_skip, g_ssm, w_out, g_post):
    offs = np.cumsum([0, ATT_WIDTH, ATT_WIDTH, ATT_WIDTH, ATT_WIDTH, QI_WIDTH, D_IDX, N_IDX_HEADS,
                      SSM_WIDTH, CONV_CH, SSM_HEADS])
    q, k, v, ga, qi, ki, wi, z, xbc, dt = [w_in[:, offs[n]:offs[n + 1]] for n in range(10)]
    pad = jnp.zeros((D_MODEL, LANES - D_IDX - N_IDX_HEADS - SSM_HEADS), F32)
    wqi_hi, wqi_lo = _split(qi.T)
    ws_hi, ws_lo = _split(jnp.concatenate([ki, wi, dt, pad], axis=1))
    return dict(g_pre=g_pre[None, :], w_rows=jnp.concatenate([k, v, ga, z, xbc], axis=1).astype(BF16),
                wq_t=q.T.astype(BF16), wqi_t_hi=wqi_hi, wqi_t_lo=wqi_lo, ws_hi=ws_hi,
                ws_both=jnp.concatenate([ws_hi, ws_lo], axis=1),
                conv_w=conv_w, conv_b=conv_b, dt_bias=dt_bias, a_log=a_log, d_skip=d_skip, g_ssm=g_ssm,
                w_top=w_out[:ATT_WIDTH].astype(BF16), w_bot=w_out[ATT_WIDTH:].astype(BF16), g_post=g_post[None, :])


def _mixer(x, lw, pos_off, t_valid, h0, c0, attn_fn):
    nb, s, _ = x.shape
    tm = min(TM_PROJ, s)
    x2d = x.reshape(nb * s, D_MODEL)
    proj = dict(zip(PROJ_NAMES, _inproj(x2d, lw, _rope_tables(s, pos_off), tm)))
    att = attn_fn(proj)
    y, h_final = _ssd_out(proj, att, x2d, lw, h0, c0, nb, s, t_valid)
    r = lambda a: a.reshape(nb, s, -1)[:, :t_valid]
    heads = lambda a: a.reshape(nb, s, N_HEADS_A, HEAD_DIM)[:, :t_valid]
    conv_state = r(proj["xbc"])[:, t_valid - (CONV_W - 1):]
    return (r(y), heads(proj["k3"]), heads(proj["v3"]), r(proj["small"])[..., :D_IDX],
            h_final.reshape(nb, SSM_HEADS, SSM_HEAD_DIM, D_STATE), conv_state)


def kernel(x_prompt, x_sample, cache_k, cache_v, cache_kidx, state_ssm, state_conv, page_table, g_pre, w_in, conv_w, conv_b, dt_bias, a_log, d_skip, g_ssm, w_out, g_post, rel_bias):
    depth = w_in.shape[0]
    bp, sp, _ = x_prompt.shape
    bs, ts, _ = x_sample.shape
    past = page_table.shape[1] * PAGE_SIZE
    assert ts <= SUBLANES and ts >= CONV_W - 1 and sp % max(TQ_PROMPT, KI3_BUILD_ROWS) == 0
    assert page_table.shape[1] % SCORE_PAGES == 0 and page_table.shape[1] % ATTN_PAGES == 0
    state_rows = SSM_HEADS * SSM_HEAD_DIM
    bias_t = _bias_tiles(rel_bias, TQ_PROMPT)

    yp = x_prompt
    ys = jnp.pad(x_sample, ((0, 0), (0, CHUNK - ts), (0, 0)))
    outs_p, outs_s = [], []
    for l in range(depth):
        lw = _layer_weights(g_pre[l], w_in[l], conv_w[l], conv_b[l], dt_bias[l], a_log[l], d_skip[l], g_ssm[l],
                            w_out[l], g_post[l])

        def prompt_attn(proj):
            return _prompt_attention(proj, bias_t, bp, sp, TQ_PROMPT)

        def sample_attn(proj, layer=l):
            att8 = _sample_attention(page_table, rel_bias, proj, cache_k[layer], cache_v[layer], cache_kidx[layer],
                                     bs, ts)
            return jnp.pad(att8, ((0, 0), (0, CHUNK - SUBLANES), (0, 0))).reshape(bs * CHUNK, ATT_WIDTH)

        op = _mixer(yp, lw, 0, sp, jnp.zeros((bp, state_rows, D_STATE), F32),
                    jnp.zeros((bp, SUBLANES, CONV_CH), F32), prompt_attn)
        c0 = jnp.pad(state_conv[l], ((0, 0), (SUBLANES - (CONV_W - 1), 0), (0, 0)))
        os_ = _mixer(ys, lw, past, ts, state_ssm[l].reshape(bs, state_rows, D_STATE), c0, sample_attn)
        yp = op[0]
        ys = jnp.pad(os_[0], ((0, 0), (0, CHUNK - ts), (0, 0)))
        outs_p.append(op[1:])
        outs_s.append(os_[1:])
    stack = lambda outs, n: jnp.stack([o[n] for o in outs])
    return (yp, ys[:, :ts], *[stack(outs_p, n) for n in range(5)], *[stack(outs_s, n) for n in range(5)])
```
